```python
import jax, jax.numpy as jnp
from jax import lax
import numpy as np

D_MODEL = 1024
BATCH = 2
SEQ = 16384
DEPTH = 1
DEC_BATCH = 16
DEC_SEQ = 32
PAST_LEN = 1024

CHUNK = 64
CONV_W = 512
CONV_K = 3
POOL_W = 512
POOL_WINDOWS = (2, 4, 8, 16)
POOL_GROUPS = len(POOL_WINDOWS)
POOL_GW = POOL_W // POOL_GROUPS
POOL_STATE = max(POOL_WINDOWS) - 1
N_MEM = 256
MEM_HEADS = 4
MEM_HD = 128
ATT_W = MEM_HEADS * MEM_HD
N_BRANCH = 3
IN_SIZES = (CONV_W, CONV_W, CONV_W, POOL_W, ATT_W, N_BRANCH * D_MODEL)
IN_COLS = sum(IN_SIZES)
IN_SPLITS = tuple(int(s) for s in np.cumsum(IN_SIZES)[:-1])
PEER_HEADS = 8
PEER_KEYS = 128
PEER_EXPERTS = PEER_KEYS * PEER_KEYS
PEER_TOPK = 16
PEER_QD = 256
PEER_HALF = PEER_QD // 2
PEER_BLOCK = 512
EPS = 1e-6

kernel_name = "hybrid_stream_conv_pool_mem_peer"


def rmsnorm(x, g):
    xf = x.astype(jnp.float32)
    y = xf * lax.rsqrt(jnp.mean(xf * xf, axis=-1, keepdims=True) + EPS) * g.astype(jnp.float32)
    return y.astype(x.dtype)


def causal_conv(z_ext, w, b):
    L = z_ext.shape[1] - (CONV_K - 1)
    out = z_ext[:, 0:L] * w[0]
    for k in range(1, CONV_K):
        out = out + z_ext[:, k:k + L] * w[k]
    return out + b


def multiscale_pool(u_ext, start_pos, w_grp, scale):
    Bn, Le, _ = u_ext.shape
    L = Le - POOL_STATE
    uf = u_ext.astype(jnp.float32).reshape(Bn, Le, POOL_GROUPS, POOL_GW)
    cs = jnp.concatenate([jnp.zeros((Bn, 1, POOL_GROUPS, POOL_GW), jnp.float32), jnp.cumsum(uf, axis=1)], axis=1)
    pos = start_pos + jnp.arange(L)
    base = POOL_STATE + 1
    means = []
    for g, w in enumerate(POOL_WINDOWS):
        s = cs[:, base:base + L, g] - cs[:, base - w:base - w + L, g]
        cnt = jnp.minimum(pos + 1, w).astype(jnp.float32)
        means.append(s / cnt[None, :, None])
    mean = jnp.stack(means, axis=2)
    d = (mean - uf[:, POOL_STATE:]).astype(w_grp.dtype)
    y = jnp.einsum('blgc,gcd->blgd', d, w_grp).reshape(Bn, L, POOL_W)
    return y * scale


def mem_kv(mem, g_mem, w_mk, w_mv):
    Bn = mem.shape[0]
    m = rmsnorm(mem, g_mem)
    k = (m @ w_mk).reshape(Bn, N_MEM, MEM_HEADS, MEM_HD)
    v = (m @ w_mv).reshape(Bn, N_MEM, MEM_HEADS, MEM_HD)
    return k, v


def mem_attend(q, k, v):
    Bn, L = q.shape[0], q.shape[1]
    s = jnp.einsum('blhd,bmhd->bhlm', q, k).astype(jnp.float32) * (MEM_HD ** -0.5)
    p = jax.nn.softmax(s, axis=-1).astype(v.dtype)
    return jnp.einsum('bhlm,bmhd->blhd', p, v).reshape(Bn, L, ATT_W)


def peer_block(xt, w_pq, sub_keys, u_tab, v_tab):
    T = xt.shape[0]
    q = (xt @ w_pq).reshape(T, PEER_HEADS, 2, PEER_HALF).astype(jnp.float32)
    s = jnp.einsum('thpc,pkc->thpk', q, sub_keys.astype(jnp.float32))
    s1, i1 = lax.top_k(s[:, :, 0], PEER_TOPK)
    s2, i2 = lax.top_k(s[:, :, 1], PEER_TOPK)
    cand = (s1[..., :, None] + s2[..., None, :]).reshape(T, PEER_HEADS, PEER_TOPK * PEER_TOPK)
    sc, ci = lax.top_k(cand, PEER_TOPK)
    e = (jnp.take_along_axis(i1, ci // PEER_TOPK, axis=-1) * PEER_KEYS
         + jnp.take_along_axis(i2, ci % PEER_TOPK, axis=-1))
    g = jax.nn.softmax(sc, axis=-1)
    u = u_tab[e]
    hval = jax.nn.gelu(jnp.einsum('td,thkd->thk', xt, u).astype(jnp.float32), approximate=False)
    wgt = (g * hval).astype(xt.dtype)
    return jnp.einsum('thk,thkd->td', wgt, v_tab[e])


def peer(x, w_pq, sub_keys, u_tab, v_tab):
    Bn, L, D = x.shape
    T = Bn * L
    nblk = -(-T // PEER_BLOCK)
    pad = nblk * PEER_BLOCK - T
    xt = jnp.pad(x.reshape(T, D), ((0, pad), (0, 0))).reshape(nblk, PEER_BLOCK, D)
    out = lax.map(lambda xb: peer_block(xb, w_pq, sub_keys, u_tab, v_tab), xt)
    return out.reshape(nblk * PEER_BLOCK, D)[:T].reshape(Bn, L, D)


def layer(x, conv_prev, pool_prev, start_pos, mk, mv,
          g_mix, w_in, conv_w, conv_b, pool_w, pool_scale, w_bc, w_bp, w_ba, gate_b, w_o,
          g_ffn, peer_wq, peer_keys, peer_u, peer_v):
    Bn, L, D = x.shape
    h = rmsnorm(x, g_mix)
    proj = h @ w_in
    xc, bg, cg, up, q, gl = jnp.split(proj, IN_SPLITS, axis=-1)
    z_ext = jnp.concatenate([conv_prev, cg * xc], axis=1)
    ya = (bg * causal_conv(z_ext, conv_w, conv_b)) @ w_bc
    new_conv = z_ext[:, -(CONV_K - 1):]
    u_ext = jnp.concatenate([pool_prev, up], axis=1)
    yb = multiscale_pool(u_ext, start_pos, pool_w, pool_scale) @ w_bp
    new_pool = u_ext[:, -POOL_STATE:]
    yc = mem_attend(q.reshape(Bn, L, MEM_HEADS, MEM_HD), mk, mv) @ w_ba
    gates = jax.nn.sigmoid((gl + gate_b).astype(jnp.float32)).astype(x.dtype).reshape(Bn, L, N_BRANCH, D)
    merged = gates[:, :, 0] * ya + gates[:, :, 1] * yb + gates[:, :, 2] * yc
    x = x + merged @ w_o
    x = x + peer(rmsnorm(x, g_ffn), peer_wq, peer_keys, peer_u, peer_v)
    return x, new_conv, new_pool


def setup_inputs(seed: int = 0) -> dict:
    key = jax.random.key(seed)
    ks = jax.random.split(key, 32)
    f = jnp.float32
    nrm = lambda k, shape, s: jax.random.normal(k, shape, f) * s
    return {
        "x_prompt": nrm(ks[0], (BATCH, SEQ, D_MODEL), 1.0),
        "x_sample": nrm(ks[1], (DEC_BATCH, DEC_SEQ, D_MODEL), 1.0),
        "mem_prompt": nrm(ks[2], (BATCH, N_MEM, D_MODEL), 1.0),
        "cache_conv": nrm(ks[3], (DEPTH, DEC_BATCH, CONV_K - 1, CONV_W), 1.0),
        "cache_pool": nrm(ks[4], (DEPTH, DEC_BATCH, POOL_STATE, POOL_W), 1.0),
        "cache_mem_k": nrm(ks[5], (DEPTH, DEC_BATCH, N_MEM, MEM_HEADS, MEM_HD), 1.0),
        "cache_mem_v": nrm(ks[6], (DEPTH, DEC_BATCH, N_MEM, MEM_HEADS, MEM_HD), 1.0),
        "g_mix": 1.0 + nrm(ks[7], (DEPTH, D_MODEL), 0.01),
        "w_in": nrm(ks[8], (DEPTH, D_MODEL, IN_COLS), D_MODEL ** -0.5),
        "conv_w": nrm(ks[9], (DEPTH, CONV_K, CONV_W), CONV_K ** -0.5),
        "conv_b": nrm(ks[10], (DEPTH, CONV_W), 0.01),
        "pool_w": nrm(ks[11], (DEPTH, POOL_GROUPS, POOL_GW, POOL_GW), POOL_GW ** -0.5),
        "pool_scale": 1.0 + nrm(ks[12], (DEPTH, POOL_W), 0.01),
        "g_mem": 1.0 + nrm(ks[13], (DEPTH, D_MODEL), 0.01),
        "w_mk": nrm(ks[14], (DEPTH, D_MODEL, ATT_W), D_MODEL ** -0.5),
        "w_mv": nrm(ks[15], (DEPTH, D_MODEL, ATT_W), D_MODEL ** -0.5),
        "w_bc": nrm(ks[16], (DEPTH, CONV_W, D_MODEL), CONV_W ** -0.5),
        "w_bp": nrm(ks[17], (DEPTH, POOL_W, D_MODEL), POOL_W ** -0.5),
        "w_ba": nrm(ks[18], (DEPTH, ATT_W, D_MODEL), ATT_W ** -0.5),
        "gate_b": nrm(ks[19], (DEPTH, N_BRANCH * D_MODEL), 0.01),
        "w_o": nrm(ks[20], (DEPTH, D_MODEL, D_MODEL), D_MODEL ** -0.5),
        "g_ffn": 1.0 + nrm(ks[21], (DEPTH, D_MODEL), 0.01),
        "peer_wq": nrm(ks[22], (DEPTH, D_MODEL, PEER_HEADS * PEER_QD), D_MODEL ** -0.5),
        "peer_keys": nrm(ks[23], (DEPTH, 2, PEER_KEYS, PEER_HALF), PEER_HALF ** -0.5),
        "peer_u": nrm(ks[24], (DEPTH, PEER_EXPERTS, D_MODEL), D_MODEL ** -0.5),
        "peer_v": nrm(ks[25], (DEPTH, PEER_EXPERTS, D_MODEL), 0.5),
        "g_final": 1.0 + nrm(ks[26], (D_MODEL,), 0.01),
    }


def reference(x_prompt, x_sample, mem_prompt, cache_conv, cache_pool, cache_mem_k, cache_mem_v,
              g_mix, w_in, conv_w, conv_b, pool_w, pool_scale, g_mem, w_mk, w_mv,
              w_bc, w_bp, w_ba, gate_b, w_o, g_ffn, peer_wq, peer_keys, peer_u, peer_v, g_final):
    xp, xs = x_prompt, x_sample
    conv_p, pool_p, mk_p, mv_p, conv_s, pool_s = [], [], [], [], [], []
    for l in range(DEPTH):
        lw = (g_mix[l], w_in[l], conv_w[l], conv_b[l], pool_w[l], pool_scale[l], w_bc[l], w_bp[l],
              w_ba[l], gate_b[l], w_o[l], g_ffn[l], peer_wq[l], peer_keys[l], peer_u[l], peer_v[l])
        mk, mv = mem_kv(mem_prompt, g_mem[l], w_mk[l], w_mv[l])
        zc = jnp.zeros((xp.shape[0], CONV_K - 1, CONV_W), xp.dtype)
        zp = jnp.zeros((xp.shape[0], POOL_STATE, POOL_W), xp.dtype)
        xp, cp, pp = layer(xp, zc, zp, 0, mk, mv, *lw)
        xs, cs, ps = layer(xs, cache_conv[l], cache_pool[l], PAST_LEN, cache_mem_k[l], cache_mem_v[l], *lw)
        conv_p.append(cp); pool_p.append(pp); mk_p.append(mk); mv_p.append(mv)
        conv_s.append(cs); pool_s.append(ps)
    y_prompt = rmsnorm(xp, g_final)
    y_sample = rmsnorm(xs, g_final)
    return (y_prompt, y_sample, jnp.stack(conv_p), jnp.stack(pool_p), jnp.stack(mk_p), jnp.stack(mv_p),
            jnp.stack(conv_s), jnp.stack(pool_s))
```

```python
import functools

import jax
import jax.numpy as jnp
from jax import lax
from jax.experimental import pallas as pl
from jax.experimental.pallas import tpu as pltpu

F32 = jnp.float32
BF16 = jnp.bfloat16
EPS = 1e-6

CONV_K = 3
POOL_WINDOWS = (2, 4, 8, 16)
POOL_STATE = max(POOL_WINDOWS) - 1
MEM_HEADS = 4
PEER_HEADS = 8
PEER_TOPK = 16

SUBLANES = 8
LANES = 128
MIXER_ROWS = 256
ROUTE_ROWS = 256
EXPERT_ROWS = 128
GATHER_SLOTS = 4
VMEM_LIMIT = 56 * 1024 * 1024


def _rms(x, g):
    return x * lax.rsqrt(jnp.mean(x * x, axis=-1, keepdims=True) + EPS) * g


def _dot(a, b):
    return jnp.dot(a, b, preferred_element_type=F32)


def _dot_nt(a, b):
    return lax.dot_general(a, b, (((1,), (1,)), ((), ())), preferred_element_type=F32)


def _resident(shape):
    zeros = (0,) * len(shape)
    return pl.BlockSpec(shape, lambda *_: zeros, pipeline_mode=pl.Buffered(1))


def _memkv_body(mem_ref, g_ref, wk_ref, wv_ref, k_ref, v_ref):
    m = _rms(mem_ref[...], g_ref[...]).astype(BF16)
    k_ref[...] = _dot(m, wk_ref[...])
    v_ref[...] = _dot(m, wv_ref[...])


def _mem_kv(mem, g_mem, w_mk, w_mv):
    bn, n_mem, d = mem.shape
    att_w = w_mk.shape[1]
    k, v = pl.pallas_call(
        _memkv_body,
        out_shape=[jax.ShapeDtypeStruct((bn * n_mem, att_w), F32)] * 2,
        name="mem_kv",
    )(mem.reshape(bn * n_mem, d), g_mem.reshape(1, d), w_mk.astype(BF16), w_mv.astype(BF16))
    return k.reshape(bn, n_mem, att_w), v.reshape(bn, n_mem, att_w)


def _mixer_body(x_ref, cprev_ref, pprev_ref, mk_ref, mv_ref, gmix_ref, win_ref, convw_ref, convb_ref, poolw_ref,
                pscale_ref, wbc_ref, wbp_ref, wba_ref, gateb_ref, wo_ref,
                x2_ref, nconv_ref, npool_ref, zbuf, ubuf, *, rows, start_pos, conv_w, pool_w, att_w):
    s = pl.program_id(1)
    d_model = x_ref.shape[-1]
    z0 = SUBLANES
    u0 = 2 * SUBLANES

    @pl.when(s == 0)
    def _():
        zbuf[z0 - (CONV_K - 1):z0, :] = cprev_ref[0]
        ubuf[u0 - POOL_STATE:u0, :] = pprev_ref[0]

    x = x_ref[0]
    hb = _rms(x, gmix_ref[...]).astype(BF16)

    c0 = 0
    pa = _dot(hb, win_ref[:, c0:c0 + 3 * conv_w])
    xc, bg, cg = pa[:, :conv_w], pa[:, conv_w:2 * conv_w], pa[:, 2 * conv_w:]
    zbuf[z0:z0 + rows, :] = cg * xc
    cw = convw_ref[...]
    conv = zbuf[z0 - 2:z0 - 2 + rows, :] * cw[0:1]
    for k in range(1, CONV_K):
        conv = conv + zbuf[z0 - 2 + k:z0 - 2 + k + rows, :] * cw[k:k + 1]
    conv = conv + convb_ref[...]
    ya = _dot((bg * conv).astype(BF16), wbc_ref[...])
    last_z = zbuf[z0 + rows - (CONV_K - 1):z0 + rows, :]
    nconv_ref[0] = last_z
    zbuf[z0 - (CONV_K - 1):z0, :] = last_z
    c0 += 3 * conv_w

    up = _dot(hb, win_ref[:, c0:c0 + pool_w])
    ubuf[u0:u0 + rows, :] = up
    pos = start_pos + s * rows + lax.broadcasted_iota(jnp.int32, (rows, 1), 0)
    gw = pool_w // len(POOL_WINDOWS)
    ys = []
    for g, w in enumerate(POOL_WINDOWS):
        cur = up[:, g * gw:(g + 1) * gw]
        acc = cur
        for k in range(1, w):
            acc = acc + ubuf[u0 - k:u0 - k + rows, g * gw:(g + 1) * gw]
        cnt = jnp.minimum(pos + 1, w).astype(F32)
        ys.append(_dot((acc / cnt - cur).astype(BF16), poolw_ref[g]))
    yb = _dot((jnp.concatenate(ys, axis=-1) * pscale_ref[...]).astype(BF16), wbp_ref[...])
    last_u = ubuf[u0 + rows - POOL_STATE:u0 + rows, :]
    npool_ref[0] = last_u
    ubuf[u0 - POOL_STATE:u0, :] = last_u
    c0 += pool_w

    q = _dot(hb, win_ref[:, c0:c0 + att_w])
    kb = mk_ref[0].astype(BF16)
    vb = mv_ref[0].astype(BF16)
    hd = att_w // MEM_HEADS
    heads = []
    for h in range(MEM_HEADS):
        sc = _dot_nt(q[:, h * hd:(h + 1) * hd].astype(BF16), kb[:, h * hd:(h + 1) * hd]) * (hd ** -0.5)
        e = jnp.exp(sc - jnp.max(sc, axis=-1, keepdims=True))
        p = e / jnp.sum(e, axis=-1, keepdims=True)
        heads.append(_dot(p.astype(BF16), vb[:, h * hd:(h + 1) * hd]))
    yc = _dot(jnp.concatenate(heads, axis=-1).astype(BF16), wba_ref[...])
    c0 += att_w

    merged = None
    for i, y in enumerate((ya, yb, yc)):
        gl = _dot(hb, win_ref[:, c0 + i * d_model:c0 + (i + 1) * d_model]) + gateb_ref[:, i * d_model:(i + 1) * d_model]
        term = (1.0 / (1.0 + jnp.exp(-gl))) * y
        merged = term if merged is None else merged + term
    x2_ref[0] = x + _dot(merged.astype(BF16), wo_ref[...])


def _mixer(x, conv_prev, pool_prev, start_pos, mk, mv, lw):
    bn, seq, d = x.shape
    conv_w, pool_w = conv_prev.shape[-1], pool_prev.shape[-1]
    n_mem, att_w = mk.shape[1], mk.shape[2]
    rows = min(MIXER_ROWS, seq)
    assert seq % rows == 0 and rows % SUBLANES == 0 and rows >= POOL_STATE
    in_cols = lw["w_in"].shape[1]
    per_b = lambda shape: pl.BlockSpec((1,) + shape, lambda b, s: (b, 0, 0))
    body = functools.partial(_mixer_body, rows=rows, start_pos=start_pos, conv_w=conv_w, pool_w=pool_w, att_w=att_w)
    return pl.pallas_call(
        body,
        grid=(bn, seq // rows),
        in_specs=[
            pl.BlockSpec((1, rows, d), lambda b, s: (b, s, 0)),
            per_b((CONV_K - 1, conv_w)), per_b((POOL_STATE, pool_w)), per_b((n_mem, att_w)), per_b((n_mem, att_w)),
            _resident((1, d)), _resident((d, in_cols)), _resident((CONV_K, conv_w)), _resident((1, conv_w)),
            _resident(lw["pool_w"].shape), _resident((1, pool_w)), _resident((conv_w, d)), _resident((pool_w, d)),
            _resident((att_w, d)), _resident((1, 3 * d)), _resident((d, d)),
        ],
        out_specs=[
            pl.BlockSpec((1, rows, d), lambda b, s: (b, s, 0)),
            per_b((CONV_K - 1, conv_w)), per_b((POOL_STATE, pool_w)),
        ],
        out_shape=[
            jax.ShapeDtypeStruct((bn, seq, d), F32),
            jax.ShapeDtypeStruct((bn, CONV_K - 1, conv_w), F32),
            jax.ShapeDtypeStruct((bn, POOL_STATE, pool_w), F32),
        ],
        scratch_shapes=[pltpu.VMEM((SUBLANES + rows, conv_w), F32), pltpu.VMEM((2 * SUBLANES + rows, pool_w), F32)],
        compiler_params=pltpu.CompilerParams(dimension_semantics=("arbitrary", "arbitrary"), vmem_limit_bytes=VMEM_LIMIT),
        name="mixer",
    )(x, conv_prev, pool_prev, mk, mv, lw["g_mix"], lw["w_in"], lw["conv_w"], lw["conv_b"], lw["pool_w"],
      lw["pool_scale"], lw["w_bc"], lw["w_bp"], lw["w_ba"], lw["gate_b"], lw["w_o"])


def _top16(v, payload_fn):
    n, t = v.shape
    rows = lax.broadcasted_iota(jnp.int32, (n, t), 0)
    out_rows = lax.broadcasted_iota(jnp.int32, (PEER_TOPK, t), 0)
    vals = jnp.zeros((PEER_TOPK, t), F32)
    pay = jnp.zeros((PEER_TOPK, t), jnp.int32)
    for k in range(PEER_TOPK):
        m = jnp.max(v, axis=0, keepdims=True)
        am = jnp.min(jnp.where(v == m, rows, n), axis=0, keepdims=True)
        vals = jnp.where(out_rows == k, m, vals)
        pay = jnp.where(out_rows == k, payload_fn(am), pay)
        v = jnp.where(rows == am, -jnp.inf, v)
    return vals, pay


def _pick16(table, sel):
    rows = lax.broadcasted_iota(jnp.int32, table.shape, 0)
    return jnp.sum(jnp.where(rows == sel, table, 0), axis=0, keepdims=True)


def _route_body(x2_ref, gffn_ref, wq_ref, keys_ref, xn_ref, ids_ref, gate_ref, *, n_keys):
    xn = _rms(x2_ref[...], gffn_ref[...])
    xn_ref[...] = xn
    q = _dot(xn.astype(BF16), wq_ref[...])
    half = keys_ref.shape[-1]
    for h in range(PEER_HEADS):
        tops = []
        for p in range(2):
            qhp = q[:, (2 * h + p) * half:(2 * h + p + 1) * half].astype(BF16)
            tops.append(_top16(_dot_nt(keys_ref[p], qhp), lambda am: am))
        (s1, i1), (s2, i2) = tops
        cand = jnp.concatenate([s1[a:a + 1, :] + s2 for a in range(PEER_TOPK)], axis=0)
        sc, e = _top16(cand, lambda c: _pick16(i1, c // PEER_TOPK) * n_keys + _pick16(i2, c % PEER_TOPK))
        ex = jnp.exp(sc - sc[0:1, :])
        ids_ref[h * PEER_TOPK:(h + 1) * PEER_TOPK, :] = e
        gate_ref[h * PEER_TOPK:(h + 1) * PEER_TOPK, :] = ex / jnp.sum(ex, axis=0, keepdims=True)


def _route(x2, g_ffn, wq, keys):
    t, d = x2.shape
    rows = ROUTE_ROWS
    assert t % rows == 0
    picks = PEER_HEADS * PEER_TOPK
    return pl.pallas_call(
        functools.partial(_route_body, n_keys=keys.shape[1]),
        grid=(t // rows,),
        in_specs=[pl.BlockSpec((rows, d), lambda i: (i, 0)), _resident((1, d)), _resident(wq.shape), _resident(keys.shape)],
        out_specs=[pl.BlockSpec((rows, d), lambda i: (i, 0)), pl.BlockSpec((picks, rows), lambda i: (0, i)),
                   pl.BlockSpec((picks, rows), lambda i: (0, i))],
        out_shape=[jax.ShapeDtypeStruct((t, d), F32), jax.ShapeDtypeStruct((picks, t), jnp.int32),
                   jax.ShapeDtypeStruct((picks, t), F32)],
        compiler_params=pltpu.CompilerParams(dimension_semantics=("arbitrary",), vmem_limit_bytes=VMEM_LIMIT),
        name="route",
    )(x2, g_ffn, wq, keys)


def _gelu(h):
    return 0.5 * h * (1.0 + lax.erf(h * (2.0 ** -0.5)))


def _experts_body(ids_hbm, gate_ref, xn_ref, x2_ref, gfin_ref, uv_hbm, y_ref, buf, sem, ids_smem, ids_sem,
                  *, rows, picks, steps, final_norm):
    i = pl.program_id(0)
    half = i % 2
    n = rows * picks
    d = xn_ref.shape[-1]

    def ids_copy(step, which):
        return pltpu.make_async_copy(ids_hbm.at[step], ids_smem.at[pl.ds(which * n, n)], ids_sem.at[which])

    @pl.when(i == 0)
    def _():
        ids_copy(0, 0).start()

    ids_copy(i, half).wait()

    @pl.when(i + 1 < steps)
    def _():
        ids_copy(i + 1, 1 - half).start()

    def row_copy(e, slot, j):
        return pltpu.make_async_copy(uv_hbm.at[e], buf.at[slot, pl.ds(j, 1)], sem.at[slot])

    def issue(t, slot):
        off = half * n + t * picks
        for j in range(picks):
            row_copy(ids_smem[off + j], slot, j).start()

    def wait_slot(slot):
        pltpu.make_async_copy(buf.at[slot], buf.at[slot], sem.at[slot]).wait()

    for s in range(GATHER_SLOTS - 1):
        issue(s, s)

    lane = lax.broadcasted_iota(jnp.int32, (picks, rows), 1)

    def token(t, carry):
        slot = t % GATHER_SLOTS
        ahead = t + GATHER_SLOTS - 1

        @pl.when(ahead < rows)
        def _():
            issue(ahead, ahead % GATHER_SLOTS)

        wait_slot(slot)
        x = xn_ref[pl.ds(t, 1), :]
        h = jnp.sum(buf[slot, :, 0:d] * x, axis=-1, keepdims=True)
        g = jnp.sum(jnp.where(lane == t, gate_ref[...], 0.0), axis=-1, keepdims=True)
        o = jnp.sum((g * _gelu(h)) * buf[slot, :, d:2 * d], axis=0, keepdims=True)
        y = x2_ref[pl.ds(t, 1), :] + o
        y_ref[pl.ds(t, 1), :] = _rms(y, gfin_ref[...]) if final_norm else y
        return carry

    lax.fori_loop(0, rows, token, 0)


def _experts(ids_t, gate_t, xn, x2, g_final, uv, final_norm):
    picks, t = ids_t.shape
    d = xn.shape[1]
    rows = EXPERT_ROWS
    assert t % rows == 0 and rows >= GATHER_SLOTS
    steps = t // rows
    ids = ids_t.T.reshape(steps, rows * picks)
    body = functools.partial(_experts_body, rows=rows, picks=picks, steps=steps, final_norm=final_norm)
    return pl.pallas_call(
        body,
        grid=(steps,),
        in_specs=[
            pl.BlockSpec(memory_space=pl.ANY),
            pl.BlockSpec((picks, rows), lambda i: (0, i)),
            pl.BlockSpec((rows, d), lambda i: (i, 0)),
            pl.BlockSpec((rows, d), lambda i: (i, 0)),
            _resident((1, d)),
            pl.BlockSpec(memory_space=pl.ANY),
        ],
        out_specs=pl.BlockSpec((rows, d), lambda i: (i, 0)),
        out_shape=jax.ShapeDtypeStruct((t, d), F32),
        scratch_shapes=[
            pltpu.VMEM((GATHER_SLOTS, picks, 2 * d), F32),
            pltpu.SemaphoreType.DMA((GATHER_SLOTS,)),
            pltpu.SMEM((2 * rows * picks,), jnp.int32),
            pltpu.SemaphoreType.DMA((2,)),
        ],
        compiler_params=pltpu.CompilerParams(dimension_semantics=("arbitrary",), vmem_limit_bytes=VMEM_LIMIT),
        name="experts",
    )(ids, gate_t, xn, x2, g_final, uv)


def _layer(x, conv_prev, pool_prev, start_pos, mk, mv, lw, g_final, final_norm):
    bn, seq, d = x.shape
    x2, new_conv, new_pool = _mixer(x, conv_prev, pool_prev, start_pos, mk, mv, lw)
    x2 = x2.reshape(bn * seq, d)
    xn, ids_t, gate_t = _route(x2, lw["g_ffn"], lw["peer_wq"], lw["peer_keys"])
    y = _experts(ids_t, gate_t, xn, x2, g_final, lw["peer_uv"], final_norm)
    return y.reshape(bn, seq, d), new_conv, new_pool


def kernel(x_prompt, x_sample, mem_prompt, cache_conv, cache_pool, cache_mem_k, cache_mem_v, g_mix, w_in, conv_w, conv_b, pool_w, pool_scale, g_mem, w_mk, w_mv, w_bc, w_bp, w_ba, gate_b, w_o, g_ffn, peer_wq, peer_keys, peer_u, peer_v, g_final):
    depth, d = g_mix.shape
    xp, xs = x_prompt, x_sample
    bp, bs = xp.shape[0], xs.shape[0]
    n_exp = peer_u.shape[1]
    gfin = g_final.reshape(1, d)
    conv_p, pool_p, mk_p, mv_p, conv_s, pool_s = [], [], [], [], [], []
    for l in range(depth):
        lw = dict(
            g_mix=g_mix[l].reshape(1, d), w_in=w_in[l].astype(BF16), conv_w=conv_w[l], conv_b=conv_b[l].reshape(1, -1),
            pool_w=pool_w[l].astype(BF16), pool_scale=pool_scale[l].reshape(1, -1), w_bc=w_bc[l].astype(BF16),
            w_bp=w_bp[l].astype(BF16), w_ba=w_ba[l].astype(BF16), gate_b=gate_b[l].reshape(1, -1), w_o=w_o[l].astype(BF16),
            g_ffn=g_ffn[l].reshape(1, d), peer_wq=peer_wq[l].astype(BF16), peer_keys=peer_keys[l].astype(BF16),
            peer_uv=jnp.concatenate([peer_u[l], peer_v[l]], axis=1).reshape(n_exp, 1, 2 * d),
        )
        last = l == depth - 1
        mk, mv = _mem_kv(mem_prompt, g_mem[l], w_mk[l], w_mv[l])
        zc = jnp.zeros((bp, CONV_K - 1, conv_w.shape[-1]), xp.dtype)
        zp = jnp.zeros((bp, POOL_STATE, pool_scale.shape[-1]), xp.dtype)
        xp, cp, pp = _layer(xp, zc, zp, 0, mk, mv, lw, gfin, last)
        n_mem = cache_mem_k.shape[2]
        past = 1024
        xs, cs, ps = _layer(xs, cache_conv[l], cache_pool[l], past, cache_mem_k[l].reshape(bs, n_mem, -1),
                            cache_mem_v[l].reshape(bs, n_mem, -1), lw, gfin, last)
        heads_shape = (bp, n_mem) + cache_mem_k.shape[3:]
        conv_p.append(cp); pool_p.append(pp); mk_p.append(mk.reshape(heads_shape)); mv_p.append(mv.reshape(heads_shape))
        conv_s.append(cs); pool_s.append(ps)
    return (xp, xs, jnp.stack(conv_p), jnp.stack(pool_p), jnp.stack(mk_p), jnp.stack(mv_p),
            jnp.stack(conv_s), jnp.stack(pool_s))
```

```python
import functools

import jax
import jax.numpy as jnp
from jax import lax
from jax.experimental import pallas as pl
from jax.experimental.pallas import tpu as pltpu

F32 = jnp.float32
BF16 = jnp.bfloat16
EPS = 1e-6

CONV_K = 3
POOL_WINDOWS = (2, 4, 8, 16)
POOL_STATE = max(POOL_WINDOWS) - 1
PAST_LEN = 1024
MEM_HEADS = 4
PEER_HEADS = 8
PEER_TOPK = 16

SUBLANES = 8
LANES = 128
MIXER_ROWS = 256
ROUTE_ROWS = 256
EXPERT_ROWS = 128
GATHER_SLOTS = 4
VMEM_LIMIT = 56 * 1024 * 1024


def _rms(x, g):
    return x * lax.rsqrt(jnp.mean(x * x, axis=-1, keepdims=True) + EPS) * g


def _dot(a, b):
    return jnp.dot(a, b, preferred_element_type=F32)


def _dot_nt(a, b):
    return lax.dot_general(a, b, (((1,), (1,)), ((), ())), preferred_element_type=F32)


def _resident(shape):
    zeros = (0,) * len(shape)
    return pl.BlockSpec(shape, lambda *_: zeros, pipeline_mode=pl.Buffered(1))


def _memkv_body(mem_ref, g_ref, wk_ref, wv_ref, k_ref, v_ref):
    m = _rms(mem_ref[...], g_ref[...]).astype(BF16)
    k_ref[...] = _dot(m, wk_ref[...])
    v_ref[...] = _dot(m, wv_ref[...])


def _mem_kv(mem, g_mem, w_mk, w_mv):
    bn, n_mem, d = mem.shape
    att_w = w_mk.shape[1]
    k, v = pl.pallas_call(
        _memkv_body,
        out_shape=[jax.ShapeDtypeStruct((bn * n_mem, att_w), F32)] * 2,
        name="mem_kv",
    )(mem.reshape(bn * n_mem, d), g_mem.reshape(1, d), w_mk.astype(BF16), w_mv.astype(BF16))
    return k.reshape(bn, n_mem, att_w), v.reshape(bn, n_mem, att_w)


def _mixer_body(x_ref, cprev_ref, pprev_ref, mk_ref, mv_ref, gmix_ref, win_ref, convw_ref, convb_ref, poolw_ref,
                pscale_ref, wbc_ref, wbp_ref, wba_ref, gateb_ref, wo_ref,
                x2_ref, nconv_ref, npool_ref, zbuf, ubuf, *, rows, start_pos, conv_w, pool_w, att_w):
    s = pl.program_id(1)
    d_model = x_ref.shape[-1]
    z0 = SUBLANES
    u0 = 2 * SUBLANES

    @pl.when(s == 0)
    def _():
        zbuf[z0 - (CONV_K - 1):z0, :] = cprev_ref[0]
        ubuf[u0 - POOL_STATE:u0, :] = pprev_ref[0]

    x = x_ref[0]
    hb = _rms(x, gmix_ref[...]).astype(BF16)

    c0 = 0
    pa = _dot(hb, win_ref[:, c0:c0 + 3 * conv_w])
    xc, bg, cg = pa[:, :conv_w], pa[:, conv_w:2 * conv_w], pa[:, 2 * conv_w:]
    zbuf[z0:z0 + rows, :] = cg * xc
    cw = convw_ref[...]
    conv = zbuf[z0 - 2:z0 - 2 + rows, :] * cw[0:1]
    for k in range(1, CONV_K):
        conv = conv + zbuf[z0 - 2 + k:z0 - 2 + k + rows, :] * cw[k:k + 1]
    conv = conv + convb_ref[...]
    ya = _dot((bg * conv).astype(BF16), wbc_ref[...])
    last_z = zbuf[z0 + rows - (CONV_K - 1):z0 + rows, :]
    nconv_ref[0] = last_z
    zbuf[z0 - (CONV_K - 1):z0, :] = last_z
    c0 += 3 * conv_w

    up = _dot(hb, win_ref[:, c0:c0 + pool_w])
    ubuf[u0:u0 + rows, :] = up
    pos = start_pos + s * rows + lax.broadcasted_iota(jnp.int32, (rows, 1), 0)
    gw = pool_w // len(POOL_WINDOWS)
    ys = []
    for g, w in enumerate(POOL_WINDOWS):
        cur = up[:, g * gw:(g + 1) * gw]
        acc = cur
        for k in range(1, w):
            acc = acc + ubuf[u0 - k:u0 - k + rows, g * gw:(g + 1) * gw]
        cnt = jnp.minimum(pos + 1, w).astype(F32)
        ys.append(_dot((acc / cnt - cur).astype(BF16), poolw_ref[g]))
    yb = _dot((jnp.concatenate(ys, axis=-1) * pscale_ref[...]).astype(BF16), wbp_ref[...])
    last_u = ubuf[u0 + rows - POOL_STATE:u0 + rows, :]
    npool_ref[0] = last_u
    ubuf[u0 - POOL_STATE:u0, :] = last_u
    c0 += pool_w

    q = _dot(hb, win_ref[:, c0:c0 + att_w])
    kb = mk_ref[0].astype(BF16)
    vb = mv_ref[0].astype(BF16)
    hd = att_w // MEM_HEADS
    heads = []
    for h in range(MEM_HEADS):
        sc = _dot_nt(q[:, h * hd:(h + 1) * hd].astype(BF16), kb[:, h * hd:(h + 1) * hd]) * (hd ** -0.5)
        e = jnp.exp(sc - jnp.max(sc, axis=-1, keepdims=True))
        p = e / jnp.sum(e, axis=-1, keepdims=True)
        heads.append(_dot(p.astype(BF16), vb[:, h * hd:(h + 1) * hd]))
    yc = _dot(jnp.concatenate(heads, axis=-1).astype(BF16), wba_ref[...])
    c0 += att_w

    merged = None
    for i, y in enumerate((ya, yb, yc)):
        gl = _dot(hb, win_ref[:, c0 + i * d_model:c0 + (i + 1) * d_model]) + gateb_ref[:, i * d_model:(i + 1) * d_model]
        term = (1.0 / (1.0 + jnp.exp(-gl))) * y
        merged = term if merged is None else merged + term
    x2_ref[0] = x + _dot(merged.astype(BF16), wo_ref[...])


def _mixer(x, conv_prev, pool_prev, start_pos, mk, mv, lw):
    bn, seq, d = x.shape
    conv_w, pool_w = conv_prev.shape[-1], pool_prev.shape[-1]
    n_mem, att_w = mk.shape[1], mk.shape[2]
    rows = min(MIXER_ROWS, seq)
    assert seq % rows == 0 and rows % SUBLANES == 0 and rows >= POOL_STATE
    in_cols = lw["w_in"].shape[1]
    per_b = lambda shape: pl.BlockSpec((1,) + shape, lambda b, s: (b, 0, 0))
    body = functools.partial(_mixer_body, rows=rows, start_pos=start_pos, conv_w=conv_w, pool_w=pool_w, att_w=att_w)
    return pl.pallas_call(
        body,
        grid=(bn, seq // rows),
        in_specs=[
            pl.BlockSpec((1, rows, d), lambda b, s: (b, s, 0)),
            per_b((CONV_K - 1, conv_w)), per_b((POOL_STATE, pool_w)), per_b((n_mem, att_w)), per_b((n_mem, att_w)),
            _resident((1, d)), _resident((d, in_cols)), _resident((CONV_K, conv_w)), _resident((1, conv_w)),
            _resident(lw["pool_w"].shape), _resident((1, pool_w)), _resident((conv_w, d)), _resident((pool_w, d)),
            _resident((att_w, d)), _resident((1, 3 * d)), _resident((d, d)),
        ],
        out_specs=[
            pl.BlockSpec((1, rows, d), lambda b, s: (b, s, 0)),
            per_b((CONV_K - 1, conv_w)), per_b((POOL_STATE, pool_w)),
        ],
        out_shape=[
            jax.ShapeDtypeStruct((bn, seq, d), F32),
            jax.ShapeDtypeStruct((bn, CONV_K - 1, conv_w), F32),
            jax.ShapeDtypeStruct((bn, POOL_STATE, pool_w), F32),
        ],
        scratch_shapes=[pltpu.VMEM((SUBLANES + rows, conv_w), F32), pltpu.VMEM((2 * SUBLANES + rows, pool_w), F32)],
        compiler_params=pltpu.CompilerParams(dimension_semantics=("arbitrary", "arbitrary"), vmem_limit_bytes=VMEM_LIMIT),
        name="mixer",
    )(x, conv_prev, pool_prev, mk, mv, lw["g_mix"], lw["w_in"], lw["conv_w"], lw["conv_b"], lw["pool_w"],
      lw["pool_scale"], lw["w_bc"], lw["w_bp"], lw["w_ba"], lw["gate_b"], lw["w_o"])


_STAIR = [(a, b) for a in range(PEER_TOPK) for b in range(PEER_TOPK) if (a + 1) * (b + 1) <= PEER_TOPK]
_STAIR_HEAD = 24
_STAIR_TAIL = [_STAIR[_STAIR_HEAD + 8 * i:_STAIR_HEAD + 8 * (i + 1)] for i in range(4)]


def _top16(v, payload=None):
    n, t = v.shape
    rows = lax.broadcasted_iota(jnp.int32, (n, t), 0)
    out_rows = lax.broadcasted_iota(jnp.int32, (PEER_TOPK, t), 0)
    vals = jnp.zeros((PEER_TOPK, t), F32)
    pay = jnp.zeros((PEER_TOPK, t), jnp.int32)
    for k in range(PEER_TOPK):
        m = jnp.max(v, axis=0, keepdims=True)
        am = jnp.min(jnp.where(v == m, rows, n), axis=0, keepdims=True)
        hit = rows == am
        p = am if payload is None else jnp.sum(jnp.where(hit, payload, 0), axis=0, keepdims=True)
        vals = jnp.where(out_rows == k, m, vals)
        pay = jnp.where(out_rows == k, p, pay)
        v = jnp.where(hit, -jnp.inf, v)
    return vals, pay


def _take_rows(src, idxs):
    t = src.shape[1]
    out_rows = lax.broadcasted_iota(jnp.int32, (SUBLANES, t), 0)
    acc = jnp.broadcast_to(src[idxs[0]:idxs[0] + 1, :], (SUBLANES, t))
    for i in range(1, SUBLANES):
        acc = jnp.where(out_rows == i, src[idxs[i]:idxs[i] + 1, :], acc)
    return acc


def _stair(first, second, combine):
    blocks = [combine(first[0:1, :], second), combine(first[1:2, :], second[0:SUBLANES, :])]
    for blk in _STAIR_TAIL:
        blk = blk + [(0, 0)] * (SUBLANES - len(blk))
        blocks.append(combine(_take_rows(first, [a for a, _ in blk]), _take_rows(second, [b for _, b in blk])))
    return jnp.concatenate(blocks, axis=0)


def _route_body(x2_ref, gffn_ref, wq_ref, keys_ref, xn_ref, ids_ref, gate_ref, *, n_keys):
    xn = _rms(x2_ref[...], gffn_ref[...])
    xn_ref[...] = xn
    q = _dot(xn.astype(BF16), wq_ref[...])
    half = keys_ref.shape[-1]
    t = q.shape[0]
    for h in range(PEER_HEADS):
        scores = [_dot_nt(keys_ref[p], q[:, (2 * h + p) * half:(2 * h + p + 1) * half].astype(BF16)) for p in range(2)]
        for l0 in range(0, t, LANES):
            (s1, i1), (s2, i2) = [_top16(sc[:, l0:l0 + LANES]) for sc in scores]
            cand = _stair(s1, s2, lambda a, b: a + b)
            pad = lax.broadcasted_iota(jnp.int32, cand.shape, 0) >= len(_STAIR)
            expert = _stair(i1, i2, lambda a, b: a * n_keys + b)
            sc, e = _top16(jnp.where(pad, -jnp.inf, cand), expert)
            ex = jnp.exp(sc - sc[0:1, :])
            ids_ref[h * PEER_TOPK:(h + 1) * PEER_TOPK, l0:l0 + LANES] = e
            gate_ref[h * PEER_TOPK:(h + 1) * PEER_TOPK, l0:l0 + LANES] = ex / jnp.sum(ex, axis=0, keepdims=True)


def _route(x2, g_ffn, wq, keys):
    t, d = x2.shape
    rows = ROUTE_ROWS
    assert t % rows == 0
    picks = PEER_HEADS * PEER_TOPK
    return pl.pallas_call(
        functools.partial(_route_body, n_keys=keys.shape[1]),
        grid=(t // rows,),
        in_specs=[pl.BlockSpec((rows, d), lambda i: (i, 0)), _resident((1, d)), _resident(wq.shape), _resident(keys.shape)],
        out_specs=[pl.BlockSpec((rows, d), lambda i: (i, 0)), pl.BlockSpec((picks, rows), lambda i: (0, i)),
                   pl.BlockSpec((picks, rows), lambda i: (0, i))],
        out_shape=[jax.ShapeDtypeStruct((t, d), F32), jax.ShapeDtypeStruct((picks, t), jnp.int32),
                   jax.ShapeDtypeStruct((picks, t), F32)],
        compiler_params=pltpu.CompilerParams(dimension_semantics=("arbitrary",), vmem_limit_bytes=VMEM_LIMIT),
        name="route",
    )(x2, g_ffn, wq, keys)


def _gelu(h):
    return 0.5 * h * (1.0 + lax.erf(h * (2.0 ** -0.5)))


def _experts_body(ids_hbm, gate_ref, xn_ref, x2_ref, gfin_ref, uv_hbm, y_ref, buf, sem, ids_smem, ids_sem,
                  *, rows, picks, steps, final_norm):
    i = pl.program_id(0)
    half = i % 2
    n = rows * picks
    d = xn_ref.shape[-1]
    lead = GATHER_SLOTS - 1

    def ids_copy(step, which):
        return pltpu.make_async_copy(ids_hbm.at[step], ids_smem.at[pl.ds(which * n, n)], ids_sem.at[which])

    def issue(off, slot):
        for j in range(picks):
            pltpu.async_copy(uv_hbm.at[ids_smem[off + j]], buf.at[slot, pl.ds(j, 1)], sem.at[slot], priority=j % 2)

    def wait_slot(slot):
        pltpu.make_async_copy(buf.at[slot], buf.at[slot], sem.at[slot]).wait()

    @pl.when(i == 0)
    def _():
        ids_copy(0, 0).start()
        ids_copy(0, 0).wait()
        if steps > 1:
            ids_copy(1, 1).start()
        for s in range(lead):
            issue(s * picks, s)

    lane = lax.broadcasted_iota(jnp.int32, (picks, rows), 1)

    def token(t, carry):
        slot = t % GATHER_SLOTS
        ahead = t + lead
        in_step = ahead < rows

        @pl.when(jnp.logical_and(ahead == rows, i + 1 < steps))
        def _():
            ids_copy(i + 1, 1 - half).wait()

        @pl.when(jnp.logical_or(in_step, i + 1 < steps))
        def _():
            issue(jnp.where(in_step, half * n + ahead * picks, (1 - half) * n + (ahead - rows) * picks),
                  ahead % GATHER_SLOTS)

        wait_slot(slot)
        x = xn_ref[pl.ds(t, 1), :]
        h = jnp.sum(buf[slot, :, 0:d] * x, axis=-1, keepdims=True)
        g = jnp.sum(jnp.where(lane == t, gate_ref[...], 0.0), axis=-1, keepdims=True)
        o = jnp.sum((g * _gelu(h)) * buf[slot, :, d:2 * d], axis=0, keepdims=True)
        y = x2_ref[pl.ds(t, 1), :] + o
        y_ref[pl.ds(t, 1), :] = _rms(y, gfin_ref[...]) if final_norm else y
        return carry

    lax.fori_loop(0, rows, token, 0)

    @pl.when(i + 2 < steps)
    def _():
        ids_copy(i + 2, half).start()


def _experts(ids_t, gate_t, xn, x2, g_final, uv, final_norm):
    picks, t = ids_t.shape
    d = xn.shape[1]
    rows = EXPERT_ROWS
    assert t % rows == 0 and rows >= GATHER_SLOTS and rows % GATHER_SLOTS == 0
    steps = t // rows
    ids = ids_t.T.reshape(steps, rows * picks)
    body = functools.partial(_experts_body, rows=rows, picks=picks, steps=steps, final_norm=final_norm)
    return pl.pallas_call(
        body,
        grid=(steps,),
        in_specs=[
            pl.BlockSpec(memory_space=pl.ANY),
            pl.BlockSpec((picks, rows), lambda i: (0, i)),
            pl.BlockSpec((rows, d), lambda i: (i, 0)),
            pl.BlockSpec((rows, d), lambda i: (i, 0)),
            _resident((1, d)),
            pl.BlockSpec(memory_space=pl.ANY),
        ],
        out_specs=pl.BlockSpec((rows, d), lambda i: (i, 0)),
        out_shape=jax.ShapeDtypeStruct((t, d), F32),
        scratch_shapes=[
            pltpu.VMEM((GATHER_SLOTS, picks, 2 * d), F32),
            pltpu.SemaphoreType.DMA((GATHER_SLOTS,)),
            pltpu.SMEM((2 * rows * picks,), jnp.int32),
            pltpu.SemaphoreType.DMA((2,)),
        ],
        compiler_params=pltpu.CompilerParams(dimension_semantics=("arbitrary",), vmem_limit_bytes=VMEM_LIMIT),
        name="experts",
    )(ids, gate_t, xn, x2, g_final, uv)


def _layer(x, conv_prev, pool_prev, start_pos, mk, mv, lw, g_final, final_norm):
    bn, seq, d = x.shape
    x2, new_conv, new_pool = _mixer(x, conv_prev, pool_prev, start_pos, mk, mv, lw)
    x2 = x2.reshape(bn * seq, d)
    xn, ids_t, gate_t = _route(x2, lw["g_ffn"], lw["peer_wq"], lw["peer_keys"])
    y = _experts(ids_t, gate_t, xn, x2, g_final, lw["peer_uv"], final_norm)
    return y.reshape(bn, seq, d), new_conv, new_pool


def kernel(x_prompt, x_sample, mem_prompt, cache_conv, cache_pool, cache_mem_k, cache_mem_v, g_mix, w_in, conv_w, conv_b, pool_w, pool_scale, g_mem, w_mk, w_mv, w_bc, w_bp, w_ba, gate_b, w_o, g_ffn, peer_wq, peer_keys, peer_u, peer_v, g_final):
    depth, d = g_mix.shape
    xp, xs = x_prompt, x_sample
    bp, bs = xp.shape[0], xs.shape[0]
    n_exp = peer_u.shape[1]
    gfin = g_final.reshape(1, d)
    conv_p, pool_p, mk_p, mv_p, conv_s, pool_s = [], [], [], [], [], []
    for l in range(depth):
        lw = dict(
            g_mix=g_mix[l].reshape(1, d), w_in=w_in[l].astype(BF16), conv_w=conv_w[l], conv_b=conv_b[l].reshape(1, -1),
            pool_w=pool_w[l].astype(BF16), pool_scale=pool_scale[l].reshape(1, -1), w_bc=w_bc[l].astype(BF16),
            w_bp=w_bp[l].astype(BF16), w_ba=w_ba[l].astype(BF16), gate_b=gate_b[l].reshape(1, -1), w_o=w_o[l].astype(BF16),
            g_ffn=g_ffn[l].reshape(1, d), peer_wq=peer_wq[l].astype(BF16), peer_keys=peer_keys[l].astype(BF16),
            peer_uv=jnp.concatenate([peer_u[l], peer_v[l]], axis=1).reshape(n_exp, 1, 2 * d),
        )
        last = l == depth - 1
        mk, mv = _mem_kv(mem_prompt, g_mem[l], w_mk[l], w_mv[l])
        zc = jnp.zeros((bp, CONV_K - 1, conv_w.shape[-1]), xp.dtype)
        zp = jnp.zeros((bp, POOL_STATE, pool_scale.shape[-1]), xp.dtype)
        xp, cp, pp = _layer(xp, zc, zp, 0, mk, mv, lw, gfin, last)
        n_mem = cache_mem_k.shape[2]
        xs, cs, ps = _layer(xs, cache_conv[l], cache_pool[l], PAST_LEN, cache_mem_k[l].reshape(bs, n_mem, -1),
                            cache_mem_v[l].reshape(bs, n_mem, -1), lw, gfin, last)
        heads_shape = (bp, n_mem) + cache_mem_k.shape[3:]
        conv_p.append(cp); pool_p.append(pp); mk_p.append(mk.reshape(heads_shape)); mv_p.append(mv.reshape(heads_shape))
        conv_s.append(cs); pool_s.append(ps)
    return (xp, xs, jnp.stack(conv_p), jnp.stack(pool_p), jnp.stack(mk_p), jnp.stack(mv_p),
            jnp.stack(conv_s), jnp.stack(pool_s))
```

```python
import functools

import jax
import jax.numpy as jnp
from jax import lax
from jax.experimental import pallas as pl
from jax.experimental.pallas import tpu as pltpu

F32 = jnp.float32
BF16 = jnp.bfloat16
EPS = 1e-6

CONV_K = 3
POOL_WINDOWS = (2, 4, 8, 16)
POOL_STATE = max(POOL_WINDOWS) - 1
PAST_LEN = 1024
MEM_HEADS = 4
PEER_HEADS = 8
PEER_TOPK = 16

SUBLANES = 8
LANES = 128
MIXER_ROWS = 256
ROUTE_ROWS = 256
EXPERT_ROWS = 128
GATHER_SLOTS = 4
VMEM_LIMIT = 56 * 1024 * 1024


def _rms(x, g):
    return x * lax.rsqrt(jnp.mean(x * x, axis=-1, keepdims=True) + EPS) * g


def _dot(a, b):
    return jnp.dot(a, b, preferred_element_type=F32)


def _dot_nt(a, b):
    return lax.dot_general(a, b, (((1,), (1,)), ((), ())), preferred_element_type=F32)


def _resident(shape):
    zeros = (0,) * len(shape)
    return pl.BlockSpec(shape, lambda *_: zeros, pipeline_mode=pl.Buffered(1))


def _memkv_body(mem_ref, g_ref, wk_ref, wv_ref, k_ref, v_ref):
    m = _rms(mem_ref[...], g_ref[...]).astype(BF16)
    k_ref[...] = _dot(m, wk_ref[...])
    v_ref[...] = _dot(m, wv_ref[...])


def _mem_kv(mem, g_mem, w_mk, w_mv):
    bn, n_mem, d = mem.shape
    att_w = w_mk.shape[1]
    k, v = pl.pallas_call(
        _memkv_body,
        out_shape=[jax.ShapeDtypeStruct((bn * n_mem, att_w), F32)] * 2,
        name="mem_kv",
    )(mem.reshape(bn * n_mem, d), g_mem.reshape(1, d), w_mk.astype(BF16), w_mv.astype(BF16))
    return k.reshape(bn, n_mem, att_w), v.reshape(bn, n_mem, att_w)


def _mixer_body(x_ref, cprev_ref, pprev_ref, mk_ref, mv_ref, gmix_ref, win_ref, convw_ref, convb_ref, poolw_ref,
                pscale_ref, wbc_ref, wbp_ref, wba_ref, gateb_ref, wo_ref,
                x2_ref, nconv_ref, npool_ref, zbuf, ubuf, *, rows, start_pos, conv_w, pool_w, att_w):
    s = pl.program_id(1)
    d_model = x_ref.shape[-1]
    z0 = SUBLANES
    u0 = 2 * SUBLANES

    @pl.when(s == 0)
    def _():
        zbuf[z0 - (CONV_K - 1):z0, :] = cprev_ref[0]
        ubuf[u0 - POOL_STATE:u0, :] = pprev_ref[0]

    x = x_ref[0]
    hb = _rms(x, gmix_ref[...]).astype(BF16)

    c0 = 0
    pa = _dot(hb, win_ref[:, c0:c0 + 3 * conv_w])
    xc, bg, cg = pa[:, :conv_w], pa[:, conv_w:2 * conv_w], pa[:, 2 * conv_w:]
    zbuf[z0:z0 + rows, :] = cg * xc
    cw = convw_ref[...]
    conv = zbuf[z0 - 2:z0 - 2 + rows, :] * cw[0:1]
    for k in range(1, CONV_K):
        conv = conv + zbuf[z0 - 2 + k:z0 - 2 + k + rows, :] * cw[k:k + 1]
    conv = conv + convb_ref[...]
    ya = _dot((bg * conv).astype(BF16), wbc_ref[...])
    last_z = zbuf[z0 + rows - (CONV_K - 1):z0 + rows, :]
    nconv_ref[0] = last_z
    zbuf[z0 - (CONV_K - 1):z0, :] = last_z
    c0 += 3 * conv_w

    up = _dot(hb, win_ref[:, c0:c0 + pool_w])
    ubuf[u0:u0 + rows, :] = up
    pos = start_pos + s * rows + lax.broadcasted_iota(jnp.int32, (rows, 1), 0)
    gw = pool_w // len(POOL_WINDOWS)
    ys = []
    for g, w in enumerate(POOL_WINDOWS):
        cur = up[:, g * gw:(g + 1) * gw]
        acc = cur
        for k in range(1, w):
            acc = acc + ubuf[u0 - k:u0 - k + rows, g * gw:(g + 1) * gw]
        cnt = jnp.minimum(pos + 1, w).astype(F32)
        ys.append(_dot((acc / cnt - cur).astype(BF16), poolw_ref[g]))
    yb = _dot((jnp.concatenate(ys, axis=-1) * pscale_ref[...]).astype(BF16), wbp_ref[...])
    last_u = ubuf[u0 + rows - POOL_STATE:u0 + rows, :]
    npool_ref[0] = last_u
    ubuf[u0 - POOL_STATE:u0, :] = last_u
    c0 += pool_w

    q = _dot(hb, win_ref[:, c0:c0 + att_w])
    kb = mk_ref[0].astype(BF16)
    vb = mv_ref[0].astype(BF16)
    hd = att_w // MEM_HEADS
    heads = []
    for h in range(MEM_HEADS):
        sc = _dot_nt(q[:, h * hd:(h + 1) * hd].astype(BF16), kb[:, h * hd:(h + 1) * hd]) * (hd ** -0.5)
        e = jnp.exp(sc - jnp.max(sc, axis=-1, keepdims=True))
        p = e / jnp.sum(e, axis=-1, keepdims=True)
        heads.append(_dot(p.astype(BF16), vb[:, h * hd:(h + 1) * hd]))
    yc = _dot(jnp.concatenate(heads, axis=-1).astype(BF16), wba_ref[...])
    c0 += att_w

    merged = None
    for i, y in enumerate((ya, yb, yc)):
        gl = _dot(hb, win_ref[:, c0 + i * d_model:c0 + (i + 1) * d_model]) + gateb_ref[:, i * d_model:(i + 1) * d_model]
        term = (1.0 / (1.0 + jnp.exp(-gl))) * y
        merged = term if merged is None else merged + term
    x2_ref[0] = x + _dot(merged.astype(BF16), wo_ref[...])


def _mixer(x, conv_prev, pool_prev, start_pos, mk, mv, lw):
    bn, seq, d = x.shape
    conv_w, pool_w = conv_prev.shape[-1], pool_prev.shape[-1]
    n_mem, att_w = mk.shape[1], mk.shape[2]
    rows = min(MIXER_ROWS, seq)
    assert seq % rows == 0 and rows % SUBLANES == 0 and rows >= POOL_STATE
    in_cols = lw["w_in"].shape[1]
    per_b = lambda shape: pl.BlockSpec((1,) + shape, lambda b, s: (b, 0, 0))
    body = functools.partial(_mixer_body, rows=rows, start_pos=start_pos, conv_w=conv_w, pool_w=pool_w, att_w=att_w)
    return pl.pallas_call(
        body,
        grid=(bn, seq // rows),
        in_specs=[
            pl.BlockSpec((1, rows, d), lambda b, s: (b, s, 0)),
            per_b((CONV_K - 1, conv_w)), per_b((POOL_STATE, pool_w)), per_b((n_mem, att_w)), per_b((n_mem, att_w)),
            _resident((1, d)), _resident((d, in_cols)), _resident((CONV_K, conv_w)), _resident((1, conv_w)),
            _resident(lw["pool_w"].shape), _resident((1, pool_w)), _resident((conv_w, d)), _resident((pool_w, d)),
            _resident((att_w, d)), _resident((1, 3 * d)), _resident((d, d)),
        ],
        out_specs=[
            pl.BlockSpec((1, rows, d), lambda b, s: (b, s, 0)),
            per_b((CONV_K - 1, conv_w)), per_b((POOL_STATE, pool_w)),
        ],
        out_shape=[
            jax.ShapeDtypeStruct((bn, seq, d), F32),
            jax.ShapeDtypeStruct((bn, CONV_K - 1, conv_w), F32),
            jax.ShapeDtypeStruct((bn, POOL_STATE, pool_w), F32),
        ],
        scratch_shapes=[pltpu.VMEM((SUBLANES + rows, conv_w), F32), pltpu.VMEM((2 * SUBLANES + rows, pool_w), F32)],
        compiler_params=pltpu.CompilerParams(dimension_semantics=("arbitrary", "arbitrary"), vmem_limit_bytes=VMEM_LIMIT),
        name="mixer",
    )(x, conv_prev, pool_prev, mk, mv, lw["g_mix"], lw["w_in"], lw["conv_w"], lw["conv_b"], lw["pool_w"],
      lw["pool_scale"], lw["w_bc"], lw["w_bp"], lw["w_ba"], lw["gate_b"], lw["w_o"])


_STAIR = [(a, b) for a in range(PEER_TOPK) for b in range(PEER_TOPK) if (a + 1) * (b + 1) <= PEER_TOPK]
_STAIR_HEAD = 24
_STAIR_TAIL = [_STAIR[_STAIR_HEAD + 8 * i:_STAIR_HEAD + 8 * (i + 1)] for i in range(4)]


def _top16(v, payload=None):
    n, t = v.shape
    rows = lax.broadcasted_iota(jnp.int32, (n, t), 0)
    out_rows = lax.broadcasted_iota(jnp.int32, (PEER_TOPK, t), 0)
    vals = jnp.zeros((PEER_TOPK, t), F32)
    pay = jnp.zeros((PEER_TOPK, t), jnp.int32)
    for k in range(PEER_TOPK):
        m = jnp.max(v, axis=0, keepdims=True)
        am = jnp.min(jnp.where(v == m, rows, n), axis=0, keepdims=True)
        hit = rows == am
        p = am if payload is None else jnp.sum(jnp.where(hit, payload, 0), axis=0, keepdims=True)
        vals = jnp.where(out_rows == k, m, vals)
        pay = jnp.where(out_rows == k, p, pay)
        v = jnp.where(hit, -jnp.inf, v)
    return vals, pay


def _take_rows(src, idxs):
    t = src.shape[1]
    out_rows = lax.broadcasted_iota(jnp.int32, (SUBLANES, t), 0)
    acc = jnp.broadcast_to(src[idxs[0]:idxs[0] + 1, :], (SUBLANES, t))
    for i in range(1, SUBLANES):
        acc = jnp.where(out_rows == i, src[idxs[i]:idxs[i] + 1, :], acc)
    return acc


def _stair(first, second, combine):
    blocks = [combine(first[0:1, :], second), combine(first[1:2, :], second[0:SUBLANES, :])]
    for blk in _STAIR_TAIL:
        blk = blk + [(0, 0)] * (SUBLANES - len(blk))
        blocks.append(combine(_take_rows(first, [a for a, _ in blk]), _take_rows(second, [b for _, b in blk])))
    return jnp.concatenate(blocks, axis=0)


def _route_body(x2_ref, gffn_ref, wq_ref, keys_ref, xn_ref, ids_ref, gate_ref, *, n_keys):
    xn = _rms(x2_ref[...], gffn_ref[...])
    xn_ref[...] = xn
    q = _dot(xn.astype(BF16), wq_ref[...])
    half = keys_ref.shape[-1]
    t = q.shape[0]
    for h in range(PEER_HEADS):
        scores = [_dot_nt(keys_ref[p], q[:, (2 * h + p) * half:(2 * h + p + 1) * half].astype(BF16)) for p in range(2)]
        for l0 in range(0, t, LANES):
            (s1, i1), (s2, i2) = [_top16(sc[:, l0:l0 + LANES]) for sc in scores]
            cand = _stair(s1, s2, lambda a, b: a + b)
            pad = lax.broadcasted_iota(jnp.int32, cand.shape, 0) >= len(_STAIR)
            expert = _stair(i1, i2, lambda a, b: a * n_keys + b)
            sc, e = _top16(jnp.where(pad, -jnp.inf, cand), expert)
            ex = jnp.exp(sc - sc[0:1, :])
            ids_ref[h * PEER_TOPK:(h + 1) * PEER_TOPK, l0:l0 + LANES] = e
            gate_ref[h * PEER_TOPK:(h + 1) * PEER_TOPK, l0:l0 + LANES] = ex / jnp.sum(ex, axis=0, keepdims=True)


def _route(x2, g_ffn, wq, keys):
    t, d = x2.shape
    rows = ROUTE_ROWS
    assert t % rows == 0
    picks = PEER_HEADS * PEER_TOPK
    return pl.pallas_call(
        functools.partial(_route_body, n_keys=keys.shape[1]),
        grid=(t // rows,),
        in_specs=[pl.BlockSpec((rows, d), lambda i: (i, 0)), _resident((1, d)), _resident(wq.shape), _resident(keys.shape)],
        out_specs=[pl.BlockSpec((rows, d), lambda i: (i, 0)), pl.BlockSpec((picks, rows), lambda i: (0, i)),
                   pl.BlockSpec((picks, rows), lambda i: (0, i))],
        out_shape=[jax.ShapeDtypeStruct((t, d), F32), jax.ShapeDtypeStruct((picks, t), jnp.int32),
                   jax.ShapeDtypeStruct((picks, t), F32)],
        compiler_params=pltpu.CompilerParams(dimension_semantics=("arbitrary",), vmem_limit_bytes=VMEM_LIMIT),
        name="route",
    )(x2, g_ffn, wq, keys)


def _gelu(h):
    return 0.5 * h * (1.0 + lax.erf(h * (2.0 ** -0.5)))


def _experts_body(ids_hbm, gate_ref, xn_ref, x2_ref, gfin_ref, uv_hbm, y_ref, *scratch,
                  rows, picks, steps, final_norm):
    bufs, (sem, ids_smem, ids_sem) = scratch[:GATHER_SLOTS], scratch[GATHER_SLOTS:]
    i = pl.program_id(0)
    half = i % 2
    n = rows * picks
    d = xn_ref.shape[-1]
    lead = GATHER_SLOTS - 1
    groups = picks // SUBLANES

    def ids_copy(step, which):
        return pltpu.make_async_copy(ids_hbm.at[step], ids_smem.at[pl.ds(which * n, n)], ids_sem.at[which])

    def issue(off, slot, j0, j1):
        for j in range(j0, j1):
            pltpu.async_copy(uv_hbm.at[ids_smem[off + j]], bufs[slot].at[pl.ds(j, 1)], sem.at[slot], priority=j % 2)

    def wait_slot(slot):
        pltpu.make_async_copy(bufs[slot], bufs[slot], sem.at[slot]).wait()

    @pl.when(i == 0)
    def _():
        ids_copy(0, 0).start()
        ids_copy(0, 0).wait()
        if steps > 1:
            ids_copy(1, 1).start()
        for s in range(lead):
            issue(s * picks, s, 0, picks)

    lane = lax.broadcasted_iota(jnp.int32, (picks, rows), 1)

    def token(t, slot, ahead_off):
        ahead_slot = (slot + lead) % GATHER_SLOTS
        buf = bufs[slot]
        wait_slot(slot)
        if ahead_off is not None:
            issue(ahead_off, ahead_slot, 0, picks)
        x = xn_ref[pl.ds(t, 1), :]
        g = jnp.sum(jnp.where(lane == t, gate_ref[...], 0.0), axis=-1, keepdims=True)
        hs = []
        for k in range(groups):
            r0 = k * SUBLANES
            hs.append(jnp.sum(buf[r0:r0 + SUBLANES, 0:d] * x, axis=-1, keepdims=True))
        w = g * _gelu(jnp.concatenate(hs, axis=0))
        acc = None
        for k in range(groups):
            r0 = k * SUBLANES
            part = w[r0:r0 + SUBLANES, :] * buf[r0:r0 + SUBLANES, d:2 * d]
            acc = part if acc is None else acc + part
        y = x2_ref[pl.ds(t, 1), :] + jnp.sum(acc, axis=0, keepdims=True)
        y_ref[pl.ds(t, 1), :] = _rms(y, gfin_ref[...]) if final_norm else y

    def four_tokens(k, carry):
        for s in range(GATHER_SLOTS):
            t = k * GATHER_SLOTS + s
            token(t, s, half * n + (t + lead) * picks)
        return carry

    lax.fori_loop(0, rows // GATHER_SLOTS - 1, four_tokens, 0)

    t0 = rows - GATHER_SLOTS
    token(t0, 0, half * n + (t0 + lead) * picks)

    @pl.when(i + 1 < steps)
    def _():
        ids_copy(i + 1, 1 - half).wait()

    for s in range(1, GATHER_SLOTS):
        @pl.when(i + 1 < steps)
        def _():
            issue((1 - half) * n + (s - 1) * picks, (s + lead) % GATHER_SLOTS, 0, picks)

        token(t0 + s, s, None)

    @pl.when(i + 2 < steps)
    def _():
        ids_copy(i + 2, half).start()


def _experts(ids_t, gate_t, xn, x2, g_final, uv, final_norm):
    picks, t = ids_t.shape
    d = xn.shape[1]
    rows = EXPERT_ROWS
    assert t % rows == 0 and rows >= 2 * GATHER_SLOTS and rows % GATHER_SLOTS == 0 and picks % (2 * SUBLANES) == 0
    steps = t // rows
    ids = ids_t.T.reshape(steps, rows * picks)
    body = functools.partial(_experts_body, rows=rows, picks=picks, steps=steps, final_norm=final_norm)
    return pl.pallas_call(
        body,
        grid=(steps,),
        in_specs=[
            pl.BlockSpec(memory_space=pl.ANY),
            pl.BlockSpec((picks, rows), lambda i: (0, i)),
            pl.BlockSpec((rows, d), lambda i: (i, 0)),
            pl.BlockSpec((rows, d), lambda i: (i, 0)),
            _resident((1, d)),
            pl.BlockSpec(memory_space=pl.ANY),
        ],
        out_specs=pl.BlockSpec((rows, d), lambda i: (i, 0)),
        out_shape=jax.ShapeDtypeStruct((t, d), F32),
        scratch_shapes=[pltpu.VMEM((picks, 2 * d), F32)] * GATHER_SLOTS + [
            pltpu.SemaphoreType.DMA((GATHER_SLOTS,)),
            pltpu.SMEM((2 * rows * picks,), jnp.int32),
            pltpu.SemaphoreType.DMA((2,)),
        ],
        compiler_params=pltpu.CompilerParams(dimension_semantics=("arbitrary",), vmem_limit_bytes=VMEM_LIMIT),
        name="experts",
    )(ids, gate_t, xn, x2, g_final, uv)


def _layer(x, conv_prev, pool_prev, start_pos, mk, mv, lw, g_final, final_norm):
    bn, seq, d = x.shape
    x2, new_conv, new_pool = _mixer(x, conv_prev, pool_prev, start_pos, mk, mv, lw)
    x2 = x2.reshape(bn * seq, d)
    xn, ids_t, gate_t = _route(x2, lw["g_ffn"], lw["peer_wq"], lw["peer_keys"])
    y = _experts(ids_t, gate_t, xn, x2, g_final, lw["peer_uv"], final_norm)
    return y.reshape(bn, seq, d), new_conv, new_pool


def kernel(x_prompt, x_sample, mem_prompt, cache_conv, cache_pool, cache_mem_k, cache_mem_v, g_mix, w_in, conv_w, conv_b, pool_w, pool_scale, g_mem, w_mk, w_mv, w_bc, w_bp, w_ba, gate_b, w_o, g_ffn, peer_wq, peer_keys, peer_u, peer_v, g_final):
    depth, d = g_mix.shape
    xp, xs = x_prompt, x_sample
    bp, bs = xp.shape[0], xs.shape[0]
    n_exp = peer_u.shape[1]
    gfin = g_final.reshape(1, d)
    conv_p, pool_p, mk_p, mv_p, conv_s, pool_s = [], [], [], [], [], []
    for l in range(depth):
        lw = dict(
            g_mix=g_mix[l].reshape(1, d), w_in=w_in[l].astype(BF16), conv_w=conv_w[l], conv_b=conv_b[l].reshape(1, -1),
            pool_w=pool_w[l].astype(BF16), pool_scale=pool_scale[l].reshape(1, -1), w_bc=w_bc[l].astype(BF16),
            w_bp=w_bp[l].astype(BF16), w_ba=w_ba[l].astype(BF16), gate_b=gate_b[l].reshape(1, -1), w_o=w_o[l].astype(BF16),
            g_ffn=g_ffn[l].reshape(1, d), peer_wq=peer_wq[l].astype(BF16), peer_keys=peer_keys[l].astype(BF16),
            peer_uv=jnp.concatenate([peer_u[l], peer_v[l]], axis=1).reshape(n_exp, 1, 2 * d),
        )
        last = l == depth - 1
        mk, mv = _mem_kv(mem_prompt, g_mem[l], w_mk[l], w_mv[l])
        zc = jnp.zeros((bp, CONV_K - 1, conv_w.shape[-1]), xp.dtype)
        zp = jnp.zeros((bp, POOL_STATE, pool_scale.shape[-1]), xp.dtype)
        xp, cp, pp = _layer(xp, zc, zp, 0, mk, mv, lw, gfin, last)
        n_mem = cache_mem_k.shape[2]
        xs, cs, ps = _layer(xs, cache_conv[l], cache_pool[l], PAST_LEN, cache_mem_k[l].reshape(bs, n_mem, -1),
                            cache_mem_v[l].reshape(bs, n_mem, -1), lw, gfin, last)
        heads_shape = (bp, n_mem) + cache_mem_k.shape[3:]
        conv_p.append(cp); pool_p.append(pp); mk_p.append(mk.reshape(heads_shape)); mv_p.append(mv.reshape(heads_shape))
        conv_s.append(cs); pool_s.append(ps)
    return (xp, xs, jnp.stack(conv_p), jnp.stack(pool_p), jnp.stack(mk_p), jnp.stack(mv_p),
            jnp.stack(conv_s), jnp.stack(pool_s))
```

```python
import functools
import math

import jax
import jax.numpy as jnp
from jax import lax
from jax.experimental import pallas as pl
from jax.experimental.pallas import tpu as pltpu
from jax.experimental.pallas import tpu_sc as plsc

F32 = jnp.float32
BF16 = jnp.bfloat16
EPS = 1e-6

CONV_K = 3
POOL_WINDOWS = (2, 4, 8, 16)
POOL_STATE = max(POOL_WINDOWS) - 1
PAST_LEN = 1024
MEM_HEADS = 4
PEER_HEADS = 8
PEER_TOPK = 16

SUBLANES = 8
LANES = 128
MIXER_ROWS = 256
ROUTE_ROWS = 256
EXPERT_ROWS = 128
GATHER_SLOTS = 4
VMEM_LIMIT = 56 * 1024 * 1024

SC_CORES = 2
SC_SUBCORES = 16
SC_WORKERS = SC_CORES * SC_SUBCORES
SC_LANES = 16
SC_CHUNK = 32
SC_UNROLL = 8
SC_SHARE = 0.43
SC_MIN_TOKENS = 8192
TC_FIRST_SHARE = 0.52


def _rms(x, g):
    return x * lax.rsqrt(jnp.mean(x * x, axis=-1, keepdims=True) + EPS) * g


def _dot(a, b):
    return jnp.dot(a, b, preferred_element_type=F32)


def _dot_nt(a, b):
    return lax.dot_general(a, b, (((1,), (1,)), ((), ())), preferred_element_type=F32)


def _resident(shape):
    zeros = (0,) * len(shape)
    return pl.BlockSpec(shape, lambda *_: zeros, pipeline_mode=pl.Buffered(1))


def _memkv_body(mem_ref, g_ref, wk_ref, wv_ref, k_ref, v_ref):
    m = _rms(mem_ref[...], g_ref[...]).astype(BF16)
    k_ref[...] = _dot(m, wk_ref[...])
    v_ref[...] = _dot(m, wv_ref[...])


def _mem_kv(mem, g_mem, w_mk, w_mv):
    bn, n_mem, d = mem.shape
    att_w = w_mk.shape[1]
    k, v = pl.pallas_call(
        _memkv_body,
        out_shape=[jax.ShapeDtypeStruct((bn * n_mem, att_w), F32)] * 2,
        name="mem_kv",
    )(mem.reshape(bn * n_mem, d), g_mem.reshape(1, d), w_mk.astype(BF16), w_mv.astype(BF16))
    return k.reshape(bn, n_mem, att_w), v.reshape(bn, n_mem, att_w)


def _mixer_body(x_ref, cprev_ref, pprev_ref, mk_ref, mv_ref, gmix_ref, win_ref, convw_ref, convb_ref, poolw_ref,
                pscale_ref, wbc_ref, wbp_ref, wba_ref, gateb_ref, wo_ref,
                x2_ref, nconv_ref, npool_ref, zbuf, ubuf, *, rows, start_pos, conv_w, pool_w, att_w):
    s = pl.program_id(1)
    d_model = x_ref.shape[-1]
    z0 = SUBLANES
    u0 = 2 * SUBLANES

    @pl.when(s == 0)
    def _():
        zbuf[z0 - (CONV_K - 1):z0, :] = cprev_ref[0]
        ubuf[u0 - POOL_STATE:u0, :] = pprev_ref[0]

    x = x_ref[0]
    hb = _rms(x, gmix_ref[...]).astype(BF16)

    c0 = 0
    pa = _dot(hb, win_ref[:, c0:c0 + 3 * conv_w])
    xc, bg, cg = pa[:, :conv_w], pa[:, conv_w:2 * conv_w], pa[:, 2 * conv_w:]
    zbuf[z0:z0 + rows, :] = cg * xc
    cw = convw_ref[...]
    conv = zbuf[z0 - 2:z0 - 2 + rows, :] * cw[0:1]
    for k in range(1, CONV_K):
        conv = conv + zbuf[z0 - 2 + k:z0 - 2 + k + rows, :] * cw[k:k + 1]
    conv = conv + convb_ref[...]
    ya = _dot((bg * conv).astype(BF16), wbc_ref[...])
    last_z = zbuf[z0 + rows - (CONV_K - 1):z0 + rows, :]
    nconv_ref[0] = last_z
    zbuf[z0 - (CONV_K - 1):z0, :] = last_z
    c0 += 3 * conv_w

    up = _dot(hb, win_ref[:, c0:c0 + pool_w])
    ubuf[u0:u0 + rows, :] = up
    pos = start_pos + s * rows + lax.broadcasted_iota(jnp.int32, (rows, 1), 0)
    gw = pool_w // len(POOL_WINDOWS)
    ys = []
    for g, w in enumerate(POOL_WINDOWS):
        cur = up[:, g * gw:(g + 1) * gw]
        acc = cur
        for k in range(1, w):
            acc = acc + ubuf[u0 - k:u0 - k + rows, g * gw:(g + 1) * gw]
        cnt = jnp.minimum(pos + 1, w).astype(F32)
        ys.append(_dot((acc / cnt - cur).astype(BF16), poolw_ref[g]))
    yb = _dot((jnp.concatenate(ys, axis=-1) * pscale_ref[...]).astype(BF16), wbp_ref[...])
    last_u = ubuf[u0 + rows - POOL_STATE:u0 + rows, :]
    npool_ref[0] = last_u
    ubuf[u0 - POOL_STATE:u0, :] = last_u
    c0 += pool_w

    q = _dot(hb, win_ref[:, c0:c0 + att_w])
    kb = mk_ref[0].astype(BF16)
    vb = mv_ref[0].astype(BF16)
    hd = att_w // MEM_HEADS
    heads = []
    for h in range(MEM_HEADS):
        sc = _dot_nt(q[:, h * hd:(h + 1) * hd].astype(BF16), kb[:, h * hd:(h + 1) * hd]) * (hd ** -0.5)
        e = jnp.exp(sc - jnp.max(sc, axis=-1, keepdims=True))
        p = e / jnp.sum(e, axis=-1, keepdims=True)
        heads.append(_dot(p.astype(BF16), vb[:, h * hd:(h + 1) * hd]))
    yc = _dot(jnp.concatenate(heads, axis=-1).astype(BF16), wba_ref[...])
    c0 += att_w

    merged = None
    for i, y in enumerate((ya, yb, yc)):
        gl = _dot(hb, win_ref[:, c0 + i * d_model:c0 + (i + 1) * d_model]) + gateb_ref[:, i * d_model:(i + 1) * d_model]
        term = (1.0 / (1.0 + jnp.exp(-gl))) * y
        merged = term if merged is None else merged + term
    x2_ref[0] = x + _dot(merged.astype(BF16), wo_ref[...])


def _mixer(x, conv_prev, pool_prev, start_pos, mk, mv, lw):
    bn, seq, d = x.shape
    conv_w, pool_w = conv_prev.shape[-1], pool_prev.shape[-1]
    n_mem, att_w = mk.shape[1], mk.shape[2]
    rows = min(MIXER_ROWS, seq)
    assert seq % rows == 0 and rows % SUBLANES == 0 and rows >= POOL_STATE
    in_cols = lw["w_in"].shape[1]
    per_b = lambda shape: pl.BlockSpec((1,) + shape, lambda b, s: (b, 0, 0))
    body = functools.partial(_mixer_body, rows=rows, start_pos=start_pos, conv_w=conv_w, pool_w=pool_w, att_w=att_w)
    return pl.pallas_call(
        body,
        grid=(bn, seq // rows),
        in_specs=[
            pl.BlockSpec((1, rows, d), lambda b, s: (b, s, 0)),
            per_b((CONV_K - 1, conv_w)), per_b((POOL_STATE, pool_w)), per_b((n_mem, att_w)), per_b((n_mem, att_w)),
            _resident((1, d)), _resident((d, in_cols)), _resident((CONV_K, conv_w)), _resident((1, conv_w)),
            _resident(lw["pool_w"].shape), _resident((1, pool_w)), _resident((conv_w, d)), _resident((pool_w, d)),
            _resident((att_w, d)), _resident((1, 3 * d)), _resident((d, d)),
        ],
        out_specs=[
            pl.BlockSpec((1, rows, d), lambda b, s: (b, s, 0)),
            per_b((CONV_K - 1, conv_w)), per_b((POOL_STATE, pool_w)),
        ],
        out_shape=[
            jax.ShapeDtypeStruct((bn, seq, d), F32),
            jax.ShapeDtypeStruct((bn, CONV_K - 1, conv_w), F32),
            jax.ShapeDtypeStruct((bn, POOL_STATE, pool_w), F32),
        ],
        scratch_shapes=[pltpu.VMEM((SUBLANES + rows, conv_w), F32), pltpu.VMEM((2 * SUBLANES + rows, pool_w), F32)],
        compiler_params=pltpu.CompilerParams(dimension_semantics=("arbitrary", "arbitrary"), vmem_limit_bytes=VMEM_LIMIT),
        name="mixer",
    )(x, conv_prev, pool_prev, mk, mv, lw["g_mix"], lw["w_in"], lw["conv_w"], lw["conv_b"], lw["pool_w"],
      lw["pool_scale"], lw["w_bc"], lw["w_bp"], lw["w_ba"], lw["gate_b"], lw["w_o"])


_STAIR = [(a, b) for a in range(PEER_TOPK) for b in range(PEER_TOPK) if (a + 1) * (b + 1) <= PEER_TOPK]
_STAIR_HEAD = 24
_STAIR_TAIL = [_STAIR[_STAIR_HEAD + 8 * i:_STAIR_HEAD + 8 * (i + 1)] for i in range(4)]


def _top16(v, payload=None):
    n, t = v.shape
    rows = lax.broadcasted_iota(jnp.int32, (n, t), 0)
    out_rows = lax.broadcasted_iota(jnp.int32, (PEER_TOPK, t), 0)
    vals = jnp.zeros((PEER_TOPK, t), F32)
    pay = jnp.zeros((PEER_TOPK, t), jnp.int32)
    for k in range(PEER_TOPK):
        m = jnp.max(v, axis=0, keepdims=True)
        am = jnp.min(jnp.where(v == m, rows, n), axis=0, keepdims=True)
        hit = rows == am
        p = am if payload is None else jnp.sum(jnp.where(hit, payload, 0), axis=0, keepdims=True)
        vals = jnp.where(out_rows == k, m, vals)
        pay = jnp.where(out_rows == k, p, pay)
        v = jnp.where(hit, -jnp.inf, v)
    return vals, pay


def _take_rows(src, idxs):
    t = src.shape[1]
    out_rows = lax.broadcasted_iota(jnp.int32, (SUBLANES, t), 0)
    acc = jnp.broadcast_to(src[idxs[0]:idxs[0] + 1, :], (SUBLANES, t))
    for i in range(1, SUBLANES):
        acc = jnp.where(out_rows == i, src[idxs[i]:idxs[i] + 1, :], acc)
    return acc


def _stair(first, second, combine):
    blocks = [combine(first[0:1, :], second), combine(first[1:2, :], second[0:SUBLANES, :])]
    for blk in _STAIR_TAIL:
        blk = blk + [(0, 0)] * (SUBLANES - len(blk))
        blocks.append(combine(_take_rows(first, [a for a, _ in blk]), _take_rows(second, [b for _, b in blk])))
    return jnp.concatenate(blocks, axis=0)


def _route_body(x2_ref, gffn_ref, wq_ref, keys_ref, xn_ref, ids_ref, gate_ref, *, n_keys):
    xn = _rms(x2_ref[...], gffn_ref[...])
    xn_ref[...] = xn
    q = _dot(xn.astype(BF16), wq_ref[...])
    half = keys_ref.shape[-1]
    t = q.shape[0]
    for h in range(PEER_HEADS):
        scores = [_dot_nt(keys_ref[p], q[:, (2 * h + p) * half:(2 * h + p + 1) * half].astype(BF16)) for p in range(2)]
        for l0 in range(0, t, LANES):
            (s1, i1), (s2, i2) = [_top16(sc[:, l0:l0 + LANES]) for sc in scores]
            cand = _stair(s1, s2, lambda a, b: a + b)
            pad = lax.broadcasted_iota(jnp.int32, cand.shape, 0) >= len(_STAIR)
            expert = _stair(i1, i2, lambda a, b: a * n_keys + b)
            sc, e = _top16(jnp.where(pad, -jnp.inf, cand), expert)
            ex = jnp.exp(sc - sc[0:1, :])
            ids_ref[h * PEER_TOPK:(h + 1) * PEER_TOPK, l0:l0 + LANES] = e
            gate_ref[h * PEER_TOPK:(h + 1) * PEER_TOPK, l0:l0 + LANES] = ex / jnp.sum(ex, axis=0, keepdims=True)


def _route(x2, g_ffn, wq, keys):
    t, d = x2.shape
    rows = ROUTE_ROWS
    assert t % rows == 0
    picks = PEER_HEADS * PEER_TOPK
    return pl.pallas_call(
        functools.partial(_route_body, n_keys=keys.shape[1]),
        grid=(t // rows,),
        in_specs=[pl.BlockSpec((rows, d), lambda i: (i, 0)), _resident((1, d)), _resident(wq.shape), _resident(keys.shape)],
        out_specs=[pl.BlockSpec((rows, d), lambda i: (i, 0)), pl.BlockSpec((picks, rows), lambda i: (0, i)),
                   pl.BlockSpec((picks, rows), lambda i: (0, i))],
        out_shape=[jax.ShapeDtypeStruct((t, d), F32), jax.ShapeDtypeStruct((picks, t), jnp.int32),
                   jax.ShapeDtypeStruct((picks, t), F32)],
        compiler_params=pltpu.CompilerParams(dimension_semantics=("arbitrary",), vmem_limit_bytes=VMEM_LIMIT),
        name="route",
    )(x2, g_ffn, wq, keys)


def _gelu(h):
    return 0.5 * h * (1.0 + lax.erf(h * (2.0 ** -0.5)))


def _experts_body(ids_hbm, gate_ref, xn_ref, x2_ref, gfin_ref, uv_hbm, after_hbm, y_ref, *scratch,
                  rows, picks, first_step, steps, final_norm):
    del after_hbm
    bufs, (sem, ids_smem, ids_sem) = scratch[:GATHER_SLOTS], scratch[GATHER_SLOTS:]
    i = pl.program_id(0)
    half = i % 2
    n = rows * picks
    d = xn_ref.shape[-1]
    lead = GATHER_SLOTS - 1
    groups = picks // SUBLANES

    def ids_copy(step, which):
        return pltpu.make_async_copy(ids_hbm.at[first_step + step], ids_smem.at[pl.ds(which * n, n)], ids_sem.at[which])

    def issue(off, slot, j0, j1):
        for j in range(j0, j1):
            pltpu.async_copy(uv_hbm.at[ids_smem[off + j]], bufs[slot].at[pl.ds(j, 1)], sem.at[slot], priority=j % 2)

    def wait_slot(slot):
        pltpu.make_async_copy(bufs[slot], bufs[slot], sem.at[slot]).wait()

    @pl.when(i == 0)
    def _():
        ids_copy(0, 0).start()
        ids_copy(0, 0).wait()
        if steps > 1:
            ids_copy(1, 1).start()
        for s in range(lead):
            issue(s * picks, s, 0, picks)

    lane = lax.broadcasted_iota(jnp.int32, (picks, rows), 1)

    def token(t, slot, ahead_off):
        ahead_slot = (slot + lead) % GATHER_SLOTS
        buf = bufs[slot]
        wait_slot(slot)
        if ahead_off is not None:
            issue(ahead_off, ahead_slot, 0, picks)
        x = xn_ref[pl.ds(t, 1), :]
        g = jnp.sum(jnp.where(lane == t, gate_ref[...], 0.0), axis=-1, keepdims=True)
        hs = []
        for k in range(groups):
            r0 = k * SUBLANES
            hs.append(jnp.sum(buf[r0:r0 + SUBLANES, 0:d] * x, axis=-1, keepdims=True))
        w = g * _gelu(jnp.concatenate(hs, axis=0))
        acc = None
        for k in range(groups):
            r0 = k * SUBLANES
            part = w[r0:r0 + SUBLANES, :] * buf[r0:r0 + SUBLANES, d:2 * d]
            acc = part if acc is None else acc + part
        y = x2_ref[pl.ds(t, 1), :] + jnp.sum(acc, axis=0, keepdims=True)
        y_ref[pl.ds(t, 1), :] = _rms(y, gfin_ref[...]) if final_norm else y

    def four_tokens(k, carry):
        for s in range(GATHER_SLOTS):
            t = k * GATHER_SLOTS + s
            token(t, s, half * n + (t + lead) * picks)
        return carry

    lax.fori_loop(0, rows // GATHER_SLOTS - 1, four_tokens, 0)

    t0 = rows - GATHER_SLOTS
    token(t0, 0, half * n + (t0 + lead) * picks)

    @pl.when(i + 1 < steps)
    def _():
        ids_copy(i + 1, 1 - half).wait()

    for s in range(1, GATHER_SLOTS):
        @pl.when(i + 1 < steps)
        def _():
            issue((1 - half) * n + (s - 1) * picks, (s + lead) % GATHER_SLOTS, 0, picks)

        token(t0 + s, s, None)

    @pl.when(i + 2 < steps)
    def _():
        ids_copy(i + 2, half).start()


def _experts(ids_tok, gate_t, xn, x2, g_final, uv, final_norm, first_step, steps, after):
    t, picks = ids_tok.shape
    d = xn.shape[1]
    rows = EXPERT_ROWS
    assert t % rows == 0 and rows >= 2 * GATHER_SLOTS and rows % GATHER_SLOTS == 0 and picks % (2 * SUBLANES) == 0
    ids = ids_tok.reshape(t // rows, rows * picks)
    body = functools.partial(_experts_body, rows=rows, picks=picks, first_step=first_step, steps=steps,
                             final_norm=final_norm)
    return pl.pallas_call(
        body,
        grid=(steps,),
        in_specs=[
            pl.BlockSpec(memory_space=pl.ANY),
            pl.BlockSpec((picks, rows), lambda i: (0, first_step + i)),
            pl.BlockSpec((rows, d), lambda i: (first_step + i, 0)),
            pl.BlockSpec((rows, d), lambda i: (first_step + i, 0)),
            _resident((1, d)),
            pl.BlockSpec(memory_space=pl.ANY),
            pl.BlockSpec(memory_space=pl.ANY),
        ],
        out_specs=pl.BlockSpec((rows, d), lambda i: (i, 0)),
        out_shape=jax.ShapeDtypeStruct((steps * rows, d), F32),
        scratch_shapes=[pltpu.VMEM((picks, 2 * d), F32)] * GATHER_SLOTS + [
            pltpu.SemaphoreType.DMA((GATHER_SLOTS,)),
            pltpu.SMEM((2 * rows * picks,), jnp.int32),
            pltpu.SemaphoreType.DMA((2,)),
        ],
        compiler_params=pltpu.CompilerParams(dimension_semantics=("arbitrary",), vmem_limit_bytes=VMEM_LIMIT),
        name="experts",
    )(ids, gate_t, xn, x2, g_final, uv, after)


def _sc_dots(ids, xn, u, n_tok):
    d = xn.shape[1]
    nch = ids.shape[1]
    picks = nch * SC_CHUNK
    per_w = n_tok // SC_WORKERS
    mesh = plsc.VectorSubcoreMesh(core_axis_name="c", subcore_axis_name="s")

    @functools.partial(
        pl.kernel, mesh=mesh, out_type=jax.ShapeDtypeStruct((n_tok, picks), F32),
        scratch_types=[pltpu.VMEM((nch, SC_CHUNK), jnp.int32), pltpu.VMEM((d,), F32), pltpu.VMEM((2, SC_CHUNK, d), F32),
                       pltpu.VMEM((picks,), F32), pltpu.SemaphoreType.DMA((2,))],
        compiler_params=pltpu.CompilerParams(needs_layout_passes=False), name="sc_dots")
    def k(ids_hbm, xn_hbm, u_hbm, h_hbm, idx_v, x_v, rows_v, h_v, sem):
        wid = lax.axis_index("s") * SC_CORES + lax.axis_index("c")
        lane = lax.iota(jnp.int32, SC_LANES)

        def gather(c, b):
            return pltpu.make_async_copy(u_hbm.at[idx_v.at[c]], rows_v.at[b], sem.at[b])

        def token(i, carry):
            t = wid * per_w + i
            pltpu.sync_copy(ids_hbm.at[t], idx_v)
            pltpu.sync_copy(xn_hbm.at[t], x_v)
            gather(0, 0).start()
            for c in range(nch):
                b = c % 2
                if c + 1 < nch:
                    gather(c + 1, 1 - b).start()
                gather(c, b).wait()
                for g in range(SC_CHUNK // SC_LANES):
                    def rows4(q, hv):
                        j0 = g * SC_LANES + q * 4

                        def span(cc, accs):
                            for uu in range(SC_UNROLL):
                                off = pl.multiple_of((cc * SC_UNROLL + uu) * SC_LANES, SC_LANES)
                                xv = x_v[pl.ds(off, SC_LANES)]
                                accs = tuple(a + rows_v[b, j0 + kk, pl.ds(off, SC_LANES)] * xv
                                             for kk, a in enumerate(accs))
                            return accs

                        accs = lax.fori_loop(0, d // (SC_LANES * SC_UNROLL), span,
                                             tuple(jnp.zeros((SC_LANES,), F32) for _ in range(4)))
                        for kk in range(4):
                            hv = jnp.where(lane == q * 4 + kk, jnp.sum(accs[kk]), hv)
                        return hv

                    hv = lax.fori_loop(0, SC_LANES // 4, rows4, jnp.zeros((SC_LANES,), F32))
                    h_v[pl.ds(c * SC_CHUNK + g * SC_LANES, SC_LANES)] = hv
            pltpu.sync_copy(h_v, h_hbm.at[t])
            return carry

        lax.fori_loop(0, per_w, token, 0)

    return k(ids, xn, u)


def _sc_mix(ids, w, v, n_tok):
    d = v.shape[1]
    nch = ids.shape[1]
    picks = nch * SC_CHUNK
    per_w = n_tok // SC_WORKERS
    cb = 16
    mesh = plsc.VectorSubcoreMesh(core_axis_name="c", subcore_axis_name="s")

    @functools.partial(
        pl.kernel, mesh=mesh, out_type=jax.ShapeDtypeStruct((n_tok, d), F32),
        scratch_types=[pltpu.VMEM((nch, SC_CHUNK), jnp.int32), pltpu.VMEM((picks,), F32), pltpu.VMEM((2, SC_CHUNK, d), F32),
                       pltpu.VMEM((d,), F32), pltpu.SemaphoreType.DMA((2,))],
        compiler_params=pltpu.CompilerParams(needs_layout_passes=False), name="sc_mix")
    def k(ids_hbm, w_hbm, v_hbm, o_hbm, idx_v, w_v, rows_v, o_v, sem):
        wid = lax.axis_index("s") * SC_CORES + lax.axis_index("c")

        def gather(c, b):
            return pltpu.make_async_copy(v_hbm.at[idx_v.at[c]], rows_v.at[b], sem.at[b])

        def token(i, carry):
            t = wid * per_w + i
            pltpu.sync_copy(ids_hbm.at[t], idx_v)
            pltpu.sync_copy(w_hbm.at[t], w_v)
            gather(0, 0).start()
            for c in range(nch):
                b = c % 2
                if c + 1 < nch:
                    gather(c + 1, 1 - b).start()
                gather(c, b).wait()
                for blk in range(d // (cb * SC_LANES)):
                    base = blk * cb * SC_LANES
                    if c == 0:
                        init = tuple(jnp.zeros((SC_LANES,), F32) for _ in range(cb))
                    else:
                        init = tuple(o_v[pl.ds(base + kk * SC_LANES, SC_LANES)] for kk in range(cb))

                    def row(r, accs):
                        wj = plsc.load_gather(w_v, [jnp.full((SC_LANES,), c * SC_CHUNK, jnp.int32) + r])
                        return tuple(a + wj * rows_v[b, r, pl.ds(base + kk * SC_LANES, SC_LANES)]
                                     for kk, a in enumerate(accs))

                    accs = lax.fori_loop(0, SC_CHUNK, row, init)
                    for kk in range(cb):
                        o_v[pl.ds(base + kk * SC_LANES, SC_LANES)] = accs[kk]
            pltpu.sync_copy(o_v, o_hbm.at[t])
            return carry

        lax.fori_loop(0, per_w, token, 0)

    return k(ids, w, v)


def _gate_gelu_body(h_ref, gate_ref, after_hbm, w_ref):
    del after_hbm
    w_ref[...] = gate_ref[...] * _gelu(h_ref[...])


def _gate_gelu(h, gate_tok, after):
    n, picks = h.shape
    rows = math.gcd(n, 1024)
    assert n % rows == 0
    return pl.pallas_call(
        _gate_gelu_body, grid=(n // rows,),
        in_specs=[pl.BlockSpec((rows, picks), lambda i: (i, 0)), pl.BlockSpec((rows, picks), lambda i: (i, 0)),
                  pl.BlockSpec(memory_space=pl.ANY)],
        out_specs=pl.BlockSpec((rows, picks), lambda i: (i, 0)),
        out_shape=jax.ShapeDtypeStruct((n, picks), F32), name="gate_gelu",
    )(h, gate_tok, after)


def _finish_body(x2_ref, o_ref, gfin_ref, y_ref, *, final_norm):
    y = x2_ref[...] + o_ref[...]
    y_ref[...] = _rms(y, gfin_ref[...]) if final_norm else y


def _finish(x2, o, g_final, final_norm):
    n, d = o.shape
    rows = math.gcd(n, 512)
    assert n % rows == 0
    return pl.pallas_call(
        functools.partial(_finish_body, final_norm=final_norm), grid=(n // rows,),
        in_specs=[pl.BlockSpec((rows, d), lambda i: (i, 0)), pl.BlockSpec((rows, d), lambda i: (i, 0)), _resident((1, d))],
        out_specs=pl.BlockSpec((rows, d), lambda i: (i, 0)),
        out_shape=jax.ShapeDtypeStruct((n, d), F32), name="finish",
    )(x2, o, g_final)


def _peer(ids_t, gate_t, xn, x2, g_final, lw, final_norm):
    picks, t = ids_t.shape
    ids_tok = ids_t.T
    total_steps = t // EXPERT_ROWS
    sc_steps = int(total_steps * SC_SHARE) if t >= SC_MIN_TOKENS else 0
    n_sc = sc_steps * EXPERT_ROWS
    if n_sc == 0:
        return _experts(ids_tok, gate_t, xn, x2, g_final, lw["peer_uv"], final_norm, 0, total_steps, g_final)
    assert n_sc % SC_WORKERS == 0 and picks % SC_CHUNK == 0
    tc_steps = total_steps - sc_steps
    first = int(tc_steps * TC_FIRST_SHARE)
    ids_sc = ids_tok.reshape(t, picks // SC_CHUNK, SC_CHUNK)
    h = _sc_dots(ids_sc, xn, lw["peer_u"], n_sc)
    y_tc1 = _experts(ids_tok, gate_t, xn, x2, g_final, lw["peer_uv"], final_norm, sc_steps, first, g_final)
    w = _gate_gelu(h, gate_t.T, y_tc1)
    o = _sc_mix(ids_sc, w, lw["peer_v"], n_sc)
    y_tc2 = _experts(ids_tok, gate_t, xn, x2, g_final, lw["peer_uv"], final_norm, sc_steps + first, tc_steps - first, w)
    y_sc = _finish(x2, o, g_final, final_norm)
    return jnp.concatenate([y_sc, y_tc1, y_tc2], axis=0)


def _layer(x, conv_prev, pool_prev, start_pos, mk, mv, lw, g_final, final_norm):
    bn, seq, d = x.shape
    x2, new_conv, new_pool = _mixer(x, conv_prev, pool_prev, start_pos, mk, mv, lw)
    x2 = x2.reshape(bn * seq, d)
    xn, ids_t, gate_t = _route(x2, lw["g_ffn"], lw["peer_wq"], lw["peer_keys"])
    y = _peer(ids_t, gate_t, xn, x2, g_final, lw, final_norm)
    return y.reshape(bn, seq, d), new_conv, new_pool


def kernel(x_prompt, x_sample, mem_prompt, cache_conv, cache_pool, cache_mem_k, cache_mem_v, g_mix, w_in, conv_w, conv_b, pool_w, pool_scale, g_mem, w_mk, w_mv, w_bc, w_bp, w_ba, gate_b, w_o, g_ffn, peer_wq, peer_keys, peer_u, peer_v, g_final):
    depth, d = g_mix.shape
    xp, xs = x_prompt, x_sample
    bp, bs = xp.shape[0], xs.shape[0]
    n_exp = peer_u.shape[1]
    gfin = g_final.reshape(1, d)
    conv_p, pool_p, mk_p, mv_p, conv_s, pool_s = [], [], [], [], [], []
    for l in range(depth):
        lw = dict(
            g_mix=g_mix[l].reshape(1, d), w_in=w_in[l].astype(BF16), conv_w=conv_w[l], conv_b=conv_b[l].reshape(1, -1),
            pool_w=pool_w[l].astype(BF16), pool_scale=pool_scale[l].reshape(1, -1), w_bc=w_bc[l].astype(BF16),
            w_bp=w_bp[l].astype(BF16), w_ba=w_ba[l].astype(BF16), gate_b=gate_b[l].reshape(1, -1), w_o=w_o[l].astype(BF16),
            g_ffn=g_ffn[l].reshape(1, d), peer_wq=peer_wq[l].astype(BF16), peer_keys=peer_keys[l].astype(BF16),
            peer_uv=jnp.concatenate([peer_u[l], peer_v[l]], axis=1).reshape(n_exp, 1, 2 * d),
            peer_u=peer_u[l], peer_v=peer_v[l],
        )
        last = l == depth - 1
        mk, mv = _mem_kv(mem_prompt, g_mem[l], w_mk[l], w_mv[l])
        zc = jnp.zeros((bp, CONV_K - 1, conv_w.shape[-1]), xp.dtype)
        zp = jnp.zeros((bp, POOL_STATE, pool_scale.shape[-1]), xp.dtype)
        xp, cp, pp = _layer(xp, zc, zp, 0, mk, mv, lw, gfin, last)
        n_mem = cache_mem_k.shape[2]
        xs, cs, ps = _layer(xs, cache_conv[l], cache_pool[l], PAST_LEN, cache_mem_k[l].reshape(bs, n_mem, -1),
                            cache_mem_v[l].reshape(bs, n_mem, -1), lw, gfin, last)
        heads_shape = (bp, n_mem) + cache_mem_k.shape[3:]
        conv_p.append(cp); pool_p.append(pp); mk_p.append(mk.reshape(heads_shape)); mv_p.append(mv.reshape(heads_shape))
        conv_s.append(cs); pool_s.append(ps)
    return (xp, xs, jnp.stack(conv_p), jnp.stack(pool_p), jnp.stack(mk_p), jnp.stack(mv_p),
            jnp.stack(conv_s), jnp.stack(pool_s))
```

```python
import functools
import math

import jax
import jax.numpy as jnp
from jax import lax
from jax.experimental import pallas as pl
from jax.experimental.pallas import tpu as pltpu
from jax.experimental.pallas import tpu_sc as plsc

F32 = jnp.float32
BF16 = jnp.bfloat16
EPS = 1e-6

CONV_K = 3
POOL_WINDOWS = (2, 4, 8, 16)
POOL_STATE = max(POOL_WINDOWS) - 1
PAST_LEN = 1024
MEM_HEADS = 4
PEER_HEADS = 8
PEER_TOPK = 16

SUBLANES = 8
LANES = 128
MIXER_ROWS = 256
ROUTE_ROWS = 256
EXPERT_ROWS = 128
GATHER_SLOTS = 4
VMEM_LIMIT = 56 * 1024 * 1024

SC_CORES = 2
SC_SUBCORES = 16
SC_WORKERS = SC_CORES * SC_SUBCORES
SC_LANES = 16
SC_CHUNK = 32
SC_UNROLL = 8
SC_SHARE = 0.58
SC_MIN_TOKENS = 8192
TC_FIRST_SHARE = 0.54


def _rms(x, g):
    return x * lax.rsqrt(jnp.mean(x * x, axis=-1, keepdims=True) + EPS) * g


def _dot(a, b):
    return jnp.dot(a, b, preferred_element_type=F32)


def _dot_nt(a, b):
    return lax.dot_general(a, b, (((1,), (1,)), ((), ())), preferred_element_type=F32)


def _resident(shape):
    zeros = (0,) * len(shape)
    return pl.BlockSpec(shape, lambda *_: zeros, pipeline_mode=pl.Buffered(1))


def _memkv_body(mem_ref, g_ref, wk_ref, wv_ref, k_ref, v_ref):
    m = _rms(mem_ref[...], g_ref[...]).astype(BF16)
    k_ref[...] = _dot(m, wk_ref[...])
    v_ref[...] = _dot(m, wv_ref[...])


def _mem_kv(mem, g_mem, w_mk, w_mv):
    bn, n_mem, d = mem.shape
    att_w = w_mk.shape[1]
    k, v = pl.pallas_call(
        _memkv_body,
        out_shape=[jax.ShapeDtypeStruct((bn * n_mem, att_w), F32)] * 2,
        name="mem_kv",
    )(mem.reshape(bn * n_mem, d), g_mem.reshape(1, d), w_mk.astype(BF16), w_mv.astype(BF16))
    return k.reshape(bn, n_mem, att_w), v.reshape(bn, n_mem, att_w)


def _mixer_body(x_ref, cprev_ref, pprev_ref, mk_ref, mv_ref, gmix_ref, win_ref, convw_ref, convb_ref, poolw_ref,
                pscale_ref, wbc_ref, wbp_ref, wba_ref, gateb_ref, wo_ref,
                x2_ref, nconv_ref, npool_ref, zbuf, ubuf, *, rows, start_pos, conv_w, pool_w, att_w):
    s = pl.program_id(1)
    d_model = x_ref.shape[-1]
    z0 = SUBLANES
    u0 = 2 * SUBLANES

    @pl.when(s == 0)
    def _():
        zbuf[z0 - (CONV_K - 1):z0, :] = cprev_ref[0]
        ubuf[u0 - POOL_STATE:u0, :] = pprev_ref[0]

    x = x_ref[0]
    hb = _rms(x, gmix_ref[...]).astype(BF16)

    c0 = 0
    pa = _dot(hb, win_ref[:, c0:c0 + 3 * conv_w])
    xc, bg, cg = pa[:, :conv_w], pa[:, conv_w:2 * conv_w], pa[:, 2 * conv_w:]
    zbuf[z0:z0 + rows, :] = cg * xc
    cw = convw_ref[...]
    conv = zbuf[z0 - 2:z0 - 2 + rows, :] * cw[0:1]
    for k in range(1, CONV_K):
        conv = conv + zbuf[z0 - 2 + k:z0 - 2 + k + rows, :] * cw[k:k + 1]
    conv = conv + convb_ref[...]
    ya = _dot((bg * conv).astype(BF16), wbc_ref[...])
    last_z = zbuf[z0 + rows - (CONV_K - 1):z0 + rows, :]
    nconv_ref[0] = last_z
    zbuf[z0 - (CONV_K - 1):z0, :] = last_z
    c0 += 3 * conv_w

    up = _dot(hb, win_ref[:, c0:c0 + pool_w])
    ubuf[u0:u0 + rows, :] = up
    pos = start_pos + s * rows + lax.broadcasted_iota(jnp.int32, (rows, 1), 0)
    gw = pool_w // len(POOL_WINDOWS)
    ys = []
    for g, w in enumerate(POOL_WINDOWS):
        cur = up[:, g * gw:(g + 1) * gw]
        acc = cur
        for k in range(1, w):
            acc = acc + ubuf[u0 - k:u0 - k + rows, g * gw:(g + 1) * gw]
        cnt = jnp.minimum(pos + 1, w).astype(F32)
        ys.append(_dot((acc / cnt - cur).astype(BF16), poolw_ref[g]))
    yb = _dot((jnp.concatenate(ys, axis=-1) * pscale_ref[...]).astype(BF16), wbp_ref[...])
    last_u = ubuf[u0 + rows - POOL_STATE:u0 + rows, :]
    npool_ref[0] = last_u
    ubuf[u0 - POOL_STATE:u0, :] = last_u
    c0 += pool_w

    q = _dot(hb, win_ref[:, c0:c0 + att_w])
    kb = mk_ref[0].astype(BF16)
    vb = mv_ref[0].astype(BF16)
    hd = att_w // MEM_HEADS
    heads = []
    for h in range(MEM_HEADS):
        sc = _dot_nt(q[:, h * hd:(h + 1) * hd].astype(BF16), kb[:, h * hd:(h + 1) * hd]) * (hd ** -0.5)
        e = jnp.exp(sc - jnp.max(sc, axis=-1, keepdims=True))
        p = e / jnp.sum(e, axis=-1, keepdims=True)
        heads.append(_dot(p.astype(BF16), vb[:, h * hd:(h + 1) * hd]))
    yc = _dot(jnp.concatenate(heads, axis=-1).astype(BF16), wba_ref[...])
    c0 += att_w

    merged = None
    for i, y in enumerate((ya, yb, yc)):
        gl = _dot(hb, win_ref[:, c0 + i * d_model:c0 + (i + 1) * d_model]) + gateb_ref[:, i * d_model:(i + 1) * d_model]
        term = (1.0 / (1.0 + jnp.exp(-gl))) * y
        merged = term if merged is None else merged + term
    x2_ref[0] = x + _dot(merged.astype(BF16), wo_ref[...])


def _mixer(x, conv_prev, pool_prev, start_pos, mk, mv, lw):
    bn, seq, d = x.shape
    conv_w, pool_w = conv_prev.shape[-1], pool_prev.shape[-1]
    n_mem, att_w = mk.shape[1], mk.shape[2]
    rows = min(MIXER_ROWS, seq)
    assert seq % rows == 0 and rows % SUBLANES == 0 and rows >= POOL_STATE
    in_cols = lw["w_in"].shape[1]
    per_b = lambda shape: pl.BlockSpec((1,) + shape, lambda b, s: (b, 0, 0))
    body = functools.partial(_mixer_body, rows=rows, start_pos=start_pos, conv_w=conv_w, pool_w=pool_w, att_w=att_w)
    return pl.pallas_call(
        body,
        grid=(bn, seq // rows),
        in_specs=[
            pl.BlockSpec((1, rows, d), lambda b, s: (b, s, 0)),
            per_b((CONV_K - 1, conv_w)), per_b((POOL_STATE, pool_w)), per_b((n_mem, att_w)), per_b((n_mem, att_w)),
            _resident((1, d)), _resident((d, in_cols)), _resident((CONV_K, conv_w)), _resident((1, conv_w)),
            _resident(lw["pool_w"].shape), _resident((1, pool_w)), _resident((conv_w, d)), _resident((pool_w, d)),
            _resident((att_w, d)), _resident((1, 3 * d)), _resident((d, d)),
        ],
        out_specs=[
            pl.BlockSpec((1, rows, d), lambda b, s: (b, s, 0)),
            per_b((CONV_K - 1, conv_w)), per_b((POOL_STATE, pool_w)),
        ],
        out_shape=[
            jax.ShapeDtypeStruct((bn, seq, d), F32),
            jax.ShapeDtypeStruct((bn, CONV_K - 1, conv_w), F32),
            jax.ShapeDtypeStruct((bn, POOL_STATE, pool_w), F32),
        ],
        scratch_shapes=[pltpu.VMEM((SUBLANES + rows, conv_w), F32), pltpu.VMEM((2 * SUBLANES + rows, pool_w), F32)],
        compiler_params=pltpu.CompilerParams(dimension_semantics=("arbitrary", "arbitrary"), vmem_limit_bytes=VMEM_LIMIT),
        name="mixer",
    )(x, conv_prev, pool_prev, mk, mv, lw["g_mix"], lw["w_in"], lw["conv_w"], lw["conv_b"], lw["pool_w"],
      lw["pool_scale"], lw["w_bc"], lw["w_bp"], lw["w_ba"], lw["gate_b"], lw["w_o"])


_STAIR = [(a, b) for a in range(PEER_TOPK) for b in range(PEER_TOPK) if (a + 1) * (b + 1) <= PEER_TOPK]
_STAIR_HEAD = 24
_STAIR_TAIL = [_STAIR[_STAIR_HEAD + 8 * i:_STAIR_HEAD + 8 * (i + 1)] for i in range(4)]


def _top16(v, payload=None):
    n, t = v.shape
    rows = lax.broadcasted_iota(jnp.int32, (n, t), 0)
    out_rows = lax.broadcasted_iota(jnp.int32, (PEER_TOPK, t), 0)
    vals = jnp.zeros((PEER_TOPK, t), F32)
    pay = jnp.zeros((PEER_TOPK, t), jnp.int32)
    for k in range(PEER_TOPK):
        m = jnp.max(v, axis=0, keepdims=True)
        am = jnp.min(jnp.where(v == m, rows, n), axis=0, keepdims=True)
        hit = rows == am
        p = am if payload is None else jnp.sum(jnp.where(hit, payload, 0), axis=0, keepdims=True)
        vals = jnp.where(out_rows == k, m, vals)
        pay = jnp.where(out_rows == k, p, pay)
        v = jnp.where(hit, -jnp.inf, v)
    return vals, pay


def _take_rows(src, idxs):
    t = src.shape[1]
    out_rows = lax.broadcasted_iota(jnp.int32, (SUBLANES, t), 0)
    acc = jnp.broadcast_to(src[idxs[0]:idxs[0] + 1, :], (SUBLANES, t))
    for i in range(1, SUBLANES):
        acc = jnp.where(out_rows == i, src[idxs[i]:idxs[i] + 1, :], acc)
    return acc


def _stair(first, second, combine):
    blocks = [combine(first[0:1, :], second), combine(first[1:2, :], second[0:SUBLANES, :])]
    for blk in _STAIR_TAIL:
        blk = blk + [(0, 0)] * (SUBLANES - len(blk))
        blocks.append(combine(_take_rows(first, [a for a, _ in blk]), _take_rows(second, [b for _, b in blk])))
    return jnp.concatenate(blocks, axis=0)


def _route_body(x2_ref, gffn_ref, wq_ref, keys_ref, xn_ref, ids_ref, gate_ref, *, n_keys):
    xn = _rms(x2_ref[...], gffn_ref[...])
    xn_ref[...] = xn
    q = _dot(xn.astype(BF16), wq_ref[...])
    half = keys_ref.shape[-1]
    t = q.shape[0]
    for h in range(PEER_HEADS):
        scores = [_dot_nt(keys_ref[p], q[:, (2 * h + p) * half:(2 * h + p + 1) * half].astype(BF16)) for p in range(2)]
        for l0 in range(0, t, LANES):
            (s1, i1), (s2, i2) = [_top16(sc[:, l0:l0 + LANES]) for sc in scores]
            cand = _stair(s1, s2, lambda a, b: a + b)
            pad = lax.broadcasted_iota(jnp.int32, cand.shape, 0) >= len(_STAIR)
            expert = _stair(i1, i2, lambda a, b: a * n_keys + b)
            sc, e = _top16(jnp.where(pad, -jnp.inf, cand), expert)
            ex = jnp.exp(sc - sc[0:1, :])
            ids_ref[h * PEER_TOPK:(h + 1) * PEER_TOPK, l0:l0 + LANES] = e
            gate_ref[h * PEER_TOPK:(h + 1) * PEER_TOPK, l0:l0 + LANES] = ex / jnp.sum(ex, axis=0, keepdims=True)


def _route(x2, g_ffn, wq, keys):
    t, d = x2.shape
    rows = ROUTE_ROWS
    assert t % rows == 0
    picks = PEER_HEADS * PEER_TOPK
    return pl.pallas_call(
        functools.partial(_route_body, n_keys=keys.shape[1]),
        grid=(t // rows,),
        in_specs=[pl.BlockSpec((rows, d), lambda i: (i, 0)), _resident((1, d)), _resident(wq.shape), _resident(keys.shape)],
        out_specs=[pl.BlockSpec((rows, d), lambda i: (i, 0)), pl.BlockSpec((picks, rows), lambda i: (0, i)),
                   pl.BlockSpec((picks, rows), lambda i: (0, i))],
        out_shape=[jax.ShapeDtypeStruct((t, d), F32), jax.ShapeDtypeStruct((picks, t), jnp.int32),
                   jax.ShapeDtypeStruct((picks, t), F32)],
        compiler_params=pltpu.CompilerParams(dimension_semantics=("arbitrary",), vmem_limit_bytes=VMEM_LIMIT),
        name="route",
    )(x2, g_ffn, wq, keys)


def _gelu(h):
    return 0.5 * h * (1.0 + lax.erf(h * (2.0 ** -0.5)))


def _experts_body(ids_hbm, gate_ref, xn_ref, x2_ref, gfin_ref, uv_hbm, after_hbm, y_ref, *scratch,
                  rows, picks, first_step, steps, final_norm):
    del after_hbm
    bufs, (sem, ids_smem, ids_sem) = scratch[:GATHER_SLOTS], scratch[GATHER_SLOTS:]
    i = pl.program_id(0)
    half = i % 2
    n = rows * picks
    d = xn_ref.shape[-1]
    lead = GATHER_SLOTS - 1
    groups = picks // SUBLANES

    def ids_copy(step, which):
        return pltpu.make_async_copy(ids_hbm.at[first_step + step], ids_smem.at[pl.ds(which * n, n)], ids_sem.at[which])

    def issue(off, slot, j0, j1):
        for j in range(j0, j1):
            pltpu.async_copy(uv_hbm.at[ids_smem[off + j]], bufs[slot].at[pl.ds(j, 1)], sem.at[slot], priority=j % 2)

    def wait_slot(slot):
        pltpu.make_async_copy(bufs[slot], bufs[slot], sem.at[slot]).wait()

    @pl.when(i == 0)
    def _():
        ids_copy(0, 0).start()
        ids_copy(0, 0).wait()
        if steps > 1:
            ids_copy(1, 1).start()
        for s in range(lead):
            issue(s * picks, s, 0, picks)

    lane = lax.broadcasted_iota(jnp.int32, (picks, rows), 1)

    def token(t, slot, ahead_off):
        ahead_slot = (slot + lead) % GATHER_SLOTS
        buf = bufs[slot]
        wait_slot(slot)
        if ahead_off is not None:
            issue(ahead_off, ahead_slot, 0, picks)
        x = xn_ref[pl.ds(t, 1), :]
        g = jnp.sum(jnp.where(lane == t, gate_ref[...], 0.0), axis=-1, keepdims=True)
        hs = []
        for k in range(groups):
            r0 = k * SUBLANES
            hs.append(jnp.sum(buf[r0:r0 + SUBLANES, 0:d] * x, axis=-1, keepdims=True))
        w = g * _gelu(jnp.concatenate(hs, axis=0))
        acc = None
        for k in range(groups):
            r0 = k * SUBLANES
            part = w[r0:r0 + SUBLANES, :] * buf[r0:r0 + SUBLANES, d:2 * d]
            acc = part if acc is None else acc + part
        y = x2_ref[pl.ds(t, 1), :] + jnp.sum(acc, axis=0, keepdims=True)
        y_ref[pl.ds(t, 1), :] = _rms(y, gfin_ref[...]) if final_norm else y

    def four_tokens(k, carry):
        for s in range(GATHER_SLOTS):
            t = k * GATHER_SLOTS + s
            token(t, s, half * n + (t + lead) * picks)
        return carry

    lax.fori_loop(0, rows // GATHER_SLOTS - 1, four_tokens, 0)

    t0 = rows - GATHER_SLOTS
    token(t0, 0, half * n + (t0 + lead) * picks)

    @pl.when(i + 1 < steps)
    def _():
        ids_copy(i + 1, 1 - half).wait()

    for s in range(1, GATHER_SLOTS):
        @pl.when(i + 1 < steps)
        def _():
            issue((1 - half) * n + (s - 1) * picks, (s + lead) % GATHER_SLOTS, 0, picks)

        token(t0 + s, s, None)

    @pl.when(i + 2 < steps)
    def _():
        ids_copy(i + 2, half).start()


def _experts(ids_tok, gate_t, xn, x2, g_final, uv, final_norm, first_step, steps, after):
    t, picks = ids_tok.shape
    d = xn.shape[1]
    rows = EXPERT_ROWS
    assert t % rows == 0 and rows >= 2 * GATHER_SLOTS and rows % GATHER_SLOTS == 0 and picks % (2 * SUBLANES) == 0
    ids = ids_tok.reshape(t // rows, rows * picks)
    body = functools.partial(_experts_body, rows=rows, picks=picks, first_step=first_step, steps=steps,
                             final_norm=final_norm)
    return pl.pallas_call(
        body,
        grid=(steps,),
        in_specs=[
            pl.BlockSpec(memory_space=pl.ANY),
            pl.BlockSpec((picks, rows), lambda i: (0, first_step + i)),
            pl.BlockSpec((rows, d), lambda i: (first_step + i, 0)),
            pl.BlockSpec((rows, d), lambda i: (first_step + i, 0)),
            _resident((1, d)),
            pl.BlockSpec(memory_space=pl.ANY),
            pl.BlockSpec(memory_space=pl.ANY),
        ],
        out_specs=pl.BlockSpec((rows, d), lambda i: (i, 0)),
        out_shape=jax.ShapeDtypeStruct((steps * rows, d), F32),
        scratch_shapes=[pltpu.VMEM((picks, 2 * d), F32)] * GATHER_SLOTS + [
            pltpu.SemaphoreType.DMA((GATHER_SLOTS,)),
            pltpu.SMEM((2 * rows * picks,), jnp.int32),
            pltpu.SemaphoreType.DMA((2,)),
        ],
        compiler_params=pltpu.CompilerParams(dimension_semantics=("arbitrary",), vmem_limit_bytes=VMEM_LIMIT),
        name="experts",
    )(ids, gate_t, xn, x2, g_final, uv, after)


def _sc_token_pipeline(per_w, nch, wid, ids_hbm, vec_hbm, tab_hbm, out_hbm, idx_v, vec_v, rows_v, out_v,
                       gsem, isem, vsem, osem, compute):
    base = wid * per_w

    def ids_copy(t, p):
        return pltpu.make_async_copy(ids_hbm.at[t], idx_v.at[p], isem.at[p])

    def vec_copy(t, p):
        return pltpu.make_async_copy(vec_hbm.at[t], vec_v.at[p], vsem.at[p])

    def out_copy(t, p):
        return pltpu.make_async_copy(out_v.at[p], out_hbm.at[t], osem.at[p])

    def gather(p, c, b):
        return pltpu.make_async_copy(tab_hbm.at[idx_v.at[p, c]], rows_v.at[b], gsem.at[b])

    ids_copy(base, 0).start()
    vec_copy(base, 0).start()
    ids_copy(base, 0).wait()
    vec_copy(base, 0).wait()
    gather(0, 0, 0).start()

    def pair(ii, carry):
        for p in range(2):
            i = 2 * ii + p
            t = base + i
            has_next = i + 1 < per_w

            @pl.when(has_next)
            def _():
                ids_copy(t + 1, 1 - p).start()
                vec_copy(t + 1, 1 - p).start()

            @pl.when(i >= 2)
            def _():
                out_copy(t - 2, p).wait()

            for c in range(nch):
                b = c % 2
                if c + 1 < nch:
                    gather(p, c + 1, 1 - b).start()
                else:
                    @pl.when(has_next)
                    def _():
                        ids_copy(t + 1, 1 - p).wait()
                        vec_copy(t + 1, 1 - p).wait()
                        gather(1 - p, 0, 1 - b).start()
                gather(p, c, b).wait()
                compute(p, c, b)
            out_copy(t, p).start()
        return carry

    lax.fori_loop(0, per_w // 2, pair, 0)
    out_copy(base + per_w - 2, 0).wait()
    out_copy(base + per_w - 1, 1).wait()


def _sc_dots(ids, xn, u, n_tok):
    d = xn.shape[1]
    nch = ids.shape[1]
    picks = nch * SC_CHUNK
    per_w = n_tok // SC_WORKERS
    assert per_w % 2 == 0 and nch % 2 == 0
    mesh = plsc.VectorSubcoreMesh(core_axis_name="c", subcore_axis_name="s")

    @functools.partial(
        pl.kernel, mesh=mesh, out_type=jax.ShapeDtypeStruct((n_tok, picks), F32),
        scratch_types=[pltpu.VMEM((2, nch, SC_CHUNK), jnp.int32), pltpu.VMEM((2, d), F32),
                       pltpu.VMEM((2, SC_CHUNK, d), F32), pltpu.VMEM((2, picks), F32)]
        + [pltpu.SemaphoreType.DMA((2,))] * 4,
        compiler_params=pltpu.CompilerParams(needs_layout_passes=False), name="sc_dots")
    def k(ids_hbm, xn_hbm, u_hbm, h_hbm, idx_v, x_v, rows_v, h_v, gsem, isem, vsem, osem):
        wid = lax.axis_index("s") * SC_CORES + lax.axis_index("c")
        lane = lax.iota(jnp.int32, SC_LANES)

        def compute(p, c, b):
            for g in range(SC_CHUNK // SC_LANES):
                def rows4(q, hv):
                    j0 = g * SC_LANES + q * 4

                    def span(cc, accs):
                        for uu in range(SC_UNROLL):
                            off = pl.multiple_of((cc * SC_UNROLL + uu) * SC_LANES, SC_LANES)
                            xv = x_v[p, pl.ds(off, SC_LANES)]
                            accs = tuple(a + rows_v[b, j0 + kk, pl.ds(off, SC_LANES)] * xv for kk, a in enumerate(accs))
                        return accs

                    accs = lax.fori_loop(0, d // (SC_LANES * SC_UNROLL), span,
                                         tuple(jnp.zeros((SC_LANES,), F32) for _ in range(4)))
                    for kk in range(4):
                        hv = jnp.where(lane == q * 4 + kk, jnp.sum(accs[kk]), hv)
                    return hv

                hv = lax.fori_loop(0, SC_LANES // 4, rows4, jnp.zeros((SC_LANES,), F32))
                h_v[p, pl.ds(c * SC_CHUNK + g * SC_LANES, SC_LANES)] = hv

        _sc_token_pipeline(per_w, nch, wid, ids_hbm, xn_hbm, u_hbm, h_hbm, idx_v, x_v, rows_v, h_v,
                           gsem, isem, vsem, osem, compute)

    return k(ids, xn, u)


def _sc_mix(ids, w, v, n_tok):
    d = v.shape[1]
    nch = ids.shape[1]
    picks = nch * SC_CHUNK
    per_w = n_tok // SC_WORKERS
    assert per_w % 2 == 0 and nch % 2 == 0
    cb = 16
    mesh = plsc.VectorSubcoreMesh(core_axis_name="c", subcore_axis_name="s")

    @functools.partial(
        pl.kernel, mesh=mesh, out_type=jax.ShapeDtypeStruct((n_tok, d), F32),
        scratch_types=[pltpu.VMEM((2, nch, SC_CHUNK), jnp.int32), pltpu.VMEM((2, picks), F32),
                       pltpu.VMEM((2, SC_CHUNK, d), F32), pltpu.VMEM((2, d), F32)]
        + [pltpu.SemaphoreType.DMA((2,))] * 4,
        compiler_params=pltpu.CompilerParams(needs_layout_passes=False), name="sc_mix")
    def k(ids_hbm, w_hbm, v_hbm, o_hbm, idx_v, w_v, rows_v, o_v, gsem, isem, vsem, osem):
        wid = lax.axis_index("s") * SC_CORES + lax.axis_index("c")

        def compute(p, c, b):
            for blk in range(d // (cb * SC_LANES)):
                base = blk * cb * SC_LANES
                if c == 0:
                    init = tuple(jnp.zeros((SC_LANES,), F32) for _ in range(cb))
                else:
                    init = tuple(o_v[p, pl.ds(base + kk * SC_LANES, SC_LANES)] for kk in range(cb))

                def row(r, accs):
                    wj = plsc.load_gather(w_v.at[p], [jnp.full((SC_LANES,), c * SC_CHUNK, jnp.int32) + r])
                    return tuple(a + wj * rows_v[b, r, pl.ds(base + kk * SC_LANES, SC_LANES)] for kk, a in enumerate(accs))

                accs = lax.fori_loop(0, SC_CHUNK, row, init)
                for kk in range(cb):
                    o_v[p, pl.ds(base + kk * SC_LANES, SC_LANES)] = accs[kk]

        _sc_token_pipeline(per_w, nch, wid, ids_hbm, w_hbm, v_hbm, o_hbm, idx_v, w_v, rows_v, o_v,
                           gsem, isem, vsem, osem, compute)

    return k(ids, w, v)


def _gate_gelu_body(h_ref, gate_ref, after_hbm, w_ref):
    del after_hbm
    w_ref[...] = gate_ref[...] * _gelu(h_ref[...])


def _gate_gelu(h, gate_tok, after):
    n, picks = h.shape
    rows = math.gcd(n, 1024)
    assert n % rows == 0
    return pl.pallas_call(
        _gate_gelu_body, grid=(n // rows,),
        in_specs=[pl.BlockSpec((rows, picks), lambda i: (i, 0)), pl.BlockSpec((rows, picks), lambda i: (i, 0)),
                  pl.BlockSpec(memory_space=pl.ANY)],
        out_specs=pl.BlockSpec((rows, picks), lambda i: (i, 0)),
        out_shape=jax.ShapeDtypeStruct((n, picks), F32), name="gate_gelu",
    )(h, gate_tok, after)


def _finish_body(x2_ref, o_ref, gfin_ref, y_ref, *, final_norm):
    y = x2_ref[...] + o_ref[...]
    y_ref[...] = _rms(y, gfin_ref[...]) if final_norm else y


def _finish(x2, o, g_final, final_norm):
    n, d = o.shape
    rows = math.gcd(n, 512)
    assert n % rows == 0
    return pl.pallas_call(
        functools.partial(_finish_body, final_norm=final_norm), grid=(n // rows,),
        in_specs=[pl.BlockSpec((rows, d), lambda i: (i, 0)), pl.BlockSpec((rows, d), lambda i: (i, 0)), _resident((1, d))],
        out_specs=pl.BlockSpec((rows, d), lambda i: (i, 0)),
        out_shape=jax.ShapeDtypeStruct((n, d), F32), name="finish",
    )(x2, o, g_final)


def _peer(ids_t, gate_t, xn, x2, g_final, lw, final_norm):
    picks, t = ids_t.shape
    ids_tok = ids_t.T
    total_steps = t // EXPERT_ROWS
    sc_steps = int(total_steps * SC_SHARE) if t >= SC_MIN_TOKENS else 0
    n_sc = sc_steps * EXPERT_ROWS
    if n_sc == 0:
        return _experts(ids_tok, gate_t, xn, x2, g_final, lw["peer_uv"], final_norm, 0, total_steps, g_final)
    assert n_sc % SC_WORKERS == 0 and picks % SC_CHUNK == 0
    tc_steps = total_steps - sc_steps
    first = int(tc_steps * TC_FIRST_SHARE)
    ids_sc = ids_tok.reshape(t, picks // SC_CHUNK, SC_CHUNK)
    h = _sc_dots(ids_sc, xn, lw["peer_u"], n_sc)
    y_tc1 = _experts(ids_tok, gate_t, xn, x2, g_final, lw["peer_uv"], final_norm, sc_steps, first, g_final)
    w = _gate_gelu(h, gate_t.T, y_tc1)
    o = _sc_mix(ids_sc, w, lw["peer_v"], n_sc)
    y_tc2 = _experts(ids_tok, gate_t, xn, x2, g_final, lw["peer_uv"], final_norm, sc_steps + first, tc_steps - first, w)
    y_sc = _finish(x2, o, g_final, final_norm)
    return jnp.concatenate([y_sc, y_tc1, y_tc2], axis=0)


def _layer(x, conv_prev, pool_prev, start_pos, mk, mv, lw, g_final, final_norm):
    bn, seq, d = x.shape
    x2, new_conv, new_pool = _mixer(x, conv_prev, pool_prev, start_pos, mk, mv, lw)
    x2 = x2.reshape(bn * seq, d)
    xn, ids_t, gate_t = _route(x2, lw["g_ffn"], lw["peer_wq"], lw["peer_keys"])
    y = _peer(ids_t, gate_t, xn, x2, g_final, lw, final_norm)
    return y.reshape(bn, seq, d), new_conv, new_pool


def kernel(x_prompt, x_sample, mem_prompt, cache_conv, cache_pool, cache_mem_k, cache_mem_v, g_mix, w_in, conv_w, conv_b, pool_w, pool_scale, g_mem, w_mk, w_mv, w_bc, w_bp, w_ba, gate_b, w_o, g_ffn, peer_wq, peer_keys, peer_u, peer_v, g_final):
    depth, d = g_mix.shape
    xp, xs = x_prompt, x_sample
    bp, bs = xp.shape[0], xs.shape[0]
    n_exp = peer_u.shape[1]
    gfin = g_final.reshape(1, d)
    conv_p, pool_p, mk_p, mv_p, conv_s, pool_s = [], [], [], [], [], []
    for l in range(depth):
        lw = dict(
            g_mix=g_mix[l].reshape(1, d), w_in=w_in[l].astype(BF16), conv_w=conv_w[l], conv_b=conv_b[l].reshape(1, -1),
            pool_w=pool_w[l].astype(BF16), pool_scale=pool_scale[l].reshape(1, -1), w_bc=w_bc[l].astype(BF16),
            w_bp=w_bp[l].astype(BF16), w_ba=w_ba[l].astype(BF16), gate_b=gate_b[l].reshape(1, -1), w_o=w_o[l].astype(BF16),
            g_ffn=g_ffn[l].reshape(1, d), peer_wq=peer_wq[l].astype(BF16), peer_keys=peer_keys[l].astype(BF16),
            peer_uv=jnp.concatenate([peer_u[l], peer_v[l]], axis=1).reshape(n_exp, 1, 2 * d),
            peer_u=peer_u[l], peer_v=peer_v[l],
        )
        last = l == depth - 1
        mk, mv = _mem_kv(mem_prompt, g_mem[l], w_mk[l], w_mv[l])
        zc = jnp.zeros((bp, CONV_K - 1, conv_w.shape[-1]), xp.dtype)
        zp = jnp.zeros((bp, POOL_STATE, pool_scale.shape[-1]), xp.dtype)
        xp, cp, pp = _layer(xp, zc, zp, 0, mk, mv, lw, gfin, last)
        n_mem = cache_mem_k.shape[2]
        xs, cs, ps = _layer(xs, cache_conv[l], cache_pool[l], PAST_LEN, cache_mem_k[l].reshape(bs, n_mem, -1),
                            cache_mem_v[l].reshape(bs, n_mem, -1), lw, gfin, last)
        heads_shape = (bp, n_mem) + cache_mem_k.shape[3:]
        conv_p.append(cp); pool_p.append(pp); mk_p.append(mk.reshape(heads_shape)); mv_p.append(mv.reshape(heads_shape))
        conv_s.append(cs); pool_s.append(ps)
    return (xp, xs, jnp.stack(conv_p), jnp.stack(pool_p), jnp.stack(mk_p), jnp.stack(mv_p),
            jnp.stack(conv_s), jnp.stack(pool_s))
```

```python
import functools
import math

import jax
import jax.numpy as jnp
from jax import lax
from jax.experimental import pallas as pl
from jax.experimental.pallas import tpu as pltpu
from jax.experimental.pallas import tpu_sc as plsc

F32 = jnp.float32
BF16 = jnp.bfloat16
EPS = 1e-6

CONV_K = 3
POOL_WINDOWS = (2, 4, 8, 16)
POOL_STATE = max(POOL_WINDOWS) - 1
PAST_LEN = 1024
MEM_HEADS = 4
PEER_HEADS = 8
PEER_TOPK = 16

SUBLANES = 8
LANES = 128
MIXER_ROWS = 256
ROUTE_ROWS = 256
EXPERT_ROWS = 128
GATHER_SLOTS = 4
VMEM_LIMIT = 56 * 1024 * 1024

SC_CORES = 2
SC_SUBCORES = 16
SC_WORKERS = SC_CORES * SC_SUBCORES
SC_LANES = 16
SC_CHUNK = 32
SC_UNROLL = 8
SC_SHARE = 0.61
SC_MIN_TOKENS = 8192
TC_FIRST_SHARE = 0.59


def _rms(x, g):
    return x * lax.rsqrt(jnp.mean(x * x, axis=-1, keepdims=True) + EPS) * g


def _dot(a, b):
    return jnp.dot(a, b, preferred_element_type=F32)


def _dot_nt(a, b):
    return lax.dot_general(a, b, (((1,), (1,)), ((), ())), preferred_element_type=F32)


def _resident(shape):
    zeros = (0,) * len(shape)
    return pl.BlockSpec(shape, lambda *_: zeros, pipeline_mode=pl.Buffered(1))


def _memkv_body(mem_ref, g_ref, wk_ref, wv_ref, k_ref, v_ref):
    m = _rms(mem_ref[...], g_ref[...]).astype(BF16)
    k_ref[...] = _dot(m, wk_ref[...])
    v_ref[...] = _dot(m, wv_ref[...])


def _mem_kv(mem, g_mem, w_mk, w_mv):
    bn, n_mem, d = mem.shape
    att_w = w_mk.shape[1]
    k, v = pl.pallas_call(
        _memkv_body,
        out_shape=[jax.ShapeDtypeStruct((bn * n_mem, att_w), F32)] * 2,
        name="mem_kv",
    )(mem.reshape(bn * n_mem, d), g_mem.reshape(1, d), w_mk.astype(BF16), w_mv.astype(BF16))
    return k.reshape(bn, n_mem, att_w), v.reshape(bn, n_mem, att_w)


def _mixer_body(x_ref, cprev_ref, pprev_ref, mk_ref, mv_ref, gmix_ref, win_ref, convw_ref, convb_ref, poolw_ref,
                pscale_ref, wbc_ref, wbp_ref, wba_ref, gateb_ref, wo_ref,
                x2_ref, nconv_ref, npool_ref, zbuf, ubuf, *, rows, start_pos, conv_w, pool_w, att_w):
    s = pl.program_id(1)
    d_model = x_ref.shape[-1]
    z0 = SUBLANES
    u0 = 2 * SUBLANES

    @pl.when(s == 0)
    def _():
        zbuf[z0 - (CONV_K - 1):z0, :] = cprev_ref[0]
        ubuf[u0 - POOL_STATE:u0, :] = pprev_ref[0]

    x = x_ref[0]
    hb = _rms(x, gmix_ref[...]).astype(BF16)

    c0 = 0
    pa = _dot(hb, win_ref[:, c0:c0 + 3 * conv_w])
    xc, bg, cg = pa[:, :conv_w], pa[:, conv_w:2 * conv_w], pa[:, 2 * conv_w:]
    zbuf[z0:z0 + rows, :] = cg * xc
    cw = convw_ref[...]
    conv = zbuf[z0 - 2:z0 - 2 + rows, :] * cw[0:1]
    for k in range(1, CONV_K):
        conv = conv + zbuf[z0 - 2 + k:z0 - 2 + k + rows, :] * cw[k:k + 1]
    conv = conv + convb_ref[...]
    ya = _dot((bg * conv).astype(BF16), wbc_ref[...])
    last_z = zbuf[z0 + rows - (CONV_K - 1):z0 + rows, :]
    nconv_ref[0] = last_z
    zbuf[z0 - (CONV_K - 1):z0, :] = last_z
    c0 += 3 * conv_w

    up = _dot(hb, win_ref[:, c0:c0 + pool_w])
    ubuf[u0:u0 + rows, :] = up
    pos = start_pos + s * rows + lax.broadcasted_iota(jnp.int32, (rows, 1), 0)
    gw = pool_w // len(POOL_WINDOWS)
    ys = []
    for g, w in enumerate(POOL_WINDOWS):
        cur = up[:, g * gw:(g + 1) * gw]
        acc = cur
        for k in range(1, w):
            acc = acc + ubuf[u0 - k:u0 - k + rows, g * gw:(g + 1) * gw]
        cnt = jnp.minimum(pos + 1, w).astype(F32)
        ys.append(_dot((acc / cnt - cur).astype(BF16), poolw_ref[g]))
    yb = _dot((jnp.concatenate(ys, axis=-1) * pscale_ref[...]).astype(BF16), wbp_ref[...])
    last_u = ubuf[u0 + rows - POOL_STATE:u0 + rows, :]
    npool_ref[0] = last_u
    ubuf[u0 - POOL_STATE:u0, :] = last_u
    c0 += pool_w

    q = _dot(hb, win_ref[:, c0:c0 + att_w])
    kb = mk_ref[0].astype(BF16)
    vb = mv_ref[0].astype(BF16)
    hd = att_w // MEM_HEADS
    heads = []
    for h in range(MEM_HEADS):
        sc = _dot_nt(q[:, h * hd:(h + 1) * hd].astype(BF16), kb[:, h * hd:(h + 1) * hd]) * (hd ** -0.5)
        e = jnp.exp(sc - jnp.max(sc, axis=-1, keepdims=True))
        p = e / jnp.sum(e, axis=-1, keepdims=True)
        heads.append(_dot(p.astype(BF16), vb[:, h * hd:(h + 1) * hd]))
    yc = _dot(jnp.concatenate(heads, axis=-1).astype(BF16), wba_ref[...])
    c0 += att_w

    merged = None
    for i, y in enumerate((ya, yb, yc)):
        gl = _dot(hb, win_ref[:, c0 + i * d_model:c0 + (i + 1) * d_model]) + gateb_ref[:, i * d_model:(i + 1) * d_model]
        term = (1.0 / (1.0 + jnp.exp(-gl))) * y
        merged = term if merged is None else merged + term
    x2_ref[0] = x + _dot(merged.astype(BF16), wo_ref[...])


def _mixer(x, conv_prev, pool_prev, start_pos, mk, mv, lw):
    bn, seq, d = x.shape
    conv_w, pool_w = conv_prev.shape[-1], pool_prev.shape[-1]
    n_mem, att_w = mk.shape[1], mk.shape[2]
    rows = min(MIXER_ROWS, seq)
    assert seq % rows == 0 and rows % SUBLANES == 0 and rows >= POOL_STATE
    in_cols = lw["w_in"].shape[1]
    per_b = lambda shape: pl.BlockSpec((1,) + shape, lambda b, s: (b, 0, 0))
    body = functools.partial(_mixer_body, rows=rows, start_pos=start_pos, conv_w=conv_w, pool_w=pool_w, att_w=att_w)
    return pl.pallas_call(
        body,
        grid=(bn, seq // rows),
        in_specs=[
            pl.BlockSpec((1, rows, d), lambda b, s: (b, s, 0)),
            per_b((CONV_K - 1, conv_w)), per_b((POOL_STATE, pool_w)), per_b((n_mem, att_w)), per_b((n_mem, att_w)),
            _resident((1, d)), _resident((d, in_cols)), _resident((CONV_K, conv_w)), _resident((1, conv_w)),
            _resident(lw["pool_w"].shape), _resident((1, pool_w)), _resident((conv_w, d)), _resident((pool_w, d)),
            _resident((att_w, d)), _resident((1, 3 * d)), _resident((d, d)),
        ],
        out_specs=[
            pl.BlockSpec((1, rows, d), lambda b, s: (b, s, 0)),
            per_b((CONV_K - 1, conv_w)), per_b((POOL_STATE, pool_w)),
        ],
        out_shape=[
            jax.ShapeDtypeStruct((bn, seq, d), F32),
            jax.ShapeDtypeStruct((bn, CONV_K - 1, conv_w), F32),
            jax.ShapeDtypeStruct((bn, POOL_STATE, pool_w), F32),
        ],
        scratch_shapes=[pltpu.VMEM((SUBLANES + rows, conv_w), F32), pltpu.VMEM((2 * SUBLANES + rows, pool_w), F32)],
        compiler_params=pltpu.CompilerParams(dimension_semantics=("arbitrary", "arbitrary"), vmem_limit_bytes=VMEM_LIMIT),
        name="mixer",
    )(x, conv_prev, pool_prev, mk, mv, lw["g_mix"], lw["w_in"], lw["conv_w"], lw["conv_b"], lw["pool_w"],
      lw["pool_scale"], lw["w_bc"], lw["w_bp"], lw["w_ba"], lw["gate_b"], lw["w_o"])


_STAIR = [(a, b) for a in range(PEER_TOPK) for b in range(PEER_TOPK) if (a + 1) * (b + 1) <= PEER_TOPK]
_STAIR_HEAD = 24
_STAIR_TAIL = [_STAIR[_STAIR_HEAD + 8 * i:_STAIR_HEAD + 8 * (i + 1)] for i in range(4)]


def _top16(v, payload=None):
    n, t = v.shape
    rows = lax.broadcasted_iota(jnp.int32, (n, t), 0)
    out_rows = lax.broadcasted_iota(jnp.int32, (PEER_TOPK, t), 0)
    vals = jnp.zeros((PEER_TOPK, t), F32)
    pay = jnp.zeros((PEER_TOPK, t), jnp.int32)
    for k in range(PEER_TOPK):
        m = jnp.max(v, axis=0, keepdims=True)
        am = jnp.min(jnp.where(v == m, rows, n), axis=0, keepdims=True)
        hit = rows == am
        p = am if payload is None else jnp.sum(jnp.where(hit, payload, 0), axis=0, keepdims=True)
        vals = jnp.where(out_rows == k, m, vals)
        pay = jnp.where(out_rows == k, p, pay)
        v = jnp.where(hit, -jnp.inf, v)
    return vals, pay


def _take_rows(src, idxs):
    t = src.shape[1]
    out_rows = lax.broadcasted_iota(jnp.int32, (SUBLANES, t), 0)
    acc = jnp.broadcast_to(src[idxs[0]:idxs[0] + 1, :], (SUBLANES, t))
    for i in range(1, SUBLANES):
        acc = jnp.where(out_rows == i, src[idxs[i]:idxs[i] + 1, :], acc)
    return acc


def _stair(first, second, combine):
    blocks = [combine(first[0:1, :], second), combine(first[1:2, :], second[0:SUBLANES, :])]
    for blk in _STAIR_TAIL:
        blk = blk + [(0, 0)] * (SUBLANES - len(blk))
        blocks.append(combine(_take_rows(first, [a for a, _ in blk]), _take_rows(second, [b for _, b in blk])))
    return jnp.concatenate(blocks, axis=0)


def _route_body(x2_ref, gffn_ref, wq_ref, keys_ref, xn_ref, ids_ref, gate_ref, *, n_keys):
    xn = _rms(x2_ref[...], gffn_ref[...])
    xn_ref[...] = xn
    q = _dot(xn.astype(BF16), wq_ref[...])
    half = keys_ref.shape[-1]
    t = q.shape[0]
    for h in range(PEER_HEADS):
        scores = [_dot_nt(keys_ref[p], q[:, (2 * h + p) * half:(2 * h + p + 1) * half].astype(BF16)) for p in range(2)]
        for l0 in range(0, t, LANES):
            (s1, i1), (s2, i2) = [_top16(sc[:, l0:l0 + LANES]) for sc in scores]
            cand = _stair(s1, s2, lambda a, b: a + b)
            pad = lax.broadcasted_iota(jnp.int32, cand.shape, 0) >= len(_STAIR)
            expert = _stair(i1, i2, lambda a, b: a * n_keys + b)
            sc, e = _top16(jnp.where(pad, -jnp.inf, cand), expert)
            ex = jnp.exp(sc - sc[0:1, :])
            ids_ref[h * PEER_TOPK:(h + 1) * PEER_TOPK, l0:l0 + LANES] = e
            gate_ref[h * PEER_TOPK:(h + 1) * PEER_TOPK, l0:l0 + LANES] = ex / jnp.sum(ex, axis=0, keepdims=True)


def _route(x2, g_ffn, wq, keys):
    t, d = x2.shape
    rows = ROUTE_ROWS
    assert t % rows == 0
    picks = PEER_HEADS * PEER_TOPK
    return pl.pallas_call(
        functools.partial(_route_body, n_keys=keys.shape[1]),
        grid=(t // rows,),
        in_specs=[pl.BlockSpec((rows, d), lambda i: (i, 0)), _resident((1, d)), _resident(wq.shape), _resident(keys.shape)],
        out_specs=[pl.BlockSpec((rows, d), lambda i: (i, 0)), pl.BlockSpec((picks, rows), lambda i: (0, i)),
                   pl.BlockSpec((picks, rows), lambda i: (0, i))],
        out_shape=[jax.ShapeDtypeStruct((t, d), F32), jax.ShapeDtypeStruct((picks, t), jnp.int32),
                   jax.ShapeDtypeStruct((picks, t), F32)],
        compiler_params=pltpu.CompilerParams(dimension_semantics=("arbitrary",), vmem_limit_bytes=VMEM_LIMIT),
        name="route",
    )(x2, g_ffn, wq, keys)


def _gelu(h):
    return 0.5 * h * (1.0 + lax.erf(h * (2.0 ** -0.5)))


def _experts_body(ids_hbm, gate_ref, xn_ref, x2_ref, gfin_ref, uv_hbm, after_hbm, y_ref, *scratch,
                  rows, picks, first_step, steps, final_norm):
    del after_hbm
    bufs, (sem, ids_smem, ids_sem) = scratch[:GATHER_SLOTS], scratch[GATHER_SLOTS:]
    i = pl.program_id(0)
    half = i % 2
    n = rows * picks
    d = xn_ref.shape[-1]
    lead = GATHER_SLOTS - 1
    groups = picks // SUBLANES

    def ids_copy(step, which):
        return pltpu.make_async_copy(ids_hbm.at[first_step + step], ids_smem.at[pl.ds(which * n, n)], ids_sem.at[which])

    def issue(off, slot, j0, j1):
        for j in range(j0, j1):
            pltpu.async_copy(uv_hbm.at[ids_smem[off + j]], bufs[slot].at[pl.ds(j, 1)], sem.at[slot], priority=j % 2)

    def wait_slot(slot):
        pltpu.make_async_copy(bufs[slot], bufs[slot], sem.at[slot]).wait()

    @pl.when(i == 0)
    def _():
        ids_copy(0, 0).start()
        ids_copy(0, 0).wait()
        if steps > 1:
            ids_copy(1, 1).start()
        for s in range(lead):
            issue(s * picks, s, 0, picks)

    lane = lax.broadcasted_iota(jnp.int32, (picks, rows), 1)

    def token(t, slot, ahead_off):
        ahead_slot = (slot + lead) % GATHER_SLOTS
        buf = bufs[slot]
        wait_slot(slot)
        if ahead_off is not None:
            issue(ahead_off, ahead_slot, 0, picks)
        x = xn_ref[pl.ds(t, 1), :]
        g = jnp.sum(jnp.where(lane == t, gate_ref[...], 0.0), axis=-1, keepdims=True)
        hs = []
        for k in range(groups):
            r0 = k * SUBLANES
            hs.append(jnp.sum(buf[r0:r0 + SUBLANES, 0:d] * x, axis=-1, keepdims=True))
        w = g * _gelu(jnp.concatenate(hs, axis=0))
        acc = None
        for k in range(groups):
            r0 = k * SUBLANES
            part = w[r0:r0 + SUBLANES, :] * buf[r0:r0 + SUBLANES, d:2 * d]
            acc = part if acc is None else acc + part
        y = x2_ref[pl.ds(t, 1), :] + jnp.sum(acc, axis=0, keepdims=True)
        y_ref[pl.ds(t, 1), :] = _rms(y, gfin_ref[...]) if final_norm else y

    def four_tokens(k, carry):
        for s in range(GATHER_SLOTS):
            t = k * GATHER_SLOTS + s
            token(t, s, half * n + (t + lead) * picks)
        return carry

    lax.fori_loop(0, rows // GATHER_SLOTS - 1, four_tokens, 0)

    t0 = rows - GATHER_SLOTS
    token(t0, 0, half * n + (t0 + lead) * picks)

    @pl.when(i + 1 < steps)
    def _():
        ids_copy(i + 1, 1 - half).wait()

    for s in range(1, GATHER_SLOTS):
        @pl.when(i + 1 < steps)
        def _():
            issue((1 - half) * n + (s - 1) * picks, (s + lead) % GATHER_SLOTS, 0, picks)

        token(t0 + s, s, None)

    @pl.when(i + 2 < steps)
    def _():
        ids_copy(i + 2, half).start()


def _experts(ids_tok, gate_t, xn, x2, g_final, uv, final_norm, first_step, steps, after):
    t, picks = ids_tok.shape
    d = xn.shape[1]
    rows = EXPERT_ROWS
    assert t % rows == 0 and rows >= 2 * GATHER_SLOTS and rows % GATHER_SLOTS == 0 and picks % (2 * SUBLANES) == 0
    ids = ids_tok.reshape(t // rows, rows * picks)
    body = functools.partial(_experts_body, rows=rows, picks=picks, first_step=first_step, steps=steps,
                             final_norm=final_norm)
    return pl.pallas_call(
        body,
        grid=(steps,),
        in_specs=[
            pl.BlockSpec(memory_space=pl.ANY),
            pl.BlockSpec((picks, rows), lambda i: (0, first_step + i)),
            pl.BlockSpec((rows, d), lambda i: (first_step + i, 0)),
            pl.BlockSpec((rows, d), lambda i: (first_step + i, 0)),
            _resident((1, d)),
            pl.BlockSpec(memory_space=pl.ANY),
            pl.BlockSpec(memory_space=pl.ANY),
        ],
        out_specs=pl.BlockSpec((rows, d), lambda i: (i, 0)),
        out_shape=jax.ShapeDtypeStruct((steps * rows, d), F32),
        scratch_shapes=[pltpu.VMEM((picks, 2 * d), F32)] * GATHER_SLOTS + [
            pltpu.SemaphoreType.DMA((GATHER_SLOTS,)),
            pltpu.SMEM((2 * rows * picks,), jnp.int32),
            pltpu.SemaphoreType.DMA((2,)),
        ],
        compiler_params=pltpu.CompilerParams(dimension_semantics=("arbitrary",), vmem_limit_bytes=VMEM_LIMIT),
        name="experts",
    )(ids, gate_t, xn, x2, g_final, uv, after)


def _sc_token_pipeline(per_w, nch, wid, ids_hbm, vec_hbm, tab_hbm, out_hbm, idx_v, vec_v, rows_v, out_v,
                       gsem, isem, vsem, osem, compute):
    base = wid * per_w

    def ids_copy(t, p):
        return pltpu.make_async_copy(ids_hbm.at[t], idx_v.at[p], isem.at[p])

    def vec_copy(t, p):
        return pltpu.make_async_copy(vec_hbm.at[t], vec_v.at[p], vsem.at[p])

    def out_copy(t, p):
        return pltpu.make_async_copy(out_v.at[p], out_hbm.at[t], osem.at[p])

    def gather(p, c, b):
        return pltpu.make_async_copy(tab_hbm.at[idx_v.at[p, c]], rows_v.at[b], gsem.at[b])

    ids_copy(base, 0).start()
    vec_copy(base, 0).start()
    ids_copy(base, 0).wait()
    vec_copy(base, 0).wait()
    gather(0, 0, 0).start()

    def pair(ii, carry):
        for p in range(2):
            i = 2 * ii + p
            t = base + i
            has_next = i + 1 < per_w

            @pl.when(has_next)
            def _():
                ids_copy(t + 1, 1 - p).start()
                vec_copy(t + 1, 1 - p).start()

            @pl.when(i >= 2)
            def _():
                out_copy(t - 2, p).wait()

            for c in range(nch):
                b = c % 2
                if c + 1 < nch:
                    gather(p, c + 1, 1 - b).start()
                else:
                    @pl.when(has_next)
                    def _():
                        ids_copy(t + 1, 1 - p).wait()
                        vec_copy(t + 1, 1 - p).wait()
                        gather(1 - p, 0, 1 - b).start()
                gather(p, c, b).wait()
                compute(p, c, b)
            out_copy(t, p).start()
        return carry

    lax.fori_loop(0, per_w // 2, pair, 0)
    out_copy(base + per_w - 2, 0).wait()
    out_copy(base + per_w - 1, 1).wait()


HI_MASK = -65536


def _pack_pairs(tab):
    n, d = tab.shape
    bits = lax.bitcast_convert_type(tab.astype(jnp.bfloat16), jnp.uint16).astype(jnp.uint32)
    return lax.bitcast_convert_type((bits[:, :d // 2] << 16) | bits[:, d // 2:], jnp.int32)


def _unpack(words):
    hi = lax.bitcast_convert_type(words & HI_MASK, F32)
    lo = lax.bitcast_convert_type(words << 16, F32)
    return hi, lo


def _sc_dots(ids, xn, u2, n_tok):
    d = xn.shape[1]
    dw = u2.shape[1]
    nch = ids.shape[1]
    picks = nch * SC_CHUNK
    per_w = n_tok // SC_WORKERS
    assert per_w % 2 == 0 and nch % 2 == 0 and 2 * dw == d
    mesh = plsc.VectorSubcoreMesh(core_axis_name="c", subcore_axis_name="s")

    @functools.partial(
        pl.kernel, mesh=mesh, out_type=jax.ShapeDtypeStruct((n_tok, picks), F32),
        scratch_types=[pltpu.VMEM((2, nch, SC_CHUNK), jnp.int32), pltpu.VMEM((2, d), F32),
                       pltpu.VMEM((2, SC_CHUNK, dw), jnp.int32), pltpu.VMEM((2, picks), F32)]
        + [pltpu.SemaphoreType.DMA((2,))] * 4,
        compiler_params=pltpu.CompilerParams(needs_layout_passes=False), name="sc_dots")
    def k(ids_hbm, xn_hbm, u_hbm, h_hbm, idx_v, x_v, rows_v, h_v, gsem, isem, vsem, osem):
        wid = lax.axis_index("s") * SC_CORES + lax.axis_index("c")
        lane = lax.iota(jnp.int32, SC_LANES)

        def compute(p, c, b):
            for g in range(SC_CHUNK // SC_LANES):
                def rows4(q, hv):
                    j0 = g * SC_LANES + q * 4

                    def span(cc, accs):
                        for uu in range(SC_UNROLL):
                            off = pl.multiple_of((cc * SC_UNROLL + uu) * SC_LANES, SC_LANES)
                            x_hi = x_v[p, pl.ds(off, SC_LANES)]
                            x_lo = x_v[p, pl.ds(dw + off, SC_LANES)]
                            nxt = []
                            for kk in range(4):
                                hi, lo = _unpack(rows_v[b, j0 + kk, pl.ds(off, SC_LANES)])
                                nxt.append((accs[2 * kk] + hi * x_hi, accs[2 * kk + 1] + lo * x_lo))
                            accs = tuple(a for pair in nxt for a in pair)
                        return accs

                    accs = lax.fori_loop(0, dw // (SC_LANES * SC_UNROLL), span,
                                         tuple(jnp.zeros((SC_LANES,), F32) for _ in range(8)))
                    for kk in range(4):
                        hv = jnp.where(lane == q * 4 + kk, jnp.sum(accs[2 * kk] + accs[2 * kk + 1]), hv)
                    return hv

                hv = lax.fori_loop(0, SC_LANES // 4, rows4, jnp.zeros((SC_LANES,), F32))
                h_v[p, pl.ds(c * SC_CHUNK + g * SC_LANES, SC_LANES)] = hv

        _sc_token_pipeline(per_w, nch, wid, ids_hbm, xn_hbm, u_hbm, h_hbm, idx_v, x_v, rows_v, h_v,
                           gsem, isem, vsem, osem, compute)

    return k(ids, xn, u2)


def _sc_mix(ids, w, v2, n_tok):
    dw = v2.shape[1]
    d = 2 * dw
    nch = ids.shape[1]
    picks = nch * SC_CHUNK
    per_w = n_tok // SC_WORKERS
    assert per_w % 2 == 0 and nch % 2 == 0
    cb = 8
    mesh = plsc.VectorSubcoreMesh(core_axis_name="c", subcore_axis_name="s")

    @functools.partial(
        pl.kernel, mesh=mesh, out_type=jax.ShapeDtypeStruct((n_tok, d), F32),
        scratch_types=[pltpu.VMEM((2, nch, SC_CHUNK), jnp.int32), pltpu.VMEM((2, picks), F32),
                       pltpu.VMEM((2, SC_CHUNK, dw), jnp.int32), pltpu.VMEM((2, d), F32)]
        + [pltpu.SemaphoreType.DMA((2,))] * 4,
        compiler_params=pltpu.CompilerParams(needs_layout_passes=False), name="sc_mix")
    def k(ids_hbm, w_hbm, v_hbm, o_hbm, idx_v, w_v, rows_v, o_v, gsem, isem, vsem, osem):
        wid = lax.axis_index("s") * SC_CORES + lax.axis_index("c")

        def compute(p, c, b):
            for blk in range(dw // (cb * SC_LANES)):
                base = blk * cb * SC_LANES
                offs = [base + kk * SC_LANES for kk in range(cb)] + [dw + base + kk * SC_LANES for kk in range(cb)]
                if c == 0:
                    init = tuple(jnp.zeros((SC_LANES,), F32) for _ in offs)
                else:
                    init = tuple(o_v[p, pl.ds(o, SC_LANES)] for o in offs)

                def row(r, accs):
                    wj = plsc.load_gather(w_v.at[p], [jnp.full((SC_LANES,), c * SC_CHUNK, jnp.int32) + r])
                    his, los = [], []
                    for kk in range(cb):
                        hi, lo = _unpack(rows_v[b, r, pl.ds(base + kk * SC_LANES, SC_LANES)])
                        his.append(accs[kk] + wj * hi)
                        los.append(accs[cb + kk] + wj * lo)
                    return tuple(his + los)

                accs = lax.fori_loop(0, SC_CHUNK, row, init)
                for o, a in zip(offs, accs):
                    o_v[p, pl.ds(o, SC_LANES)] = a

        _sc_token_pipeline(per_w, nch, wid, ids_hbm, w_hbm, v_hbm, o_hbm, idx_v, w_v, rows_v, o_v,
                           gsem, isem, vsem, osem, compute)

    return k(ids, w, v2)


def _gate_gelu_body(h_ref, gate_ref, after_hbm, w_ref):
    del after_hbm
    w_ref[...] = gate_ref[...] * _gelu(h_ref[...])


def _gate_gelu(h, gate_tok, after):
    n, picks = h.shape
    rows = math.gcd(n, 1024)
    assert n % rows == 0
    return pl.pallas_call(
        _gate_gelu_body, grid=(n // rows,),
        in_specs=[pl.BlockSpec((rows, picks), lambda i: (i, 0)), pl.BlockSpec((rows, picks), lambda i: (i, 0)),
                  pl.BlockSpec(memory_space=pl.ANY)],
        out_specs=pl.BlockSpec((rows, picks), lambda i: (i, 0)),
        out_shape=jax.ShapeDtypeStruct((n, picks), F32), name="gate_gelu",
    )(h, gate_tok, after)


def _finish_body(x2_ref, o_ref, gfin_ref, y_ref, *, final_norm):
    y = x2_ref[...] + o_ref[...]
    y_ref[...] = _rms(y, gfin_ref[...]) if final_norm else y


def _finish(x2, o, g_final, final_norm):
    n, d = o.shape
    rows = math.gcd(n, 512)
    assert n % rows == 0
    return pl.pallas_call(
        functools.partial(_finish_body, final_norm=final_norm), grid=(n // rows,),
        in_specs=[pl.BlockSpec((rows, d), lambda i: (i, 0)), pl.BlockSpec((rows, d), lambda i: (i, 0)), _resident((1, d))],
        out_specs=pl.BlockSpec((rows, d), lambda i: (i, 0)),
        out_shape=jax.ShapeDtypeStruct((n, d), F32), name="finish",
    )(x2, o, g_final)


def _peer(ids_t, gate_t, xn, x2, g_final, lw, final_norm):
    picks, t = ids_t.shape
    ids_tok = ids_t.T
    total_steps = t // EXPERT_ROWS
    sc_steps = int(total_steps * SC_SHARE) if t >= SC_MIN_TOKENS else 0
    n_sc = sc_steps * EXPERT_ROWS
    if n_sc == 0:
        return _experts(ids_tok, gate_t, xn, x2, g_final, lw["peer_uv"], final_norm, 0, total_steps, g_final)
    assert n_sc % SC_WORKERS == 0 and picks % SC_CHUNK == 0
    tc_steps = total_steps - sc_steps
    first = int(tc_steps * TC_FIRST_SHARE)
    ids_sc = ids_tok.reshape(t, picks // SC_CHUNK, SC_CHUNK)
    h = _sc_dots(ids_sc, xn, lw["peer_u2"], n_sc)
    y_tc1 = _experts(ids_tok, gate_t, xn, x2, g_final, lw["peer_uv"], final_norm, sc_steps, first, g_final)
    w = _gate_gelu(h, gate_t.T, y_tc1)
    o = _sc_mix(ids_sc, w, lw["peer_v2"], n_sc)
    y_tc2 = _experts(ids_tok, gate_t, xn, x2, g_final, lw["peer_uv"], final_norm, sc_steps + first, tc_steps - first, w)
    y_sc = _finish(x2, o, g_final, final_norm)
    return jnp.concatenate([y_sc, y_tc1, y_tc2], axis=0)


def _layer(x, conv_prev, pool_prev, start_pos, mk, mv, lw, g_final, final_norm):
    bn, seq, d = x.shape
    x2, new_conv, new_pool = _mixer(x, conv_prev, pool_prev, start_pos, mk, mv, lw)
    x2 = x2.reshape(bn * seq, d)
    xn, ids_t, gate_t = _route(x2, lw["g_ffn"], lw["peer_wq"], lw["peer_keys"])
    y = _peer(ids_t, gate_t, xn, x2, g_final, lw, final_norm)
    return y.reshape(bn, seq, d), new_conv, new_pool


def kernel(x_prompt, x_sample, mem_prompt, cache_conv, cache_pool, cache_mem_k, cache_mem_v, g_mix, w_in, conv_w, conv_b, pool_w, pool_scale, g_mem, w_mk, w_mv, w_bc, w_bp, w_ba, gate_b, w_o, g_ffn, peer_wq, peer_keys, peer_u, peer_v, g_final):
    depth, d = g_mix.shape
    xp, xs = x_prompt, x_sample
    bp, bs = xp.shape[0], xs.shape[0]
    n_exp = peer_u.shape[1]
    gfin = g_final.reshape(1, d)
    conv_p, pool_p, mk_p, mv_p, conv_s, pool_s = [], [], [], [], [], []
    for l in range(depth):
        lw = dict(
            g_mix=g_mix[l].reshape(1, d), w_in=w_in[l].astype(BF16), conv_w=conv_w[l], conv_b=conv_b[l].reshape(1, -1),
            pool_w=pool_w[l].astype(BF16), pool_scale=pool_scale[l].reshape(1, -1), w_bc=w_bc[l].astype(BF16),
            w_bp=w_bp[l].astype(BF16), w_ba=w_ba[l].astype(BF16), gate_b=gate_b[l].reshape(1, -1), w_o=w_o[l].astype(BF16),
            g_ffn=g_ffn[l].reshape(1, d), peer_wq=peer_wq[l].astype(BF16), peer_keys=peer_keys[l].astype(BF16),
            peer_uv=jnp.concatenate([peer_u[l], peer_v[l]], axis=1).reshape(n_exp, 1, 2 * d),
            peer_u2=_pack_pairs(peer_u[l]), peer_v2=_pack_pairs(peer_v[l]),
        )
        last = l == depth - 1
        mk, mv = _mem_kv(mem_prompt, g_mem[l], w_mk[l], w_mv[l])
        zc = jnp.zeros((bp, CONV_K - 1, conv_w.shape[-1]), xp.dtype)
        zp = jnp.zeros((bp, POOL_STATE, pool_scale.shape[-1]), xp.dtype)
        xp, cp, pp = _layer(xp, zc, zp, 0, mk, mv, lw, gfin, last)
        n_mem = cache_mem_k.shape[2]
        xs, cs, ps = _layer(xs, cache_conv[l], cache_pool[l], PAST_LEN, cache_mem_k[l].reshape(bs, n_mem, -1),
                            cache_mem_v[l].reshape(bs, n_mem, -1), lw, gfin, last)
        heads_shape = (bp, n_mem) + cache_mem_k.shape[3:]
        conv_p.append(cp); pool_p.append(pp); mk_p.append(mk.reshape(heads_shape)); mv_p.append(mv.reshape(heads_shape))
        conv_s.append(cs); pool_s.append(ps)
    return (xp, xs, jnp.stack(conv_p), jnp.stack(pool_p), jnp.stack(mk_p), jnp.stack(mv_p),
            jnp.stack(conv_s), jnp.stack(pool_s))
```

```python
import functools
import math

import jax
import jax.numpy as jnp
from jax import lax
from jax.experimental import pallas as pl
from jax.experimental.pallas import tpu as pltpu
from jax.experimental.pallas import tpu_sc as plsc

F32 = jnp.float32
BF16 = jnp.bfloat16
EPS = 1e-6

CONV_K = 3
POOL_WINDOWS = (2, 4, 8, 16)
POOL_STATE = max(POOL_WINDOWS) - 1
PAST_LEN = 1024
MEM_HEADS = 4
PEER_HEADS = 8
PEER_TOPK = 16

SUBLANES = 8
LANES = 128
MIXER_ROWS = 256
ROUTE_ROWS = 256
EXPERT_ROWS = 128
GATHER_SLOTS = 4
VMEM_LIMIT = 56 * 1024 * 1024

SC_CORES = 2
SC_SUBCORES = 16
SC_WORKERS = SC_CORES * SC_SUBCORES
SC_LANES = 16
SC_CHUNK = 32
SC_UNROLL = 8
SC_SHARE = 0.66
SC_MIN_TOKENS = 8192
TC_FIRST_SHARE = 0.57


def _rms(x, g):
    return x * lax.rsqrt(jnp.mean(x * x, axis=-1, keepdims=True) + EPS) * g


def _dot(a, b):
    return jnp.dot(a, b, preferred_element_type=F32)


def _dot_nt(a, b):
    return lax.dot_general(a, b, (((1,), (1,)), ((), ())), preferred_element_type=F32)


def _resident(shape):
    zeros = (0,) * len(shape)
    return pl.BlockSpec(shape, lambda *_: zeros, pipeline_mode=pl.Buffered(1))


def _memkv_body(mem_ref, g_ref, wk_ref, wv_ref, k_ref, v_ref):
    m = _rms(mem_ref[...], g_ref[...]).astype(BF16)
    k_ref[...] = _dot(m, wk_ref[...])
    v_ref[...] = _dot(m, wv_ref[...])


def _mem_kv(mem, g_mem, w_mk, w_mv):
    bn, n_mem, d = mem.shape
    att_w = w_mk.shape[1]
    k, v = pl.pallas_call(
        _memkv_body,
        out_shape=[jax.ShapeDtypeStruct((bn * n_mem, att_w), F32)] * 2,
        name="mem_kv",
    )(mem.reshape(bn * n_mem, d), g_mem.reshape(1, d), w_mk.astype(BF16), w_mv.astype(BF16))
    return k.reshape(bn, n_mem, att_w), v.reshape(bn, n_mem, att_w)


def _mixer_body(x_ref, cprev_ref, pprev_ref, mk_ref, mv_ref, gmix_ref, win_ref, convw_ref, convb_ref, poolw_ref,
                pscale_ref, wbc_ref, wbp_ref, wba_ref, gateb_ref, wo_ref,
                x2_ref, nconv_ref, npool_ref, zbuf, ubuf, *, rows, start_pos, conv_w, pool_w, att_w):
    s = pl.program_id(1)
    d_model = x_ref.shape[-1]
    z0 = SUBLANES
    u0 = 2 * SUBLANES

    @pl.when(s == 0)
    def _():
        zbuf[z0 - (CONV_K - 1):z0, :] = cprev_ref[0]
        ubuf[u0 - POOL_STATE:u0, :] = pprev_ref[0]

    x = x_ref[0]
    hb = _rms(x, gmix_ref[...]).astype(BF16)

    c0 = 0
    pa = _dot(hb, win_ref[:, c0:c0 + 3 * conv_w])
    xc, bg, cg = pa[:, :conv_w], pa[:, conv_w:2 * conv_w], pa[:, 2 * conv_w:]
    zbuf[z0:z0 + rows, :] = cg * xc
    cw = convw_ref[...]
    conv = zbuf[z0 - 2:z0 - 2 + rows, :] * cw[0:1]
    for k in range(1, CONV_K):
        conv = conv + zbuf[z0 - 2 + k:z0 - 2 + k + rows, :] * cw[k:k + 1]
    conv = conv + convb_ref[...]
    ya = _dot((bg * conv).astype(BF16), wbc_ref[...])
    last_z = zbuf[z0 + rows - (CONV_K - 1):z0 + rows, :]
    nconv_ref[0] = last_z
    zbuf[z0 - (CONV_K - 1):z0, :] = last_z
    c0 += 3 * conv_w

    up = _dot(hb, win_ref[:, c0:c0 + pool_w])
    ubuf[u0:u0 + rows, :] = up
    pos = start_pos + s * rows + lax.broadcasted_iota(jnp.int32, (rows, 1), 0)
    gw = pool_w // len(POOL_WINDOWS)
    ys = []
    for g, w in enumerate(POOL_WINDOWS):
        cur = up[:, g * gw:(g + 1) * gw]
        acc = cur
        for k in range(1, w):
            acc = acc + ubuf[u0 - k:u0 - k + rows, g * gw:(g + 1) * gw]
        cnt = jnp.minimum(pos + 1, w).astype(F32)
        ys.append(_dot((acc / cnt - cur).astype(BF16), poolw_ref[g]))
    yb = _dot((jnp.concatenate(ys, axis=-1) * pscale_ref[...]).astype(BF16), wbp_ref[...])
    last_u = ubuf[u0 + rows - POOL_STATE:u0 + rows, :]
    npool_ref[0] = last_u
    ubuf[u0 - POOL_STATE:u0, :] = last_u
    c0 += pool_w

    q = _dot(hb, win_ref[:, c0:c0 + att_w])
    kb = mk_ref[0].astype(BF16)
    vb = mv_ref[0].astype(BF16)
    hd = att_w // MEM_HEADS
    heads = []
    for h in range(MEM_HEADS):
        sc = _dot_nt(q[:, h * hd:(h + 1) * hd].astype(BF16), kb[:, h * hd:(h + 1) * hd]) * (hd ** -0.5)
        e = jnp.exp(sc - jnp.max(sc, axis=-1, keepdims=True))
        p = e / jnp.sum(e, axis=-1, keepdims=True)
        heads.append(_dot(p.astype(BF16), vb[:, h * hd:(h + 1) * hd]))
    yc = _dot(jnp.concatenate(heads, axis=-1).astype(BF16), wba_ref[...])
    c0 += att_w

    merged = None
    for i, y in enumerate((ya, yb, yc)):
        gl = _dot(hb, win_ref[:, c0 + i * d_model:c0 + (i + 1) * d_model]) + gateb_ref[:, i * d_model:(i + 1) * d_model]
        term = (1.0 / (1.0 + jnp.exp(-gl))) * y
        merged = term if merged is None else merged + term
    x2_ref[0] = x + _dot(merged.astype(BF16), wo_ref[...])


def _mixer(x, conv_prev, pool_prev, start_pos, mk, mv, lw):
    bn, seq, d = x.shape
    conv_w, pool_w = conv_prev.shape[-1], pool_prev.shape[-1]
    n_mem, att_w = mk.shape[1], mk.shape[2]
    rows = min(MIXER_ROWS, seq)
    assert seq % rows == 0 and rows % SUBLANES == 0 and rows >= POOL_STATE
    in_cols = lw["w_in"].shape[1]
    per_b = lambda shape: pl.BlockSpec((1,) + shape, lambda b, s: (b, 0, 0))
    body = functools.partial(_mixer_body, rows=rows, start_pos=start_pos, conv_w=conv_w, pool_w=pool_w, att_w=att_w)
    return pl.pallas_call(
        body,
        grid=(bn, seq // rows),
        in_specs=[
            pl.BlockSpec((1, rows, d), lambda b, s: (b, s, 0)),
            per_b((CONV_K - 1, conv_w)), per_b((POOL_STATE, pool_w)), per_b((n_mem, att_w)), per_b((n_mem, att_w)),
            _resident((1, d)), _resident((d, in_cols)), _resident((CONV_K, conv_w)), _resident((1, conv_w)),
            _resident(lw["pool_w"].shape), _resident((1, pool_w)), _resident((conv_w, d)), _resident((pool_w, d)),
            _resident((att_w, d)), _resident((1, 3 * d)), _resident((d, d)),
        ],
        out_specs=[
            pl.BlockSpec((1, rows, d), lambda b, s: (b, s, 0)),
            per_b((CONV_K - 1, conv_w)), per_b((POOL_STATE, pool_w)),
        ],
        out_shape=[
            jax.ShapeDtypeStruct((bn, seq, d), F32),
            jax.ShapeDtypeStruct((bn, CONV_K - 1, conv_w), F32),
            jax.ShapeDtypeStruct((bn, POOL_STATE, pool_w), F32),
        ],
        scratch_shapes=[pltpu.VMEM((SUBLANES + rows, conv_w), F32), pltpu.VMEM((2 * SUBLANES + rows, pool_w), F32)],
        compiler_params=pltpu.CompilerParams(dimension_semantics=("arbitrary", "arbitrary"), vmem_limit_bytes=VMEM_LIMIT),
        name="mixer",
    )(x, conv_prev, pool_prev, mk, mv, lw["g_mix"], lw["w_in"], lw["conv_w"], lw["conv_b"], lw["pool_w"],
      lw["pool_scale"], lw["w_bc"], lw["w_bp"], lw["w_ba"], lw["gate_b"], lw["w_o"])


_STAIR = [(a, b) for a in range(PEER_TOPK) for b in range(PEER_TOPK) if (a + 1) * (b + 1) <= PEER_TOPK]
_STAIR_HEAD = 24
_STAIR_TAIL = [_STAIR[_STAIR_HEAD + 8 * i:_STAIR_HEAD + 8 * (i + 1)] for i in range(4)]


def _top16(v, payload=None):
    n, t = v.shape
    rows = lax.broadcasted_iota(jnp.int32, (n, t), 0)
    out_rows = lax.broadcasted_iota(jnp.int32, (PEER_TOPK, t), 0)
    vals = jnp.zeros((PEER_TOPK, t), F32)
    pay = jnp.zeros((PEER_TOPK, t), jnp.int32)
    for k in range(PEER_TOPK):
        m = jnp.max(v, axis=0, keepdims=True)
        am = jnp.min(jnp.where(v == m, rows, n), axis=0, keepdims=True)
        hit = rows == am
        p = am if payload is None else jnp.sum(jnp.where(hit, payload, 0), axis=0, keepdims=True)
        vals = jnp.where(out_rows == k, m, vals)
        pay = jnp.where(out_rows == k, p, pay)
        v = jnp.where(hit, -jnp.inf, v)
    return vals, pay


def _take_rows(src, idxs):
    t = src.shape[1]
    out_rows = lax.broadcasted_iota(jnp.int32, (SUBLANES, t), 0)
    acc = jnp.broadcast_to(src[idxs[0]:idxs[0] + 1, :], (SUBLANES, t))
    for i in range(1, SUBLANES):
        acc = jnp.where(out_rows == i, src[idxs[i]:idxs[i] + 1, :], acc)
    return acc


def _stair(first, second, combine):
    blocks = [combine(first[0:1, :], second), combine(first[1:2, :], second[0:SUBLANES, :])]
    for blk in _STAIR_TAIL:
        blk = blk + [(0, 0)] * (SUBLANES - len(blk))
        blocks.append(combine(_take_rows(first, [a for a, _ in blk]), _take_rows(second, [b for _, b in blk])))
    return jnp.concatenate(blocks, axis=0)


def _route_body(x2_ref, gffn_ref, wq_ref, keys_ref, xn_ref, ids_ref, gate_ref, *, n_keys):
    xn = _rms(x2_ref[...], gffn_ref[...])
    xn_ref[...] = xn
    q = _dot(xn.astype(BF16), wq_ref[...])
    half = keys_ref.shape[-1]
    t = q.shape[0]
    for h in range(PEER_HEADS):
        scores = [_dot_nt(keys_ref[p], q[:, (2 * h + p) * half:(2 * h + p + 1) * half].astype(BF16)) for p in range(2)]
        for l0 in range(0, t, LANES):
            (s1, i1), (s2, i2) = [_top16(sc[:, l0:l0 + LANES]) for sc in scores]
            cand = _stair(s1, s2, lambda a, b: a + b)
            pad = lax.broadcasted_iota(jnp.int32, cand.shape, 0) >= len(_STAIR)
            expert = _stair(i1, i2, lambda a, b: a * n_keys + b)
            sc, e = _top16(jnp.where(pad, -jnp.inf, cand), expert)
            ex = jnp.exp(sc - sc[0:1, :])
            ids_ref[h * PEER_TOPK:(h + 1) * PEER_TOPK, l0:l0 + LANES] = e
            gate_ref[h * PEER_TOPK:(h + 1) * PEER_TOPK, l0:l0 + LANES] = ex / jnp.sum(ex, axis=0, keepdims=True)


def _route(x2, g_ffn, wq, keys):
    t, d = x2.shape
    rows = ROUTE_ROWS
    assert t % rows == 0
    picks = PEER_HEADS * PEER_TOPK
    return pl.pallas_call(
        functools.partial(_route_body, n_keys=keys.shape[1]),
        grid=(t // rows,),
        in_specs=[pl.BlockSpec((rows, d), lambda i: (i, 0)), _resident((1, d)), _resident(wq.shape), _resident(keys.shape)],
        out_specs=[pl.BlockSpec((rows, d), lambda i: (i, 0)), pl.BlockSpec((picks, rows), lambda i: (0, i)),
                   pl.BlockSpec((picks, rows), lambda i: (0, i))],
        out_shape=[jax.ShapeDtypeStruct((t, d), F32), jax.ShapeDtypeStruct((picks, t), jnp.int32),
                   jax.ShapeDtypeStruct((picks, t), F32)],
        compiler_params=pltpu.CompilerParams(dimension_semantics=("arbitrary",), vmem_limit_bytes=VMEM_LIMIT),
        name="route",
    )(x2, g_ffn, wq, keys)


def _gelu(h):
    return 0.5 * h * (1.0 + lax.erf(h * (2.0 ** -0.5)))


def _experts_body(ids_hbm, gate_ref, xn_ref, x2_ref, gfin_ref, uv_hbm, after_hbm, y_ref, *scratch,
                  rows, picks, first_step, steps, final_norm):
    del after_hbm
    bufs, (sem, ids_smem, ids_sem) = scratch[:GATHER_SLOTS], scratch[GATHER_SLOTS:]
    i = pl.program_id(0)
    half = i % 2
    n = rows * picks
    d = xn_ref.shape[-1]
    lead = GATHER_SLOTS - 1
    groups = picks // SUBLANES

    def ids_copy(step, which):
        return pltpu.make_async_copy(ids_hbm.at[first_step + step], ids_smem.at[pl.ds(which * n, n)], ids_sem.at[which])

    def issue(off, slot, j0, j1):
        for j in range(j0, j1):
            pltpu.async_copy(uv_hbm.at[ids_smem[off + j]], bufs[slot].at[pl.ds(j, 1)], sem.at[slot], priority=j % 2)

    def wait_slot(slot):
        pltpu.make_async_copy(bufs[slot], bufs[slot], sem.at[slot]).wait()

    @pl.when(i == 0)
    def _():
        ids_copy(0, 0).start()
        ids_copy(0, 0).wait()
        if steps > 1:
            ids_copy(1, 1).start()
        for s in range(lead):
            issue(s * picks, s, 0, picks)

    lane = lax.broadcasted_iota(jnp.int32, (picks, rows), 1)

    def token(t, slot, ahead_off):
        ahead_slot = (slot + lead) % GATHER_SLOTS
        buf = bufs[slot]
        wait_slot(slot)
        if ahead_off is not None:
            issue(ahead_off, ahead_slot, 0, picks)
        x = xn_ref[pl.ds(t, 1), :]
        g = jnp.sum(jnp.where(lane == t, gate_ref[...], 0.0), axis=-1, keepdims=True)
        hs = []
        for k in range(groups):
            r0 = k * SUBLANES
            hs.append(jnp.sum(buf[r0:r0 + SUBLANES, 0:d] * x, axis=-1, keepdims=True))
        w = g * _gelu(jnp.concatenate(hs, axis=0))
        acc = None
        for k in range(groups):
            r0 = k * SUBLANES
            part = w[r0:r0 + SUBLANES, :] * buf[r0:r0 + SUBLANES, d:2 * d]
            acc = part if acc is None else acc + part
        y = x2_ref[pl.ds(t, 1), :] + jnp.sum(acc, axis=0, keepdims=True)
        y_ref[pl.ds(t, 1), :] = _rms(y, gfin_ref[...]) if final_norm else y

    def four_tokens(k, carry):
        for s in range(GATHER_SLOTS):
            t = k * GATHER_SLOTS + s
            token(t, s, half * n + (t + lead) * picks)
        return carry

    lax.fori_loop(0, rows // GATHER_SLOTS - 1, four_tokens, 0)

    t0 = rows - GATHER_SLOTS
    token(t0, 0, half * n + (t0 + lead) * picks)

    @pl.when(i + 1 < steps)
    def _():
        ids_copy(i + 1, 1 - half).wait()

    for s in range(1, GATHER_SLOTS):
        @pl.when(i + 1 < steps)
        def _():
            issue((1 - half) * n + (s - 1) * picks, (s + lead) % GATHER_SLOTS, 0, picks)

        token(t0 + s, s, None)

    @pl.when(i + 2 < steps)
    def _():
        ids_copy(i + 2, half).start()


def _experts(ids_tok, gate_t, xn, x2, g_final, uv, final_norm, first_step, steps, after):
    t, picks = ids_tok.shape
    d = xn.shape[1]
    rows = EXPERT_ROWS
    assert t % rows == 0 and rows >= 2 * GATHER_SLOTS and rows % GATHER_SLOTS == 0 and picks % (2 * SUBLANES) == 0
    ids = ids_tok.reshape(t // rows, rows * picks)
    body = functools.partial(_experts_body, rows=rows, picks=picks, first_step=first_step, steps=steps,
                             final_norm=final_norm)
    return pl.pallas_call(
        body,
        grid=(steps,),
        in_specs=[
            pl.BlockSpec(memory_space=pl.ANY),
            pl.BlockSpec((picks, rows), lambda i: (0, first_step + i)),
            pl.BlockSpec((rows, d), lambda i: (first_step + i, 0)),
            pl.BlockSpec((rows, d), lambda i: (first_step + i, 0)),
            _resident((1, d)),
            pl.BlockSpec(memory_space=pl.ANY),
            pl.BlockSpec(memory_space=pl.ANY),
        ],
        out_specs=pl.BlockSpec((rows, d), lambda i: (i, 0)),
        out_shape=jax.ShapeDtypeStruct((steps * rows, d), F32),
        scratch_shapes=[pltpu.VMEM((picks, 2 * d), F32)] * GATHER_SLOTS + [
            pltpu.SemaphoreType.DMA((GATHER_SLOTS,)),
            pltpu.SMEM((2 * rows * picks,), jnp.int32),
            pltpu.SemaphoreType.DMA((2,)),
        ],
        compiler_params=pltpu.CompilerParams(dimension_semantics=("arbitrary",), vmem_limit_bytes=VMEM_LIMIT),
        name="experts",
    )(ids, gate_t, xn, x2, g_final, uv, after)


def _sc_token_pipeline(per_w, nch, wid, ids_hbm, vec_hbm, tab_hbm, out_hbm, idx_v, vec_v, rows_v, out_v,
                       gsem, isem, vsem, osem, compute):
    base = wid * per_w

    def ids_copy(t, p):
        return pltpu.make_async_copy(ids_hbm.at[t], idx_v.at[p], isem.at[p])

    def vec_copy(t, p):
        return pltpu.make_async_copy(vec_hbm.at[t], vec_v.at[p], vsem.at[p])

    def out_copy(t, p):
        return pltpu.make_async_copy(out_v.at[p], out_hbm.at[t], osem.at[p])

    def gather(p, c, b):
        return pltpu.make_async_copy(tab_hbm.at[idx_v.at[p, c]], rows_v.at[b], gsem.at[b])

    ids_copy(base, 0).start()
    vec_copy(base, 0).start()
    ids_copy(base, 0).wait()
    vec_copy(base, 0).wait()
    gather(0, 0, 0).start()

    def pair(ii, carry):
        for p in range(2):
            i = 2 * ii + p
            t = base + i
            has_next = i + 1 < per_w

            @pl.when(has_next)
            def _():
                ids_copy(t + 1, 1 - p).start()
                vec_copy(t + 1, 1 - p).start()

            @pl.when(i >= 2)
            def _():
                out_copy(t - 2, p).wait()

            for c in range(nch):
                b = c % 2
                if c + 1 < nch:
                    gather(p, c + 1, 1 - b).start()
                else:
                    @pl.when(has_next)
                    def _():
                        ids_copy(t + 1, 1 - p).wait()
                        vec_copy(t + 1, 1 - p).wait()
                        gather(1 - p, 0, 1 - b).start()
                gather(p, c, b).wait()
                compute(p, c, b)
            out_copy(t, p).start()
        return carry

    lax.fori_loop(0, per_w // 2, pair, 0)
    out_copy(base + per_w - 2, 0).wait()
    out_copy(base + per_w - 1, 1).wait()


HI_MASK = -65536


def _pack_pairs(tab):
    n, d = tab.shape
    bits = lax.bitcast_convert_type(tab.astype(jnp.bfloat16), jnp.uint16).astype(jnp.uint32)
    return lax.bitcast_convert_type((bits[:, :d // 2] << 16) | bits[:, d // 2:], jnp.int32)


def _unpack(words):
    hi = lax.bitcast_convert_type(words & HI_MASK, F32)
    lo = lax.bitcast_convert_type(words << 16, F32)
    return hi, lo


def _sc_dots(ids, xn, u2, n_tok):
    d = xn.shape[1]
    dw = u2.shape[1]
    nch = ids.shape[1]
    picks = nch * SC_CHUNK
    per_w = n_tok // SC_WORKERS
    assert per_w % 2 == 0 and nch % 2 == 0 and 2 * dw == d
    mesh = plsc.VectorSubcoreMesh(core_axis_name="c", subcore_axis_name="s")

    @functools.partial(
        pl.kernel, mesh=mesh, out_type=jax.ShapeDtypeStruct((n_tok, picks), F32),
        scratch_types=[pltpu.VMEM((2, nch, SC_CHUNK), jnp.int32), pltpu.VMEM((2, d), F32),
                       pltpu.VMEM((2, SC_CHUNK, dw), jnp.int32), pltpu.VMEM((2, picks), F32)]
        + [pltpu.SemaphoreType.DMA((2,))] * 4,
        compiler_params=pltpu.CompilerParams(needs_layout_passes=False), name="sc_dots")
    def k(ids_hbm, xn_hbm, u_hbm, h_hbm, idx_v, x_v, rows_v, h_v, gsem, isem, vsem, osem):
        wid = lax.axis_index("s") * SC_CORES + lax.axis_index("c")
        lane = lax.iota(jnp.int32, SC_LANES)

        def compute(p, c, b):
            for g in range(SC_CHUNK // SC_LANES):
                def rows4(q, hv):
                    j0 = g * SC_LANES + q * 4

                    def span(cc, accs):
                        for uu in range(SC_UNROLL):
                            off = pl.multiple_of((cc * SC_UNROLL + uu) * SC_LANES, SC_LANES)
                            x_hi = x_v[p, pl.ds(off, SC_LANES)]
                            x_lo = x_v[p, pl.ds(dw + off, SC_LANES)]
                            nxt = []
                            for kk in range(4):
                                hi, lo = _unpack(rows_v[b, j0 + kk, pl.ds(off, SC_LANES)])
                                nxt.append((accs[2 * kk] + hi * x_hi, accs[2 * kk + 1] + lo * x_lo))
                            accs = tuple(a for pair in nxt for a in pair)
                        return accs

                    accs = lax.fori_loop(0, dw // (SC_LANES * SC_UNROLL), span,
                                         tuple(jnp.zeros((SC_LANES,), F32) for _ in range(8)))
                    for kk in range(4):
                        hv = jnp.where(lane == q * 4 + kk, jnp.sum(accs[2 * kk] + accs[2 * kk + 1]), hv)
                    return hv

                hv = lax.fori_loop(0, SC_LANES // 4, rows4, jnp.zeros((SC_LANES,), F32))
                h_v[p, pl.ds(c * SC_CHUNK + g * SC_LANES, SC_LANES)] = hv

        _sc_token_pipeline(per_w, nch, wid, ids_hbm, xn_hbm, u_hbm, h_hbm, idx_v, x_v, rows_v, h_v,
                           gsem, isem, vsem, osem, compute)

    return k(ids, xn, u2)


def _sc_mix(ids, w, v2, n_tok):
    dw = v2.shape[1]
    d = 2 * dw
    nch = ids.shape[1]
    picks = nch * SC_CHUNK
    per_w = n_tok // SC_WORKERS
    assert per_w % 2 == 0 and nch % 2 == 0
    cb = 8
    mesh = plsc.VectorSubcoreMesh(core_axis_name="c", subcore_axis_name="s")

    @functools.partial(
        pl.kernel, mesh=mesh, out_type=jax.ShapeDtypeStruct((n_tok, d), F32),
        scratch_types=[pltpu.VMEM((2, nch, SC_CHUNK), jnp.int32), pltpu.VMEM((2, picks), F32),
                       pltpu.VMEM((2, SC_CHUNK, dw), jnp.int32), pltpu.VMEM((2, d), F32)]
        + [pltpu.SemaphoreType.DMA((2,))] * 4,
        compiler_params=pltpu.CompilerParams(needs_layout_passes=False), name="sc_mix")
    def k(ids_hbm, w_hbm, v_hbm, o_hbm, idx_v, w_v, rows_v, o_v, gsem, isem, vsem, osem):
        wid = lax.axis_index("s") * SC_CORES + lax.axis_index("c")

        def compute(p, c, b):
            for blk in range(dw // (cb * SC_LANES)):
                base = blk * cb * SC_LANES
                offs = [base + kk * SC_LANES for kk in range(cb)] + [dw + base + kk * SC_LANES for kk in range(cb)]
                if c == 0:
                    init = tuple(jnp.zeros((SC_LANES,), F32) for _ in offs)
                else:
                    init = tuple(o_v[p, pl.ds(o, SC_LANES)] for o in offs)

                def row(r, accs):
                    wj = plsc.load_gather(w_v.at[p], [jnp.full((SC_LANES,), c * SC_CHUNK, jnp.int32) + r])
                    his, los = [], []
                    for kk in range(cb):
                        hi, lo = _unpack(rows_v[b, r, pl.ds(base + kk * SC_LANES, SC_LANES)])
                        his.append(accs[kk] + wj * hi)
                        los.append(accs[cb + kk] + wj * lo)
                    return tuple(his + los)

                accs = lax.fori_loop(0, SC_CHUNK, row, init)
                for o, a in zip(offs, accs):
                    o_v[p, pl.ds(o, SC_LANES)] = a

        _sc_token_pipeline(per_w, nch, wid, ids_hbm, w_hbm, v_hbm, o_hbm, idx_v, w_v, rows_v, o_v,
                           gsem, isem, vsem, osem, compute)

    return k(ids, w, v2)


def _gate_gelu_body(h_ref, gate_ref, after_hbm, w_ref):
    del after_hbm
    w_ref[...] = gate_ref[...] * _gelu(h_ref[...])


def _gate_gelu(h, gate_tok, after):
    n, picks = h.shape
    rows = math.gcd(n, 1024)
    assert n % rows == 0
    return pl.pallas_call(
        _gate_gelu_body, grid=(n // rows,),
        in_specs=[pl.BlockSpec((rows, picks), lambda i: (i, 0)), pl.BlockSpec((rows, picks), lambda i: (i, 0)),
                  pl.BlockSpec(memory_space=pl.ANY)],
        out_specs=pl.BlockSpec((rows, picks), lambda i: (i, 0)),
        out_shape=jax.ShapeDtypeStruct((n, picks), F32), name="gate_gelu",
    )(h, gate_tok, after)


def _finish_body(x2_ref, o_ref, gfin_ref, y_ref, *, final_norm):
    y = x2_ref[...] + o_ref[...]
    y_ref[...] = _rms(y, gfin_ref[...]) if final_norm else y


def _finish(x2, o, g_final, final_norm):
    n, d = o.shape
    rows = math.gcd(n, 512)
    assert n % rows == 0
    return pl.pallas_call(
        functools.partial(_finish_body, final_norm=final_norm), grid=(n // rows,),
        in_specs=[pl.BlockSpec((rows, d), lambda i: (i, 0)), pl.BlockSpec((rows, d), lambda i: (i, 0)), _resident((1, d))],
        out_specs=pl.BlockSpec((rows, d), lambda i: (i, 0)),
        out_shape=jax.ShapeDtypeStruct((n, d), F32), name="finish",
    )(x2, o, g_final)


def _peer(ids_t, gate_t, xn, x2, g_final, lw, final_norm):
    picks, t = ids_t.shape
    ids_tok = ids_t.T
    total_steps = t // EXPERT_ROWS
    sc_steps = int(total_steps * SC_SHARE) if t >= SC_MIN_TOKENS else 0
    n_sc = sc_steps * EXPERT_ROWS
    if n_sc == 0:
        return _experts(ids_tok, gate_t, xn, x2, g_final, lw["peer_uv"], final_norm, 0, total_steps, g_final)
    assert n_sc % SC_WORKERS == 0 and picks % SC_CHUNK == 0
    tc_steps = total_steps - sc_steps
    first = int(tc_steps * TC_FIRST_SHARE)
    ids_sc = ids_tok.reshape(t, picks // SC_CHUNK, SC_CHUNK)
    h = _sc_dots(ids_sc, xn, lw["peer_u2"], n_sc)
    y_tc1 = _experts(ids_tok, gate_t, xn, x2, g_final, lw["peer_uv"], final_norm, sc_steps, first, g_final)
    w = _gate_gelu(h, gate_t.T, y_tc1)
    o = _sc_mix(ids_sc, w, lw["peer_v2"], n_sc)
    y_tc2 = _experts(ids_tok, gate_t, xn, x2, g_final, lw["peer_uv"], final_norm, sc_steps + first, tc_steps - first, w)
    y_sc = _finish(x2, o, g_final, final_norm)
    return jnp.concatenate([y_sc, y_tc1, y_tc2], axis=0)


def _layer(x, conv_prev, pool_prev, start_pos, mk, mv, lw, g_final, final_norm):
    bn, seq, d = x.shape
    x2, new_conv, new_pool = _mixer(x, conv_prev, pool_prev, start_pos, mk, mv, lw)
    x2 = x2.reshape(bn * seq, d)
    xn, ids_t, gate_t = _route(x2, lw["g_ffn"], lw["peer_wq"], lw["peer_keys"])
    y = _peer(ids_t, gate_t, xn, x2, g_final, lw, final_norm)
    return y.reshape(bn, seq, d), new_conv, new_pool


def kernel(x_prompt, x_sample, mem_prompt, cache_conv, cache_pool, cache_mem_k, cache_mem_v, g_mix, w_in, conv_w, conv_b, pool_w, pool_scale, g_mem, w_mk, w_mv, w_bc, w_bp, w_ba, gate_b, w_o, g_ffn, peer_wq, peer_keys, peer_u, peer_v, g_final):
    depth, d = g_mix.shape
    xp, xs = x_prompt, x_sample
    bp, bs = xp.shape[0], xs.shape[0]
    n_exp = peer_u.shape[1]
    gfin = g_final.reshape(1, d)
    conv_p, pool_p, mk_p, mv_p, conv_s, pool_s = [], [], [], [], [], []
    for l in range(depth):
        lw = dict(
            g_mix=g_mix[l].reshape(1, d), w_in=w_in[l].astype(BF16), conv_w=conv_w[l], conv_b=conv_b[l].reshape(1, -1),
            pool_w=pool_w[l].astype(BF16), pool_scale=pool_scale[l].reshape(1, -1), w_bc=w_bc[l].astype(BF16),
            w_bp=w_bp[l].astype(BF16), w_ba=w_ba[l].astype(BF16), gate_b=gate_b[l].reshape(1, -1), w_o=w_o[l].astype(BF16),
            g_ffn=g_ffn[l].reshape(1, d), peer_wq=peer_wq[l].astype(BF16), peer_keys=peer_keys[l].astype(BF16),
            peer_uv=jnp.concatenate([peer_u[l], peer_v[l]], axis=1).reshape(n_exp, 1, 2 * d),
            peer_u2=_pack_pairs(peer_u[l]), peer_v2=_pack_pairs(peer_v[l]),
        )
        last = l == depth - 1
        mk, mv = _mem_kv(mem_prompt, g_mem[l], w_mk[l], w_mv[l])
        zc = jnp.zeros((bp, CONV_K - 1, conv_w.shape[-1]), xp.dtype)
        zp = jnp.zeros((bp, POOL_STATE, pool_scale.shape[-1]), xp.dtype)
        parts = [_layer(xp[b:b + 1], zc[b:b + 1], zp[b:b + 1], 0, mk[b:b + 1], mv[b:b + 1], lw, gfin, last)
                 for b in range(bp)]
        xp, cp, pp = (jnp.concatenate(z, axis=0) for z in zip(*parts))
        n_mem = cache_mem_k.shape[2]
        xs, cs, ps = _layer(xs, cache_conv[l], cache_pool[l], PAST_LEN, cache_mem_k[l].reshape(bs, n_mem, -1),
                            cache_mem_v[l].reshape(bs, n_mem, -1), lw, gfin, last)
        heads_shape = (bp, n_mem) + cache_mem_k.shape[3:]
        conv_p.append(cp); pool_p.append(pp); mk_p.append(mk.reshape(heads_shape)); mv_p.append(mv.reshape(heads_shape))
        conv_s.append(cs); pool_s.append(ps)
    return (xp, xs, jnp.stack(conv_p), jnp.stack(pool_p), jnp.stack(mk_p), jnp.stack(mv_p),
            jnp.stack(conv_s), jnp.stack(pool_s))
```

```python
import functools
import math

import jax
import jax.numpy as jnp
from jax import lax
from jax.experimental import pallas as pl
from jax.experimental.pallas import tpu as pltpu
from jax.experimental.pallas import tpu_sc as plsc

F32 = jnp.float32
BF16 = jnp.bfloat16
EPS = 1e-6

CONV_K = 3
POOL_WINDOWS = (2, 4, 8, 16)
POOL_STATE = max(POOL_WINDOWS) - 1
PAST_LEN = 1024
MEM_HEADS = 4
PEER_HEADS = 8
PEER_TOPK = 16

SUBLANES = 8
LANES = 128
MIXER_ROWS = 256
ROUTE_ROWS = 256
EXPERT_ROWS = 128
GATHER_SLOTS = 4
VMEM_LIMIT = 56 * 1024 * 1024

SC_CORES = 2
SC_SUBCORES = 16
SC_WORKERS = SC_CORES * SC_SUBCORES
SC_LANES = 16
SC_CHUNK = 64
SC_UNROLL = 8
SC_SHARE = 0.69
SC_MIN_TOKENS = 8192
PROMPT_SEGMENTS = 2
TC_FIRST_SHARE = 0.63


def _rms(x, g):
    return x * lax.rsqrt(jnp.mean(x * x, axis=-1, keepdims=True) + EPS) * g


def _dot(a, b):
    return jnp.dot(a, b, preferred_element_type=F32)


def _dot_nt(a, b):
    return lax.dot_general(a, b, (((1,), (1,)), ((), ())), preferred_element_type=F32)


def _resident(shape):
    zeros = (0,) * len(shape)
    return pl.BlockSpec(shape, lambda *_: zeros, pipeline_mode=pl.Buffered(1))


def _memkv_body(mem_ref, g_ref, wk_ref, wv_ref, k_ref, v_ref):
    m = _rms(mem_ref[...], g_ref[...]).astype(BF16)
    k_ref[...] = _dot(m, wk_ref[...])
    v_ref[...] = _dot(m, wv_ref[...])


def _mem_kv(mem, g_mem, w_mk, w_mv):
    bn, n_mem, d = mem.shape
    att_w = w_mk.shape[1]
    k, v = pl.pallas_call(
        _memkv_body,
        out_shape=[jax.ShapeDtypeStruct((bn * n_mem, att_w), F32)] * 2,
        name="mem_kv",
    )(mem.reshape(bn * n_mem, d), g_mem.reshape(1, d), w_mk.astype(BF16), w_mv.astype(BF16))
    return k.reshape(bn, n_mem, att_w), v.reshape(bn, n_mem, att_w)


def _mixer_body(x_ref, cprev_ref, pprev_ref, mk_ref, mv_ref, gmix_ref, win_ref, convw_ref, convb_ref, poolw_ref,
                pscale_ref, wbc_ref, wbp_ref, wba_ref, gateb_ref, wo_ref,
                x2_ref, nconv_ref, npool_ref, zbuf, ubuf, *, rows, start_pos, conv_w, pool_w, att_w):
    s = pl.program_id(1)
    d_model = x_ref.shape[-1]
    z0 = SUBLANES
    u0 = 2 * SUBLANES

    @pl.when(s == 0)
    def _():
        zbuf[z0 - (CONV_K - 1):z0, :] = cprev_ref[0]
        ubuf[u0 - POOL_STATE:u0, :] = pprev_ref[0]

    x = x_ref[0]
    hb = _rms(x, gmix_ref[...]).astype(BF16)

    c0 = 0
    pa = _dot(hb, win_ref[:, c0:c0 + 3 * conv_w])
    xc, bg, cg = pa[:, :conv_w], pa[:, conv_w:2 * conv_w], pa[:, 2 * conv_w:]
    zbuf[z0:z0 + rows, :] = cg * xc
    cw = convw_ref[...]
    conv = zbuf[z0 - 2:z0 - 2 + rows, :] * cw[0:1]
    for k in range(1, CONV_K):
        conv = conv + zbuf[z0 - 2 + k:z0 - 2 + k + rows, :] * cw[k:k + 1]
    conv = conv + convb_ref[...]
    ya = _dot((bg * conv).astype(BF16), wbc_ref[...])
    last_z = zbuf[z0 + rows - (CONV_K - 1):z0 + rows, :]
    nconv_ref[0] = last_z
    zbuf[z0 - (CONV_K - 1):z0, :] = last_z
    c0 += 3 * conv_w

    up = _dot(hb, win_ref[:, c0:c0 + pool_w])
    ubuf[u0:u0 + rows, :] = up
    pos = start_pos + s * rows + lax.broadcasted_iota(jnp.int32, (rows, 1), 0)
    gw = pool_w // len(POOL_WINDOWS)
    ys = []
    for g, w in enumerate(POOL_WINDOWS):
        cur = up[:, g * gw:(g + 1) * gw]
        acc = cur
        for k in range(1, w):
            acc = acc + ubuf[u0 - k:u0 - k + rows, g * gw:(g + 1) * gw]
        cnt = jnp.minimum(pos + 1, w).astype(F32)
        ys.append(_dot((acc / cnt - cur).astype(BF16), poolw_ref[g]))
    yb = _dot((jnp.concatenate(ys, axis=-1) * pscale_ref[...]).astype(BF16), wbp_ref[...])
    last_u = ubuf[u0 + rows - POOL_STATE:u0 + rows, :]
    npool_ref[0] = last_u
    ubuf[u0 - POOL_STATE:u0, :] = last_u
    c0 += pool_w

    q = _dot(hb, win_ref[:, c0:c0 + att_w])
    kb = mk_ref[0].astype(BF16)
    vb = mv_ref[0].astype(BF16)
    hd = att_w // MEM_HEADS
    heads = []
    for h in range(MEM_HEADS):
        sc = _dot_nt(q[:, h * hd:(h + 1) * hd].astype(BF16), kb[:, h * hd:(h + 1) * hd]) * (hd ** -0.5)
        e = jnp.exp(sc - jnp.max(sc, axis=-1, keepdims=True))
        p = e / jnp.sum(e, axis=-1, keepdims=True)
        heads.append(_dot(p.astype(BF16), vb[:, h * hd:(h + 1) * hd]))
    yc = _dot(jnp.concatenate(heads, axis=-1).astype(BF16), wba_ref[...])
    c0 += att_w

    merged = None
    for i, y in enumerate((ya, yb, yc)):
        gl = _dot(hb, win_ref[:, c0 + i * d_model:c0 + (i + 1) * d_model]) + gateb_ref[:, i * d_model:(i + 1) * d_model]
        term = (1.0 / (1.0 + jnp.exp(-gl))) * y
        merged = term if merged is None else merged + term
    x2_ref[0] = x + _dot(merged.astype(BF16), wo_ref[...])


def _mixer(x, conv_prev, pool_prev, start_pos, mk, mv, lw):
    bn, seq, d = x.shape
    conv_w, pool_w = conv_prev.shape[-1], pool_prev.shape[-1]
    n_mem, att_w = mk.shape[1], mk.shape[2]
    rows = min(MIXER_ROWS, seq)
    assert seq % rows == 0 and rows % SUBLANES == 0 and rows >= POOL_STATE
    in_cols = lw["w_in"].shape[1]
    per_b = lambda shape: pl.BlockSpec((1,) + shape, lambda b, s: (b, 0, 0))
    body = functools.partial(_mixer_body, rows=rows, start_pos=start_pos, conv_w=conv_w, pool_w=pool_w, att_w=att_w)
    return pl.pallas_call(
        body,
        grid=(bn, seq // rows),
        in_specs=[
            pl.BlockSpec((1, rows, d), lambda b, s: (b, s, 0)),
            per_b((CONV_K - 1, conv_w)), per_b((POOL_STATE, pool_w)), per_b((n_mem, att_w)), per_b((n_mem, att_w)),
            _resident((1, d)), _resident((d, in_cols)), _resident((CONV_K, conv_w)), _resident((1, conv_w)),
            _resident(lw["pool_w"].shape), _resident((1, pool_w)), _resident((conv_w, d)), _resident((pool_w, d)),
            _resident((att_w, d)), _resident((1, 3 * d)), _resident((d, d)),
        ],
        out_specs=[
            pl.BlockSpec((1, rows, d), lambda b, s: (b, s, 0)),
            per_b((CONV_K - 1, conv_w)), per_b((POOL_STATE, pool_w)),
        ],
        out_shape=[
            jax.ShapeDtypeStruct((bn, seq, d), F32),
            jax.ShapeDtypeStruct((bn, CONV_K - 1, conv_w), F32),
            jax.ShapeDtypeStruct((bn, POOL_STATE, pool_w), F32),
        ],
        scratch_shapes=[pltpu.VMEM((SUBLANES + rows, conv_w), F32), pltpu.VMEM((2 * SUBLANES + rows, pool_w), F32)],
        compiler_params=pltpu.CompilerParams(dimension_semantics=("arbitrary", "arbitrary"), vmem_limit_bytes=VMEM_LIMIT),
        name="mixer",
    )(x, conv_prev, pool_prev, mk, mv, lw["g_mix"], lw["w_in"], lw["conv_w"], lw["conv_b"], lw["pool_w"],
      lw["pool_scale"], lw["w_bc"], lw["w_bp"], lw["w_ba"], lw["gate_b"], lw["w_o"])


_STAIR = [(a, b) for a in range(PEER_TOPK) for b in range(PEER_TOPK) if (a + 1) * (b + 1) <= PEER_TOPK]
_STAIR_HEAD = 24
_STAIR_TAIL = [_STAIR[_STAIR_HEAD + 8 * i:_STAIR_HEAD + 8 * (i + 1)] for i in range(4)]


def _top16(v, payload=None):
    n, t = v.shape
    rows = lax.broadcasted_iota(jnp.int32, (n, t), 0)
    out_rows = lax.broadcasted_iota(jnp.int32, (PEER_TOPK, t), 0)
    vals = jnp.zeros((PEER_TOPK, t), F32)
    pay = jnp.zeros((PEER_TOPK, t), jnp.int32)
    for k in range(PEER_TOPK):
        m = jnp.max(v, axis=0, keepdims=True)
        am = jnp.min(jnp.where(v == m, rows, n), axis=0, keepdims=True)
        hit = rows == am
        p = am if payload is None else jnp.sum(jnp.where(hit, payload, 0), axis=0, keepdims=True)
        vals = jnp.where(out_rows == k, m, vals)
        pay = jnp.where(out_rows == k, p, pay)
        v = jnp.where(hit, -jnp.inf, v)
    return vals, pay


def _take_rows(src, idxs):
    t = src.shape[1]
    out_rows = lax.broadcasted_iota(jnp.int32, (SUBLANES, t), 0)
    acc = jnp.broadcast_to(src[idxs[0]:idxs[0] + 1, :], (SUBLANES, t))
    for i in range(1, SUBLANES):
        acc = jnp.where(out_rows == i, src[idxs[i]:idxs[i] + 1, :], acc)
    return acc


def _stair(first, second, combine):
    blocks = [combine(first[0:1, :], second), combine(first[1:2, :], second[0:SUBLANES, :])]
    for blk in _STAIR_TAIL:
        blk = blk + [(0, 0)] * (SUBLANES - len(blk))
        blocks.append(combine(_take_rows(first, [a for a, _ in blk]), _take_rows(second, [b for _, b in blk])))
    return jnp.concatenate(blocks, axis=0)


def _route_body(x2_ref, gffn_ref, wq_ref, keys_ref, xn_ref, ids_ref, gate_ref, *, n_keys):
    xn = _rms(x2_ref[...], gffn_ref[...])
    xn_ref[...] = xn
    q = _dot(xn.astype(BF16), wq_ref[...])
    half = keys_ref.shape[-1]
    t = q.shape[0]
    for h in range(PEER_HEADS):
        scores = [_dot_nt(keys_ref[p], q[:, (2 * h + p) * half:(2 * h + p + 1) * half].astype(BF16)) for p in range(2)]
        for l0 in range(0, t, LANES):
            (s1, i1), (s2, i2) = [_top16(sc[:, l0:l0 + LANES]) for sc in scores]
            cand = _stair(s1, s2, lambda a, b: a + b)
            pad = lax.broadcasted_iota(jnp.int32, cand.shape, 0) >= len(_STAIR)
            expert = _stair(i1, i2, lambda a, b: a * n_keys + b)
            sc, e = _top16(jnp.where(pad, -jnp.inf, cand), expert)
            ex = jnp.exp(sc - sc[0:1, :])
            ids_ref[h * PEER_TOPK:(h + 1) * PEER_TOPK, l0:l0 + LANES] = e
            gate_ref[h * PEER_TOPK:(h + 1) * PEER_TOPK, l0:l0 + LANES] = ex / jnp.sum(ex, axis=0, keepdims=True)


def _route(x2, g_ffn, wq, keys):
    t, d = x2.shape
    rows = ROUTE_ROWS
    assert t % rows == 0
    picks = PEER_HEADS * PEER_TOPK
    return pl.pallas_call(
        functools.partial(_route_body, n_keys=keys.shape[1]),
        grid=(t // rows,),
        in_specs=[pl.BlockSpec((rows, d), lambda i: (i, 0)), _resident((1, d)), _resident(wq.shape), _resident(keys.shape)],
        out_specs=[pl.BlockSpec((rows, d), lambda i: (i, 0)), pl.BlockSpec((picks, rows), lambda i: (0, i)),
                   pl.BlockSpec((picks, rows), lambda i: (0, i))],
        out_shape=[jax.ShapeDtypeStruct((t, d), F32), jax.ShapeDtypeStruct((picks, t), jnp.int32),
                   jax.ShapeDtypeStruct((picks, t), F32)],
        compiler_params=pltpu.CompilerParams(dimension_semantics=("arbitrary",), vmem_limit_bytes=VMEM_LIMIT),
        name="route",
    )(x2, g_ffn, wq, keys)


def _gelu(h):
    return 0.5 * h * (1.0 + lax.erf(h * (2.0 ** -0.5)))


def _experts_body(ids_hbm, gate_ref, xn_ref, x2_ref, gfin_ref, uv_hbm, after_hbm, y_ref, *scratch,
                  rows, picks, first_step, steps, final_norm):
    del after_hbm
    bufs, (sem, ids_smem, ids_sem) = scratch[:GATHER_SLOTS], scratch[GATHER_SLOTS:]
    i = pl.program_id(0)
    half = i % 2
    n = rows * picks
    d = xn_ref.shape[-1]
    lead = GATHER_SLOTS - 1
    groups = picks // SUBLANES

    def ids_copy(step, which):
        return pltpu.make_async_copy(ids_hbm.at[first_step + step], ids_smem.at[pl.ds(which * n, n)], ids_sem.at[which])

    def issue(off, slot, j0, j1):
        for j in range(j0, j1):
            pltpu.async_copy(uv_hbm.at[ids_smem[off + j]], bufs[slot].at[pl.ds(j, 1)], sem.at[slot], priority=j % 2)

    def wait_slot(slot):
        pltpu.make_async_copy(bufs[slot], bufs[slot], sem.at[slot]).wait()

    @pl.when(i == 0)
    def _():
        ids_copy(0, 0).start()
        ids_copy(0, 0).wait()
        if steps > 1:
            ids_copy(1, 1).start()
        for s in range(lead):
            issue(s * picks, s, 0, picks)

    lane = lax.broadcasted_iota(jnp.int32, (picks, rows), 1)

    def token(t, slot, ahead_off):
        ahead_slot = (slot + lead) % GATHER_SLOTS
        buf = bufs[slot]
        wait_slot(slot)
        if ahead_off is not None:
            issue(ahead_off, ahead_slot, 0, picks)
        x = xn_ref[pl.ds(t, 1), :]
        g = jnp.sum(jnp.where(lane == t, gate_ref[...], 0.0), axis=-1, keepdims=True)
        hs = []
        for k in range(groups):
            r0 = k * SUBLANES
            hs.append(jnp.sum(buf[r0:r0 + SUBLANES, 0:d] * x, axis=-1, keepdims=True))
        w = g * _gelu(jnp.concatenate(hs, axis=0))
        acc = None
        for k in range(groups):
            r0 = k * SUBLANES
            part = w[r0:r0 + SUBLANES, :] * buf[r0:r0 + SUBLANES, d:2 * d]
            acc = part if acc is None else acc + part
        y = x2_ref[pl.ds(t, 1), :] + jnp.sum(acc, axis=0, keepdims=True)
        y_ref[pl.ds(t, 1), :] = _rms(y, gfin_ref[...]) if final_norm else y

    def four_tokens(k, carry):
        for s in range(GATHER_SLOTS):
            t = k * GATHER_SLOTS + s
            token(t, s, half * n + (t + lead) * picks)
        return carry

    lax.fori_loop(0, rows // GATHER_SLOTS - 1, four_tokens, 0)

    t0 = rows - GATHER_SLOTS
    token(t0, 0, half * n + (t0 + lead) * picks)

    @pl.when(i + 1 < steps)
    def _():
        ids_copy(i + 1, 1 - half).wait()

    for s in range(1, GATHER_SLOTS):
        @pl.when(i + 1 < steps)
        def _():
            issue((1 - half) * n + (s - 1) * picks, (s + lead) % GATHER_SLOTS, 0, picks)

        token(t0 + s, s, None)

    @pl.when(i + 2 < steps)
    def _():
        ids_copy(i + 2, half).start()


def _experts(ids_tok, gate_t, xn, x2, g_final, uv, final_norm, first_step, steps, after):
    t, picks = ids_tok.shape
    d = xn.shape[1]
    rows = EXPERT_ROWS
    assert t % rows == 0 and rows >= 2 * GATHER_SLOTS and rows % GATHER_SLOTS == 0 and picks % (2 * SUBLANES) == 0
    ids = ids_tok.reshape(t // rows, rows * picks)
    body = functools.partial(_experts_body, rows=rows, picks=picks, first_step=first_step, steps=steps,
                             final_norm=final_norm)
    return pl.pallas_call(
        body,
        grid=(steps,),
        in_specs=[
            pl.BlockSpec(memory_space=pl.ANY),
            pl.BlockSpec((picks, rows), lambda i: (0, first_step + i)),
            pl.BlockSpec((rows, d), lambda i: (first_step + i, 0)),
            pl.BlockSpec((rows, d), lambda i: (first_step + i, 0)),
            _resident((1, d)),
            pl.BlockSpec(memory_space=pl.ANY),
            pl.BlockSpec(memory_space=pl.ANY),
        ],
        out_specs=pl.BlockSpec((rows, d), lambda i: (i, 0)),
        out_shape=jax.ShapeDtypeStruct((steps * rows, d), F32),
        scratch_shapes=[pltpu.VMEM((picks, 2 * d), F32)] * GATHER_SLOTS + [
            pltpu.SemaphoreType.DMA((GATHER_SLOTS,)),
            pltpu.SMEM((2 * rows * picks,), jnp.int32),
            pltpu.SemaphoreType.DMA((2,)),
        ],
        compiler_params=pltpu.CompilerParams(dimension_semantics=("arbitrary",), vmem_limit_bytes=VMEM_LIMIT),
        name="experts",
    )(ids, gate_t, xn, x2, g_final, uv, after)


def _sc_token_pipeline(per_w, nch, wid, ids_hbm, vec_hbm, tab_hbm, out_hbm, idx_v, vec_v, rows_v, out_v,
                       gsem, isem, vsem, osem, compute):
    base = wid * per_w

    def ids_copy(t, p):
        return pltpu.make_async_copy(ids_hbm.at[t], idx_v.at[p], isem.at[p])

    def vec_copy(t, p):
        return pltpu.make_async_copy(vec_hbm.at[t], vec_v.at[p], vsem.at[p])

    def out_copy(t, p):
        return pltpu.make_async_copy(out_v.at[p], out_hbm.at[t], osem.at[p])

    def gather(p, c, b):
        return pltpu.make_async_copy(tab_hbm.at[idx_v.at[p, c]], rows_v.at[b], gsem.at[b])

    ids_copy(base, 0).start()
    vec_copy(base, 0).start()
    ids_copy(base, 0).wait()
    vec_copy(base, 0).wait()
    gather(0, 0, 0).start()

    def pair(ii, carry):
        for p in range(2):
            i = 2 * ii + p
            t = base + i
            has_next = i + 1 < per_w

            @pl.when(has_next)
            def _():
                ids_copy(t + 1, 1 - p).start()
                vec_copy(t + 1, 1 - p).start()

            @pl.when(i >= 2)
            def _():
                out_copy(t - 2, p).wait()

            for c in range(nch):
                b = c % 2
                if c + 1 < nch:
                    gather(p, c + 1, 1 - b).start()
                else:
                    @pl.when(has_next)
                    def _():
                        ids_copy(t + 1, 1 - p).wait()
                        vec_copy(t + 1, 1 - p).wait()
                        gather(1 - p, 0, 1 - b).start()
                gather(p, c, b).wait()
                compute(p, c, b)
            out_copy(t, p).start()
        return carry

    lax.fori_loop(0, per_w // 2, pair, 0)
    out_copy(base + per_w - 2, 0).wait()
    out_copy(base + per_w - 1, 1).wait()


HI_MASK = -65536


def _pack_pairs(tab):
    n, d = tab.shape
    bits = lax.bitcast_convert_type(tab.astype(jnp.bfloat16), jnp.uint16).astype(jnp.uint32)
    return lax.bitcast_convert_type((bits[:, :d // 2] << 16) | bits[:, d // 2:], jnp.int32)


def _unpack(words):
    hi = lax.bitcast_convert_type(words & HI_MASK, F32)
    lo = lax.bitcast_convert_type(words << 16, F32)
    return hi, lo


def _sc_dots(ids, xn, u2, n_tok):
    d = xn.shape[1]
    dw = u2.shape[1]
    nch = ids.shape[1]
    picks = nch * SC_CHUNK
    per_w = n_tok // SC_WORKERS
    assert per_w % 2 == 0 and nch % 2 == 0 and 2 * dw == d
    mesh = plsc.VectorSubcoreMesh(core_axis_name="c", subcore_axis_name="s")

    @functools.partial(
        pl.kernel, mesh=mesh, out_type=jax.ShapeDtypeStruct((n_tok, picks), F32),
        scratch_types=[pltpu.VMEM((2, nch, SC_CHUNK), jnp.int32), pltpu.VMEM((2, d), F32),
                       pltpu.VMEM((2, SC_CHUNK, dw), jnp.int32), pltpu.VMEM((2, picks), F32)]
        + [pltpu.SemaphoreType.DMA((2,))] * 4,
        compiler_params=pltpu.CompilerParams(needs_layout_passes=False), name="sc_dots")
    def k(ids_hbm, xn_hbm, u_hbm, h_hbm, idx_v, x_v, rows_v, h_v, gsem, isem, vsem, osem):
        wid = lax.axis_index("s") * SC_CORES + lax.axis_index("c")
        lane = lax.iota(jnp.int32, SC_LANES)

        def compute(p, c, b):
            for g in range(SC_CHUNK // SC_LANES):
                def rows4(q, hv):
                    j0 = g * SC_LANES + q * 4

                    def span(cc, accs):
                        for uu in range(SC_UNROLL):
                            off = pl.multiple_of((cc * SC_UNROLL + uu) * SC_LANES, SC_LANES)
                            x_hi = x_v[p, pl.ds(off, SC_LANES)]
                            x_lo = x_v[p, pl.ds(dw + off, SC_LANES)]
                            nxt = []
                            for kk in range(4):
                                hi, lo = _unpack(rows_v[b, j0 + kk, pl.ds(off, SC_LANES)])
                                nxt.append((accs[2 * kk] + hi * x_hi, accs[2 * kk + 1] + lo * x_lo))
                            accs = tuple(a for pair in nxt for a in pair)
                        return accs

                    accs = lax.fori_loop(0, dw // (SC_LANES * SC_UNROLL), span,
                                         tuple(jnp.zeros((SC_LANES,), F32) for _ in range(8)))
                    for kk in range(4):
                        hv = jnp.where(lane == q * 4 + kk, jnp.sum(accs[2 * kk] + accs[2 * kk + 1]), hv)
                    return hv

                hv = lax.fori_loop(0, SC_LANES // 4, rows4, jnp.zeros((SC_LANES,), F32))
                h_v[p, pl.ds(c * SC_CHUNK + g * SC_LANES, SC_LANES)] = hv

        _sc_token_pipeline(per_w, nch, wid, ids_hbm, xn_hbm, u_hbm, h_hbm, idx_v, x_v, rows_v, h_v,
                           gsem, isem, vsem, osem, compute)

    return k(ids, xn, u2)


def _sc_mix(ids, w, v2, n_tok):
    dw = v2.shape[1]
    d = 2 * dw
    nch = ids.shape[1]
    picks = nch * SC_CHUNK
    per_w = n_tok // SC_WORKERS
    assert per_w % 2 == 0 and nch % 2 == 0
    cb = 8
    mesh = plsc.VectorSubcoreMesh(core_axis_name="c", subcore_axis_name="s")

    @functools.partial(
        pl.kernel, mesh=mesh, out_type=jax.ShapeDtypeStruct((n_tok, d), F32),
        scratch_types=[pltpu.VMEM((2, nch, SC_CHUNK), jnp.int32), pltpu.VMEM((2, picks), F32),
                       pltpu.VMEM((2, SC_CHUNK, dw), jnp.int32), pltpu.VMEM((2, d), F32)]
        + [pltpu.SemaphoreType.DMA((2,))] * 4,
        compiler_params=pltpu.CompilerParams(needs_layout_passes=False), name="sc_mix")
    def k(ids_hbm, w_hbm, v_hbm, o_hbm, idx_v, w_v, rows_v, o_v, gsem, isem, vsem, osem):
        wid = lax.axis_index("s") * SC_CORES + lax.axis_index("c")

        def compute(p, c, b):
            for blk in range(dw // (cb * SC_LANES)):
                base = blk * cb * SC_LANES
                offs = [base + kk * SC_LANES for kk in range(cb)] + [dw + base + kk * SC_LANES for kk in range(cb)]
                if c == 0:
                    init = tuple(jnp.zeros((SC_LANES,), F32) for _ in offs)
                else:
                    init = tuple(o_v[p, pl.ds(o, SC_LANES)] for o in offs)

                def row(r, accs):
                    wj = plsc.load_gather(w_v.at[p], [jnp.full((SC_LANES,), c * SC_CHUNK, jnp.int32) + r])
                    his, los = [], []
                    for kk in range(cb):
                        hi, lo = _unpack(rows_v[b, r, pl.ds(base + kk * SC_LANES, SC_LANES)])
                        his.append(accs[kk] + wj * hi)
                        los.append(accs[cb + kk] + wj * lo)
                    return tuple(his + los)

                accs = lax.fori_loop(0, SC_CHUNK, row, init)
                for o, a in zip(offs, accs):
                    o_v[p, pl.ds(o, SC_LANES)] = a

        _sc_token_pipeline(per_w, nch, wid, ids_hbm, w_hbm, v_hbm, o_hbm, idx_v, w_v, rows_v, o_v,
                           gsem, isem, vsem, osem, compute)

    return k(ids, w, v2)


def _gate_gelu_body(h_ref, gate_ref, after_hbm, w_ref):
    del after_hbm
    w_ref[...] = gate_ref[...] * _gelu(h_ref[...])


def _gate_gelu(h, gate_tok, after):
    n, picks = h.shape
    rows = math.gcd(n, 1024)
    assert n % rows == 0
    return pl.pallas_call(
        _gate_gelu_body, grid=(n // rows,),
        in_specs=[pl.BlockSpec((rows, picks), lambda i: (i, 0)), pl.BlockSpec((rows, picks), lambda i: (i, 0)),
                  pl.BlockSpec(memory_space=pl.ANY)],
        out_specs=pl.BlockSpec((rows, picks), lambda i: (i, 0)),
        out_shape=jax.ShapeDtypeStruct((n, picks), F32), name="gate_gelu",
    )(h, gate_tok, after)


def _finish_body(x2_ref, o_ref, gfin_ref, y_ref, *, final_norm):
    y = x2_ref[...] + o_ref[...]
    y_ref[...] = _rms(y, gfin_ref[...]) if final_norm else y


def _finish(x2, o, g_final, final_norm):
    n, d = o.shape
    rows = math.gcd(n, 512)
    assert n % rows == 0
    return pl.pallas_call(
        functools.partial(_finish_body, final_norm=final_norm), grid=(n // rows,),
        in_specs=[pl.BlockSpec((rows, d), lambda i: (i, 0)), pl.BlockSpec((rows, d), lambda i: (i, 0)), _resident((1, d))],
        out_specs=pl.BlockSpec((rows, d), lambda i: (i, 0)),
        out_shape=jax.ShapeDtypeStruct((n, d), F32), name="finish",
    )(x2, o, g_final)


def _peer(ids_t, gate_t, xn, x2, g_final, lw, final_norm):
    picks, t = ids_t.shape
    ids_tok = ids_t.T
    total_steps = t // EXPERT_ROWS
    sc_steps = int(total_steps * SC_SHARE) if t >= SC_MIN_TOKENS else 0
    n_sc = sc_steps * EXPERT_ROWS
    if n_sc == 0:
        return [_experts(ids_tok, gate_t, xn, x2, g_final, lw["peer_uv"], final_norm, 0, total_steps, g_final)]
    assert n_sc % SC_WORKERS == 0 and picks % SC_CHUNK == 0
    tc_steps = total_steps - sc_steps
    first = int(tc_steps * TC_FIRST_SHARE)
    ids_sc = ids_tok.reshape(t, picks // SC_CHUNK, SC_CHUNK)
    h = _sc_dots(ids_sc, xn, lw["peer_u2"], n_sc)
    y_tc1 = _experts(ids_tok, gate_t, xn, x2, g_final, lw["peer_uv"], final_norm, sc_steps, first, g_final)
    w = _gate_gelu(h, gate_t.T, y_tc1)
    o = _sc_mix(ids_sc, w, lw["peer_v2"], n_sc)
    y_tc2 = _experts(ids_tok, gate_t, xn, x2, g_final, lw["peer_uv"], final_norm, sc_steps + first, tc_steps - first, w)
    y_sc = _finish(x2, o, g_final, final_norm)
    return [y_sc, y_tc1, y_tc2]


def _layer(x, conv_prev, pool_prev, start_pos, mk, mv, lw, g_final, final_norm):
    bn, seq, d = x.shape
    x2, new_conv, new_pool = _mixer(x, conv_prev, pool_prev, start_pos, mk, mv, lw)
    x2 = x2.reshape(bn * seq, d)
    xn, ids_t, gate_t = _route(x2, lw["g_ffn"], lw["peer_wq"], lw["peer_keys"])
    y = _peer(ids_t, gate_t, xn, x2, g_final, lw, final_norm)
    return y, new_conv, new_pool


def kernel(x_prompt, x_sample, mem_prompt, cache_conv, cache_pool, cache_mem_k, cache_mem_v, g_mix, w_in, conv_w, conv_b, pool_w, pool_scale, g_mem, w_mk, w_mv, w_bc, w_bp, w_ba, gate_b, w_o, g_ffn, peer_wq, peer_keys, peer_u, peer_v, g_final):
    depth, d = g_mix.shape
    xp, xs = x_prompt, x_sample
    bp, bs = xp.shape[0], xs.shape[0]
    n_exp = peer_u.shape[1]
    gfin = g_final.reshape(1, d)
    conv_p, pool_p, mk_p, mv_p, conv_s, pool_s = [], [], [], [], [], []
    for l in range(depth):
        lw = dict(
            g_mix=g_mix[l].reshape(1, d), w_in=w_in[l].astype(BF16), conv_w=conv_w[l], conv_b=conv_b[l].reshape(1, -1),
            pool_w=pool_w[l].astype(BF16), pool_scale=pool_scale[l].reshape(1, -1), w_bc=w_bc[l].astype(BF16),
            w_bp=w_bp[l].astype(BF16), w_ba=w_ba[l].astype(BF16), gate_b=gate_b[l].reshape(1, -1), w_o=w_o[l].astype(BF16),
            g_ffn=g_ffn[l].reshape(1, d), peer_wq=peer_wq[l].astype(BF16), peer_keys=peer_keys[l].astype(BF16),
            peer_uv=jnp.concatenate([peer_u[l], peer_v[l]], axis=1).reshape(n_exp, 1, 2 * d),
            peer_u2=_pack_pairs(peer_u[l]), peer_v2=_pack_pairs(peer_v[l]),
        )
        last = l == depth - 1
        mk, mv = _mem_kv(mem_prompt, g_mem[l], w_mk[l], w_mv[l])
        zc = jnp.zeros((bp, CONV_K - 1, conv_w.shape[-1]), xp.dtype)
        zp = jnp.zeros((bp, POOL_STATE, pool_scale.shape[-1]), xp.dtype)
        seq = xp.shape[1]
        seg = seq // PROMPT_SEGMENTS if seq % PROMPT_SEGMENTS == 0 and seq // PROMPT_SEGMENTS >= SC_MIN_TOKENS else seq
        pieces, cps, pps = [], [], []
        for b in range(bp):
            cprev, pprev = zc[b:b + 1], zp[b:b + 1]
            for s0 in range(0, seq, seg):
                ys, cprev, pprev = _layer(xp[b:b + 1, s0:s0 + seg], cprev, pprev, s0, mk[b:b + 1], mv[b:b + 1], lw, gfin, last)
                pieces += ys
            cps.append(cprev); pps.append(pprev)
        xp = jnp.concatenate(pieces, axis=0).reshape(bp, seq, d)
        cp, pp = jnp.concatenate(cps, axis=0), jnp.concatenate(pps, axis=0)
        n_mem = cache_mem_k.shape[2]
        ys, cs, ps = _layer(xs, cache_conv[l], cache_pool[l], PAST_LEN, cache_mem_k[l].reshape(bs, n_mem, -1),
                            cache_mem_v[l].reshape(bs, n_mem, -1), lw, gfin, last)
        xs = jnp.concatenate(ys, axis=0).reshape(xs.shape)
        heads_shape = (bp, n_mem) + cache_mem_k.shape[3:]
        conv_p.append(cp); pool_p.append(pp); mk_p.append(mk.reshape(heads_shape)); mv_p.append(mv.reshape(heads_shape))
        conv_s.append(cs); pool_s.append(ps)
    return (xp, xs, jnp.stack(conv_p), jnp.stack(pool_p), jnp.stack(mk_p), jnp.stack(mv_p),
            jnp.stack(conv_s), jnp.stack(pool_s))
```

```python
import functools
import math

import jax
import jax.numpy as jnp
from jax import lax
from jax.experimental import pallas as pl
from jax.experimental.pallas import tpu as pltpu
from jax.experimental.pallas import tpu_sc as plsc

F32 = jnp.float32
BF16 = jnp.bfloat16
EPS = 1e-6

CONV_K = 3
POOL_WINDOWS = (2, 4, 8, 16)
POOL_STATE = max(POOL_WINDOWS) - 1
PAST_LEN = 1024
MEM_HEADS = 4
PEER_HEADS = 8
PEER_TOPK = 16

SUBLANES = 8
LANES = 128
MIXER_ROWS = 256
ROUTE_ROWS = 256
EXPERT_ROWS = 128
GATHER_SLOTS = 4
VMEM_LIMIT = 56 * 1024 * 1024

SC_CORES = 2
SC_SUBCORES = 16
SC_WORKERS = SC_CORES * SC_SUBCORES
SC_LANES = 16
SC_CHUNK = 64
SC_UNROLL = 8
SC_SHARE = 0.68
SC_MIN_TOKENS = 8192
PROMPT_SEGMENTS = 2
TC_FIRST_SHARE = 0.63


def _rms(x, g):
    return x * lax.rsqrt(jnp.mean(x * x, axis=-1, keepdims=True) + EPS) * g


def _dot(a, b):
    return jnp.dot(a, b, preferred_element_type=F32)


def _dot_nt(a, b):
    return lax.dot_general(a, b, (((1,), (1,)), ((), ())), preferred_element_type=F32)


def _resident(shape):
    zeros = (0,) * len(shape)
    return pl.BlockSpec(shape, lambda *_: zeros, pipeline_mode=pl.Buffered(1))


def _memkv_body(mem_ref, g_ref, wk_ref, wv_ref, k_ref, v_ref):
    m = _rms(mem_ref[...], g_ref[...]).astype(BF16)
    k_ref[...] = _dot(m, wk_ref[...])
    v_ref[...] = _dot(m, wv_ref[...])


def _mem_kv(mem, g_mem, w_mk, w_mv):
    bn, n_mem, d = mem.shape
    att_w = w_mk.shape[1]
    k, v = pl.pallas_call(
        _memkv_body,
        out_shape=[jax.ShapeDtypeStruct((bn * n_mem, att_w), F32)] * 2,
        name="mem_kv",
    )(mem.reshape(bn * n_mem, d), g_mem.reshape(1, d), w_mk.astype(BF16), w_mv.astype(BF16))
    return k.reshape(bn, n_mem, att_w), v.reshape(bn, n_mem, att_w)


def _mixer_body(x_ref, cprev_ref, pprev_ref, mk_ref, mv_ref, gmix_ref, win_ref, convw_ref, convb_ref, poolw_ref,
                pscale_ref, wbc_ref, wbp_ref, wba_ref, gateb_ref, wo_ref,
                x2_ref, nconv_ref, npool_ref, zbuf, ubuf, *, rows, start_pos, conv_w, pool_w, att_w):
    s = pl.program_id(1)
    d_model = x_ref.shape[-1]
    z0 = SUBLANES
    u0 = 2 * SUBLANES

    @pl.when(s == 0)
    def _():
        zbuf[z0 - (CONV_K - 1):z0, :] = cprev_ref[0]
        ubuf[u0 - POOL_STATE:u0, :] = pprev_ref[0]

    x = x_ref[0]
    hb = _rms(x, gmix_ref[...]).astype(BF16)

    c0 = 0
    pa = _dot(hb, win_ref[:, c0:c0 + 3 * conv_w])
    xc, bg, cg = pa[:, :conv_w], pa[:, conv_w:2 * conv_w], pa[:, 2 * conv_w:]
    zbuf[z0:z0 + rows, :] = cg * xc
    cw = convw_ref[...]
    conv = zbuf[z0 - 2:z0 - 2 + rows, :] * cw[0:1]
    for k in range(1, CONV_K):
        conv = conv + zbuf[z0 - 2 + k:z0 - 2 + k + rows, :] * cw[k:k + 1]
    conv = conv + convb_ref[...]
    ya = _dot((bg * conv).astype(BF16), wbc_ref[...])
    last_z = zbuf[z0 + rows - (CONV_K - 1):z0 + rows, :]
    nconv_ref[0] = last_z
    zbuf[z0 - (CONV_K - 1):z0, :] = last_z
    c0 += 3 * conv_w

    up = _dot(hb, win_ref[:, c0:c0 + pool_w])
    ubuf[u0:u0 + rows, :] = up
    pos = start_pos + s * rows + lax.broadcasted_iota(jnp.int32, (rows, 1), 0)
    gw = pool_w // len(POOL_WINDOWS)
    ys = []
    for g, w in enumerate(POOL_WINDOWS):
        cur = up[:, g * gw:(g + 1) * gw]
        acc = cur
        for k in range(1, w):
            acc = acc + ubuf[u0 - k:u0 - k + rows, g * gw:(g + 1) * gw]
        cnt = jnp.minimum(pos + 1, w).astype(F32)
        ys.append(_dot((acc / cnt - cur).astype(BF16), poolw_ref[g]))
    yb = _dot((jnp.concatenate(ys, axis=-1) * pscale_ref[...]).astype(BF16), wbp_ref[...])
    last_u = ubuf[u0 + rows - POOL_STATE:u0 + rows, :]
    npool_ref[0] = last_u
    ubuf[u0 - POOL_STATE:u0, :] = last_u
    c0 += pool_w

    q = _dot(hb, win_ref[:, c0:c0 + att_w])
    kb = mk_ref[0].astype(BF16)
    vb = mv_ref[0].astype(BF16)
    hd = att_w // MEM_HEADS
    heads = []
    for h in range(MEM_HEADS):
        sc = _dot_nt(q[:, h * hd:(h + 1) * hd].astype(BF16), kb[:, h * hd:(h + 1) * hd]) * (hd ** -0.5)
        e = jnp.exp(sc - jnp.max(sc, axis=-1, keepdims=True))
        p = e / jnp.sum(e, axis=-1, keepdims=True)
        heads.append(_dot(p.astype(BF16), vb[:, h * hd:(h + 1) * hd]))
    yc = _dot(jnp.concatenate(heads, axis=-1).astype(BF16), wba_ref[...])
    c0 += att_w

    merged = None
    for i, y in enumerate((ya, yb, yc)):
        gl = _dot(hb, win_ref[:, c0 + i * d_model:c0 + (i + 1) * d_model]) + gateb_ref[:, i * d_model:(i + 1) * d_model]
        term = (1.0 / (1.0 + jnp.exp(-gl))) * y
        merged = term if merged is None else merged + term
    x2_ref[0] = x + _dot(merged.astype(BF16), wo_ref[...])


def _mixer(x, conv_prev, pool_prev, start_pos, mk, mv, lw):
    bn, seq, d = x.shape
    conv_w, pool_w = conv_prev.shape[-1], pool_prev.shape[-1]
    n_mem, att_w = mk.shape[1], mk.shape[2]
    rows = min(MIXER_ROWS, seq)
    assert seq % rows == 0 and rows % SUBLANES == 0 and rows >= POOL_STATE
    in_cols = lw["w_in"].shape[1]
    per_b = lambda shape: pl.BlockSpec((1,) + shape, lambda b, s: (b, 0, 0))
    body = functools.partial(_mixer_body, rows=rows, start_pos=start_pos, conv_w=conv_w, pool_w=pool_w, att_w=att_w)
    return pl.pallas_call(
        body,
        grid=(bn, seq // rows),
        in_specs=[
            pl.BlockSpec((1, rows, d), lambda b, s: (b, s, 0)),
            per_b((CONV_K - 1, conv_w)), per_b((POOL_STATE, pool_w)), per_b((n_mem, att_w)), per_b((n_mem, att_w)),
            _resident((1, d)), _resident((d, in_cols)), _resident((CONV_K, conv_w)), _resident((1, conv_w)),
            _resident(lw["pool_w"].shape), _resident((1, pool_w)), _resident((conv_w, d)), _resident((pool_w, d)),
            _resident((att_w, d)), _resident((1, 3 * d)), _resident((d, d)),
        ],
        out_specs=[
            pl.BlockSpec((1, rows, d), lambda b, s: (b, s, 0)),
            per_b((CONV_K - 1, conv_w)), per_b((POOL_STATE, pool_w)),
        ],
        out_shape=[
            jax.ShapeDtypeStruct((bn, seq, d), F32),
            jax.ShapeDtypeStruct((bn, CONV_K - 1, conv_w), F32),
            jax.ShapeDtypeStruct((bn, POOL_STATE, pool_w), F32),
        ],
        scratch_shapes=[pltpu.VMEM((SUBLANES + rows, conv_w), F32), pltpu.VMEM((2 * SUBLANES + rows, pool_w), F32)],
        compiler_params=pltpu.CompilerParams(dimension_semantics=("arbitrary", "arbitrary"), vmem_limit_bytes=VMEM_LIMIT),
        name="mixer",
    )(x, conv_prev, pool_prev, mk, mv, lw["g_mix"], lw["w_in"], lw["conv_w"], lw["conv_b"], lw["pool_w"],
      lw["pool_scale"], lw["w_bc"], lw["w_bp"], lw["w_ba"], lw["gate_b"], lw["w_o"])


_STAIR = [(a, b) for a in range(PEER_TOPK) for b in range(PEER_TOPK) if (a + 1) * (b + 1) <= PEER_TOPK]
_STAIR_HEAD = 24
_STAIR_TAIL = [_STAIR[_STAIR_HEAD + 8 * i:_STAIR_HEAD + 8 * (i + 1)] for i in range(4)]


def _top16(v, payload=None):
    n, t = v.shape
    rows = lax.broadcasted_iota(jnp.int32, (n, t), 0)
    out_rows = lax.broadcasted_iota(jnp.int32, (PEER_TOPK, t), 0)
    vals = jnp.zeros((PEER_TOPK, t), F32)
    pay = jnp.zeros((PEER_TOPK, t), jnp.int32)
    for k in range(PEER_TOPK):
        m = jnp.max(v, axis=0, keepdims=True)
        am = jnp.min(jnp.where(v == m, rows, n), axis=0, keepdims=True)
        hit = rows == am
        p = am if payload is None else jnp.sum(jnp.where(hit, payload, 0), axis=0, keepdims=True)
        vals = jnp.where(out_rows == k, m, vals)
        pay = jnp.where(out_rows == k, p, pay)
        v = jnp.where(hit, -jnp.inf, v)
    return vals, pay


def _take_rows(src, idxs):
    t = src.shape[1]
    out_rows = lax.broadcasted_iota(jnp.int32, (SUBLANES, t), 0)
    acc = jnp.broadcast_to(src[idxs[0]:idxs[0] + 1, :], (SUBLANES, t))
    for i in range(1, SUBLANES):
        acc = jnp.where(out_rows == i, src[idxs[i]:idxs[i] + 1, :], acc)
    return acc


def _stair(first, second, combine):
    blocks = [combine(first[0:1, :], second), combine(first[1:2, :], second[0:SUBLANES, :])]
    for blk in _STAIR_TAIL:
        blk = blk + [(0, 0)] * (SUBLANES - len(blk))
        blocks.append(combine(_take_rows(first, [a for a, _ in blk]), _take_rows(second, [b for _, b in blk])))
    return jnp.concatenate(blocks, axis=0)


def _route_body(x2_ref, gffn_ref, wq_ref, keys_ref, xn_ref, ids_ref, gate_ref, *, n_keys):
    xn = _rms(x2_ref[...], gffn_ref[...])
    xn_ref[...] = xn
    q = _dot(xn.astype(BF16), wq_ref[...])
    half = keys_ref.shape[-1]
    t = q.shape[0]
    for h in range(PEER_HEADS):
        scores = [_dot_nt(keys_ref[p], q[:, (2 * h + p) * half:(2 * h + p + 1) * half].astype(BF16)) for p in range(2)]
        for l0 in range(0, t, LANES):
            (s1, i1), (s2, i2) = [_top16(sc[:, l0:l0 + LANES]) for sc in scores]
            cand = _stair(s1, s2, lambda a, b: a + b)
            pad = lax.broadcasted_iota(jnp.int32, cand.shape, 0) >= len(_STAIR)
            expert = _stair(i1, i2, lambda a, b: a * n_keys + b)
            sc, e = _top16(jnp.where(pad, -jnp.inf, cand), expert)
            ex = jnp.exp(sc - sc[0:1, :])
            ids_ref[h * PEER_TOPK:(h + 1) * PEER_TOPK, l0:l0 + LANES] = e
            gate_ref[h * PEER_TOPK:(h + 1) * PEER_TOPK, l0:l0 + LANES] = ex / jnp.sum(ex, axis=0, keepdims=True)


def _route(x2, g_ffn, wq, keys):
    t, d = x2.shape
    rows = ROUTE_ROWS
    assert t % rows == 0
    picks = PEER_HEADS * PEER_TOPK
    return pl.pallas_call(
        functools.partial(_route_body, n_keys=keys.shape[1]),
        grid=(t // rows,),
        in_specs=[pl.BlockSpec((rows, d), lambda i: (i, 0)), _resident((1, d)), _resident(wq.shape), _resident(keys.shape)],
        out_specs=[pl.BlockSpec((rows, d), lambda i: (i, 0)), pl.BlockSpec((picks, rows), lambda i: (0, i)),
                   pl.BlockSpec((picks, rows), lambda i: (0, i))],
        out_shape=[jax.ShapeDtypeStruct((t, d), F32), jax.ShapeDtypeStruct((picks, t), jnp.int32),
                   jax.ShapeDtypeStruct((picks, t), F32)],
        compiler_params=pltpu.CompilerParams(dimension_semantics=("arbitrary",), vmem_limit_bytes=VMEM_LIMIT),
        name="route",
    )(x2, g_ffn, wq, keys)


def _gelu(h):
    return 0.5 * h * (1.0 + lax.erf(h * (2.0 ** -0.5)))


def _experts_body(ids_hbm, gate_ref, xn_ref, x2_ref, gfin_ref, uv_hbm, after_hbm, y_ref, *scratch,
                  rows, picks, first_step, steps, final_norm):
    del after_hbm
    bufs, (sem, ids_smem, ids_sem) = scratch[:GATHER_SLOTS], scratch[GATHER_SLOTS:]
    i = pl.program_id(0)
    half = i % 2
    n = rows * picks
    d = xn_ref.shape[-1]
    lead = GATHER_SLOTS - 1
    groups = picks // SUBLANES

    def ids_copy(step, which):
        return pltpu.make_async_copy(ids_hbm.at[first_step + step], ids_smem.at[pl.ds(which * n, n)], ids_sem.at[which])

    def issue(off, slot, j0, j1):
        for j in range(j0, j1):
            pltpu.async_copy(uv_hbm.at[ids_smem[off + j]], bufs[slot].at[pl.ds(j, 1)], sem.at[slot], priority=j % 2)

    def wait_slot(slot):
        pltpu.make_async_copy(bufs[slot], bufs[slot], sem.at[slot]).wait()

    @pl.when(i == 0)
    def _():
        ids_copy(0, 0).start()
        ids_copy(0, 0).wait()
        if steps > 1:
            ids_copy(1, 1).start()
        for s in range(lead):
            issue(s * picks, s, 0, picks)

    lane = lax.broadcasted_iota(jnp.int32, (picks, rows), 1)

    def token(t, slot, ahead_off):
        ahead_slot = (slot + lead) % GATHER_SLOTS
        buf = bufs[slot]
        wait_slot(slot)
        if ahead_off is not None:
            issue(ahead_off, ahead_slot, 0, picks)
        x = xn_ref[pl.ds(t, 1), :]
        g = jnp.sum(jnp.where(lane == t, gate_ref[...], 0.0), axis=-1, keepdims=True)
        hs = []
        for k in range(groups):
            r0 = k * SUBLANES
            u_rows = lax.bitcast_convert_type(buf[r0:r0 + SUBLANES, :] & HI_MASK, F32)
            hs.append(jnp.sum(u_rows * x, axis=-1, keepdims=True))
        w = g * _gelu(jnp.concatenate(hs, axis=0))
        acc = None
        for k in range(groups):
            r0 = k * SUBLANES
            v_rows = lax.bitcast_convert_type(buf[r0:r0 + SUBLANES, :] << 16, F32)
            part = w[r0:r0 + SUBLANES, :] * v_rows
            acc = part if acc is None else acc + part
        y = x2_ref[pl.ds(t, 1), :] + jnp.sum(acc, axis=0, keepdims=True)
        y_ref[pl.ds(t, 1), :] = _rms(y, gfin_ref[...]) if final_norm else y

    def four_tokens(k, carry):
        for s in range(GATHER_SLOTS):
            t = k * GATHER_SLOTS + s
            token(t, s, half * n + (t + lead) * picks)
        return carry

    lax.fori_loop(0, rows // GATHER_SLOTS - 1, four_tokens, 0)

    t0 = rows - GATHER_SLOTS
    token(t0, 0, half * n + (t0 + lead) * picks)

    @pl.when(i + 1 < steps)
    def _():
        ids_copy(i + 1, 1 - half).wait()

    for s in range(1, GATHER_SLOTS):
        @pl.when(i + 1 < steps)
        def _():
            issue((1 - half) * n + (s - 1) * picks, (s + lead) % GATHER_SLOTS, 0, picks)

        token(t0 + s, s, None)

    @pl.when(i + 2 < steps)
    def _():
        ids_copy(i + 2, half).start()


def _experts(ids_tok, gate_t, xn, x2, g_final, uv, final_norm, first_step, steps, after):
    t, picks = ids_tok.shape
    d = xn.shape[1]
    rows = EXPERT_ROWS
    assert t % rows == 0 and rows >= 2 * GATHER_SLOTS and rows % GATHER_SLOTS == 0 and picks % (2 * SUBLANES) == 0
    ids = ids_tok.reshape(t // rows, rows * picks)
    body = functools.partial(_experts_body, rows=rows, picks=picks, first_step=first_step, steps=steps,
                             final_norm=final_norm)
    return pl.pallas_call(
        body,
        grid=(steps,),
        in_specs=[
            pl.BlockSpec(memory_space=pl.ANY),
            pl.BlockSpec((picks, rows), lambda i: (0, first_step + i)),
            pl.BlockSpec((rows, d), lambda i: (first_step + i, 0)),
            pl.BlockSpec((rows, d), lambda i: (first_step + i, 0)),
            _resident((1, d)),
            pl.BlockSpec(memory_space=pl.ANY),
            pl.BlockSpec(memory_space=pl.ANY),
        ],
        out_specs=pl.BlockSpec((rows, d), lambda i: (i, 0)),
        out_shape=jax.ShapeDtypeStruct((steps * rows, d), F32),
        scratch_shapes=[pltpu.VMEM((picks, d), jnp.int32)] * GATHER_SLOTS + [
            pltpu.SemaphoreType.DMA((GATHER_SLOTS,)),
            pltpu.SMEM((2 * rows * picks,), jnp.int32),
            pltpu.SemaphoreType.DMA((2,)),
        ],
        compiler_params=pltpu.CompilerParams(dimension_semantics=("arbitrary",), vmem_limit_bytes=VMEM_LIMIT),
        name="experts",
    )(ids, gate_t, xn, x2, g_final, uv, after)


def _sc_token_pipeline(per_w, nch, wid, ids_hbm, vec_hbm, tab_hbm, out_hbm, idx_v, vec_v, rows_v, out_v,
                       gsem, isem, vsem, osem, compute):
    base = wid * per_w

    def ids_copy(t, p):
        return pltpu.make_async_copy(ids_hbm.at[t], idx_v.at[p], isem.at[p])

    def vec_copy(t, p):
        return pltpu.make_async_copy(vec_hbm.at[t], vec_v.at[p], vsem.at[p])

    def out_copy(t, p):
        return pltpu.make_async_copy(out_v.at[p], out_hbm.at[t], osem.at[p])

    def gather(p, c, b):
        return pltpu.make_async_copy(tab_hbm.at[idx_v.at[p, c]], rows_v.at[b], gsem.at[b])

    ids_copy(base, 0).start()
    vec_copy(base, 0).start()
    ids_copy(base, 0).wait()
    vec_copy(base, 0).wait()
    gather(0, 0, 0).start()

    def pair(ii, carry):
        for p in range(2):
            i = 2 * ii + p
            t = base + i
            has_next = i + 1 < per_w

            @pl.when(has_next)
            def _():
                ids_copy(t + 1, 1 - p).start()
                vec_copy(t + 1, 1 - p).start()

            @pl.when(i >= 2)
            def _():
                out_copy(t - 2, p).wait()

            for c in range(nch):
                b = c % 2
                if c + 1 < nch:
                    gather(p, c + 1, 1 - b).start()
                else:
                    @pl.when(has_next)
                    def _():
                        ids_copy(t + 1, 1 - p).wait()
                        vec_copy(t + 1, 1 - p).wait()
                        gather(1 - p, 0, 1 - b).start()
                gather(p, c, b).wait()
                compute(p, c, b)
            out_copy(t, p).start()
        return carry

    lax.fori_loop(0, per_w // 2, pair, 0)
    out_copy(base + per_w - 2, 0).wait()
    out_copy(base + per_w - 1, 1).wait()


HI_MASK = -65536


def _pack_pairs(tab):
    d = tab.shape[1]
    return _pack_halves(tab[:, :d // 2], tab[:, d // 2:])


def _pack_halves(hi, lo):
    bits = lambda a: lax.bitcast_convert_type(a.astype(jnp.bfloat16), jnp.uint16).astype(jnp.uint32)
    return lax.bitcast_convert_type((bits(hi) << 16) | bits(lo), jnp.int32)


def _unpack(words):
    hi = lax.bitcast_convert_type(words & HI_MASK, F32)
    lo = lax.bitcast_convert_type(words << 16, F32)
    return hi, lo


def _sc_dots(ids, xn, u2, n_tok):
    d = xn.shape[1]
    dw = u2.shape[1]
    nch = ids.shape[1]
    picks = nch * SC_CHUNK
    per_w = n_tok // SC_WORKERS
    assert per_w % 2 == 0 and nch % 2 == 0 and 2 * dw == d
    mesh = plsc.VectorSubcoreMesh(core_axis_name="c", subcore_axis_name="s")

    @functools.partial(
        pl.kernel, mesh=mesh, out_type=jax.ShapeDtypeStruct((n_tok, picks), F32),
        scratch_types=[pltpu.VMEM((2, nch, SC_CHUNK), jnp.int32), pltpu.VMEM((2, d), F32),
                       pltpu.VMEM((2, SC_CHUNK, dw), jnp.int32), pltpu.VMEM((2, picks), F32)]
        + [pltpu.SemaphoreType.DMA((2,))] * 4,
        compiler_params=pltpu.CompilerParams(needs_layout_passes=False), name="sc_dots")
    def k(ids_hbm, xn_hbm, u_hbm, h_hbm, idx_v, x_v, rows_v, h_v, gsem, isem, vsem, osem):
        wid = lax.axis_index("s") * SC_CORES + lax.axis_index("c")
        lane = lax.iota(jnp.int32, SC_LANES)

        def compute(p, c, b):
            for g in range(SC_CHUNK // SC_LANES):
                def rows4(q, hv):
                    j0 = g * SC_LANES + q * 4

                    def span(cc, accs):
                        for uu in range(SC_UNROLL):
                            off = pl.multiple_of((cc * SC_UNROLL + uu) * SC_LANES, SC_LANES)
                            x_hi = x_v[p, pl.ds(off, SC_LANES)]
                            x_lo = x_v[p, pl.ds(dw + off, SC_LANES)]
                            nxt = []
                            for kk in range(4):
                                hi, lo = _unpack(rows_v[b, j0 + kk, pl.ds(off, SC_LANES)])
                                nxt.append((accs[2 * kk] + hi * x_hi, accs[2 * kk + 1] + lo * x_lo))
                            accs = tuple(a for pair in nxt for a in pair)
                        return accs

                    accs = lax.fori_loop(0, dw // (SC_LANES * SC_UNROLL), span,
                                         tuple(jnp.zeros((SC_LANES,), F32) for _ in range(8)))
                    for kk in range(4):
                        hv = jnp.where(lane == q * 4 + kk, jnp.sum(accs[2 * kk] + accs[2 * kk + 1]), hv)
                    return hv

                hv = lax.fori_loop(0, SC_LANES // 4, rows4, jnp.zeros((SC_LANES,), F32))
                h_v[p, pl.ds(c * SC_CHUNK + g * SC_LANES, SC_LANES)] = hv

        _sc_token_pipeline(per_w, nch, wid, ids_hbm, xn_hbm, u_hbm, h_hbm, idx_v, x_v, rows_v, h_v,
                           gsem, isem, vsem, osem, compute)

    return k(ids, xn, u2)


def _sc_mix(ids, w, v2, n_tok):
    dw = v2.shape[1]
    d = 2 * dw
    nch = ids.shape[1]
    picks = nch * SC_CHUNK
    per_w = n_tok // SC_WORKERS
    assert per_w % 2 == 0 and nch % 2 == 0
    cb = 8
    mesh = plsc.VectorSubcoreMesh(core_axis_name="c", subcore_axis_name="s")

    @functools.partial(
        pl.kernel, mesh=mesh, out_type=jax.ShapeDtypeStruct((n_tok, d), F32),
        scratch_types=[pltpu.VMEM((2, nch, SC_CHUNK), jnp.int32), pltpu.VMEM((2, picks), F32),
                       pltpu.VMEM((2, SC_CHUNK, dw), jnp.int32), pltpu.VMEM((2, d), F32)]
        + [pltpu.SemaphoreType.DMA((2,))] * 4,
        compiler_params=pltpu.CompilerParams(needs_layout_passes=False), name="sc_mix")
    def k(ids_hbm, w_hbm, v_hbm, o_hbm, idx_v, w_v, rows_v, o_v, gsem, isem, vsem, osem):
        wid = lax.axis_index("s") * SC_CORES + lax.axis_index("c")

        def compute(p, c, b):
            for blk in range(dw // (cb * SC_LANES)):
                base = blk * cb * SC_LANES
                offs = [base + kk * SC_LANES for kk in range(cb)] + [dw + base + kk * SC_LANES for kk in range(cb)]
                if c == 0:
                    init = tuple(jnp.zeros((SC_LANES,), F32) for _ in offs)
                else:
                    init = tuple(o_v[p, pl.ds(o, SC_LANES)] for o in offs)

                def row(r, accs):
                    wj = plsc.load_gather(w_v.at[p], [jnp.full((SC_LANES,), c * SC_CHUNK, jnp.int32) + r])
                    his, los = [], []
                    for kk in range(cb):
                        hi, lo = _unpack(rows_v[b, r, pl.ds(base + kk * SC_LANES, SC_LANES)])
                        his.append(accs[kk] + wj * hi)
                        los.append(accs[cb + kk] + wj * lo)
                    return tuple(his + los)

                accs = lax.fori_loop(0, SC_CHUNK, row, init)
                for o, a in zip(offs, accs):
                    o_v[p, pl.ds(o, SC_LANES)] = a

        _sc_token_pipeline(per_w, nch, wid, ids_hbm, w_hbm, v_hbm, o_hbm, idx_v, w_v, rows_v, o_v,
                           gsem, isem, vsem, osem, compute)

    return k(ids, w, v2)


def _gate_gelu_body(h_ref, gate_ref, after_hbm, w_ref):
    del after_hbm
    w_ref[...] = gate_ref[...] * _gelu(h_ref[...])


def _gate_gelu(h, gate_tok, after):
    n, picks = h.shape
    rows = math.gcd(n, 1024)
    assert n % rows == 0
    return pl.pallas_call(
        _gate_gelu_body, grid=(n // rows,),
        in_specs=[pl.BlockSpec((rows, picks), lambda i: (i, 0)), pl.BlockSpec((rows, picks), lambda i: (i, 0)),
                  pl.BlockSpec(memory_space=pl.ANY)],
        out_specs=pl.BlockSpec((rows, picks), lambda i: (i, 0)),
        out_shape=jax.ShapeDtypeStruct((n, picks), F32), name="gate_gelu",
    )(h, gate_tok, after)


def _finish_body(x2_ref, o_ref, gfin_ref, y_ref, *, final_norm):
    y = x2_ref[...] + o_ref[...]
    y_ref[...] = _rms(y, gfin_ref[...]) if final_norm else y


def _finish(x2, o, g_final, final_norm):
    n, d = o.shape
    rows = math.gcd(n, 512)
    assert n % rows == 0
    return pl.pallas_call(
        functools.partial(_finish_body, final_norm=final_norm), grid=(n // rows,),
        in_specs=[pl.BlockSpec((rows, d), lambda i: (i, 0)), pl.BlockSpec((rows, d), lambda i: (i, 0)), _resident((1, d))],
        out_specs=pl.BlockSpec((rows, d), lambda i: (i, 0)),
        out_shape=jax.ShapeDtypeStruct((n, d), F32), name="finish",
    )(x2, o, g_final)


def _peer(ids_t, gate_t, xn, x2, g_final, lw, final_norm):
    picks, t = ids_t.shape
    ids_tok = ids_t.T
    total_steps = t // EXPERT_ROWS
    sc_steps = int(total_steps * SC_SHARE) if t >= SC_MIN_TOKENS else 0
    n_sc = sc_steps * EXPERT_ROWS
    if n_sc == 0:
        return [_experts(ids_tok, gate_t, xn, x2, g_final, lw["peer_uv"], final_norm, 0, total_steps, g_final)]
    assert n_sc % SC_WORKERS == 0 and picks % SC_CHUNK == 0
    tc_steps = total_steps - sc_steps
    first = int(tc_steps * TC_FIRST_SHARE)
    ids_sc = ids_tok.reshape(t, picks // SC_CHUNK, SC_CHUNK)
    h = _sc_dots(ids_sc, xn, lw["peer_u2"], n_sc)
    y_tc1 = _experts(ids_tok, gate_t, xn, x2, g_final, lw["peer_uv"], final_norm, sc_steps, first, g_final)
    w = _gate_gelu(h, gate_t.T, y_tc1)
    o = _sc_mix(ids_sc, w, lw["peer_v2"], n_sc)
    y_tc2 = _experts(ids_tok, gate_t, xn, x2, g_final, lw["peer_uv"], final_norm, sc_steps + first, tc_steps - first, w)
    y_sc = _finish(x2, o, g_final, final_norm)
    return [y_sc, y_tc1, y_tc2]


def _layer(x, conv_prev, pool_prev, start_pos, mk, mv, lw, g_final, final_norm):
    bn, seq, d = x.shape
    x2, new_conv, new_pool = _mixer(x, conv_prev, pool_prev, start_pos, mk, mv, lw)
    x2 = x2.reshape(bn * seq, d)
    xn, ids_t, gate_t = _route(x2, lw["g_ffn"], lw["peer_wq"], lw["peer_keys"])
    y = _peer(ids_t, gate_t, xn, x2, g_final, lw, final_norm)
    return y, new_conv, new_pool


def kernel(x_prompt, x_sample, mem_prompt, cache_conv, cache_pool, cache_mem_k, cache_mem_v, g_mix, w_in, conv_w, conv_b, pool_w, pool_scale, g_mem, w_mk, w_mv, w_bc, w_bp, w_ba, gate_b, w_o, g_ffn, peer_wq, peer_keys, peer_u, peer_v, g_final):
    depth, d = g_mix.shape
    xp, xs = x_prompt, x_sample
    bp, bs = xp.shape[0], xs.shape[0]
    n_exp = peer_u.shape[1]
    gfin = g_final.reshape(1, d)
    conv_p, pool_p, mk_p, mv_p, conv_s, pool_s = [], [], [], [], [], []
    for l in range(depth):
        lw = dict(
            g_mix=g_mix[l].reshape(1, d), w_in=w_in[l].astype(BF16), conv_w=conv_w[l], conv_b=conv_b[l].reshape(1, -1),
            pool_w=pool_w[l].astype(BF16), pool_scale=pool_scale[l].reshape(1, -1), w_bc=w_bc[l].astype(BF16),
            w_bp=w_bp[l].astype(BF16), w_ba=w_ba[l].astype(BF16), gate_b=gate_b[l].reshape(1, -1), w_o=w_o[l].astype(BF16),
            g_ffn=g_ffn[l].reshape(1, d), peer_wq=peer_wq[l].astype(BF16), peer_keys=peer_keys[l].astype(BF16),
            peer_uv=_pack_halves(peer_u[l], peer_v[l]).reshape(n_exp, 1, d),
            peer_u2=_pack_pairs(peer_u[l]), peer_v2=_pack_pairs(peer_v[l]),
        )
        last = l == depth - 1
        mk, mv = _mem_kv(mem_prompt, g_mem[l], w_mk[l], w_mv[l])
        zc = jnp.zeros((bp, CONV_K - 1, conv_w.shape[-1]), xp.dtype)
        zp = jnp.zeros((bp, POOL_STATE, pool_scale.shape[-1]), xp.dtype)
        seq = xp.shape[1]
        seg = seq // PROMPT_SEGMENTS if seq % PROMPT_SEGMENTS == 0 and seq // PROMPT_SEGMENTS >= SC_MIN_TOKENS else seq
        pieces, cps, pps = [], [], []
        for b in range(bp):
            cprev, pprev = zc[b:b + 1], zp[b:b + 1]
            for s0 in range(0, seq, seg):
                ys, cprev, pprev = _layer(xp[b:b + 1, s0:s0 + seg], cprev, pprev, s0, mk[b:b + 1], mv[b:b + 1], lw, gfin, last)
                pieces += ys
            cps.append(cprev); pps.append(pprev)
        xp = jnp.concatenate(pieces, axis=0).reshape(bp, seq, d)
        cp, pp = jnp.concatenate(cps, axis=0), jnp.concatenate(pps, axis=0)
        n_mem = cache_mem_k.shape[2]
        ys, cs, ps = _layer(xs, cache_conv[l], cache_pool[l], PAST_LEN, cache_mem_k[l].reshape(bs, n_mem, -1),
                            cache_mem_v[l].reshape(bs, n_mem, -1), lw, gfin, last)
        xs = jnp.concatenate(ys, axis=0).reshape(xs.shape)
        heads_shape = (bp, n_mem) + cache_mem_k.shape[3:]
        conv_p.append(cp); pool_p.append(pp); mk_p.append(mk.reshape(heads_shape)); mv_p.append(mv.reshape(heads_shape))
        conv_s.append(cs); pool_s.append(ps)
    return (xp, xs, jnp.stack(conv_p), jnp.stack(pool_p), jnp.stack(mk_p), jnp.stack(mv_p),
            jnp.stack(conv_s), jnp.stack(pool_s))
```

```python
import functools
import math

import jax
import jax.numpy as jnp
from jax import lax
from jax.experimental import pallas as pl
from jax.experimental.pallas import tpu as pltpu
from jax.experimental.pallas import tpu_sc as plsc

F32 = jnp.float32
BF16 = jnp.bfloat16
EPS = 1e-6

CONV_K = 3
POOL_WINDOWS = (2, 4, 8, 16)
POOL_STATE = max(POOL_WINDOWS) - 1
PAST_LEN = 1024
MEM_HEADS = 4
PEER_HEADS = 8
PEER_TOPK = 16

SUBLANES = 8
LANES = 128
MIXER_ROWS = 256
ROUTE_ROWS = 256
EXPERT_ROWS = 128
GATHER_SLOTS = 4
VMEM_LIMIT = 56 * 1024 * 1024

SC_CORES = 2
SC_SUBCORES = 16
SC_WORKERS = SC_CORES * SC_SUBCORES
SC_LANES = 16
SC_CHUNK = 64
SC_UNROLL = 8
SC_SHARE = 0.68
SC_MIN_TOKENS = 8192
PROMPT_SEGMENTS = 2
TC_FIRST_SHARE = 0.63


def _rms(x, g):
    return x * lax.rsqrt(jnp.mean(x * x, axis=-1, keepdims=True) + EPS) * g


def _dot(a, b):
    return jnp.dot(a, b, preferred_element_type=F32)


def _dot_nt(a, b):
    return lax.dot_general(a, b, (((1,), (1,)), ((), ())), preferred_element_type=F32)


def _resident(shape):
    zeros = (0,) * len(shape)
    return pl.BlockSpec(shape, lambda *_: zeros, pipeline_mode=pl.Buffered(1))


def _memkv_body(mem_ref, g_ref, wk_ref, wv_ref, k_ref, v_ref):
    m = _rms(mem_ref[...], g_ref[...]).astype(BF16)
    k_ref[...] = _dot(m, wk_ref[...])
    v_ref[...] = _dot(m, wv_ref[...])


def _mem_kv(mem, g_mem, w_mk, w_mv):
    bn, n_mem, d = mem.shape
    att_w = w_mk.shape[1]
    k, v = pl.pallas_call(
        _memkv_body,
        out_shape=[jax.ShapeDtypeStruct((bn * n_mem, att_w), F32)] * 2,
        name="mem_kv",
    )(mem.reshape(bn * n_mem, d), g_mem.reshape(1, d), w_mk.astype(BF16), w_mv.astype(BF16))
    return k.reshape(bn, n_mem, att_w), v.reshape(bn, n_mem, att_w)


def _mixer_body(x_ref, cprev_ref, pprev_ref, mk_ref, mv_ref, gmix_ref, win_ref, convw_ref, convb_ref, poolw_ref,
                pscale_ref, wbc_ref, wbp_ref, wba_ref, gateb_ref, wo_ref,
                x2_ref, nconv_ref, npool_ref, zbuf, ubuf, *, rows, start_pos, conv_w, pool_w, att_w):
    s = pl.program_id(1)
    d_model = x_ref.shape[-1]
    z0 = SUBLANES
    u0 = 2 * SUBLANES

    @pl.when(s == 0)
    def _():
        zbuf[z0 - (CONV_K - 1):z0, :] = cprev_ref[0]
        ubuf[u0 - POOL_STATE:u0, :] = pprev_ref[0]

    x = x_ref[0]
    hb = _rms(x, gmix_ref[...]).astype(BF16)

    c0 = 0
    pa = _dot(hb, win_ref[:, c0:c0 + 3 * conv_w])
    xc, bg, cg = pa[:, :conv_w], pa[:, conv_w:2 * conv_w], pa[:, 2 * conv_w:]
    zbuf[z0:z0 + rows, :] = cg * xc
    cw = convw_ref[...]
    conv = zbuf[z0 - 2:z0 - 2 + rows, :] * cw[0:1]
    for k in range(1, CONV_K):
        conv = conv + zbuf[z0 - 2 + k:z0 - 2 + k + rows, :] * cw[k:k + 1]
    conv = conv + convb_ref[...]
    ya = _dot((bg * conv).astype(BF16), wbc_ref[...])
    last_z = zbuf[z0 + rows - (CONV_K - 1):z0 + rows, :]
    nconv_ref[0] = last_z
    zbuf[z0 - (CONV_K - 1):z0, :] = last_z
    c0 += 3 * conv_w

    up = _dot(hb, win_ref[:, c0:c0 + pool_w])
    ubuf[u0:u0 + rows, :] = up
    pos = start_pos + s * rows + lax.broadcasted_iota(jnp.int32, (rows, 1), 0)
    gw = pool_w // len(POOL_WINDOWS)
    ys = []
    for g, w in enumerate(POOL_WINDOWS):
        cur = up[:, g * gw:(g + 1) * gw]
        acc = cur
        for k in range(1, w):
            acc = acc + ubuf[u0 - k:u0 - k + rows, g * gw:(g + 1) * gw]
        cnt = jnp.minimum(pos + 1, w).astype(F32)
        ys.append(_dot((acc / cnt - cur).astype(BF16), poolw_ref[g]))
    yb = _dot((jnp.concatenate(ys, axis=-1) * pscale_ref[...]).astype(BF16), wbp_ref[...])
    last_u = ubuf[u0 + rows - POOL_STATE:u0 + rows, :]
    npool_ref[0] = last_u
    ubuf[u0 - POOL_STATE:u0, :] = last_u
    c0 += pool_w

    q = _dot(hb, win_ref[:, c0:c0 + att_w])
    kb = mk_ref[0].astype(BF16)
    vb = mv_ref[0].astype(BF16)
    hd = att_w // MEM_HEADS
    heads = []
    for h in range(MEM_HEADS):
        sc = _dot_nt(q[:, h * hd:(h + 1) * hd].astype(BF16), kb[:, h * hd:(h + 1) * hd]) * (hd ** -0.5)
        e = jnp.exp(sc - jnp.max(sc, axis=-1, keepdims=True))
        p = e / jnp.sum(e, axis=-1, keepdims=True)
        heads.append(_dot(p.astype(BF16), vb[:, h * hd:(h + 1) * hd]))
    yc = _dot(jnp.concatenate(heads, axis=-1).astype(BF16), wba_ref[...])
    c0 += att_w

    merged = None
    for i, y in enumerate((ya, yb, yc)):
        gl = _dot(hb, win_ref[:, c0 + i * d_model:c0 + (i + 1) * d_model]) + gateb_ref[:, i * d_model:(i + 1) * d_model]
        term = (1.0 / (1.0 + jnp.exp(-gl))) * y
        merged = term if merged is None else merged + term
    x2_ref[0] = x + _dot(merged.astype(BF16), wo_ref[...])


def _mixer(x, window, conv_prev, pool_prev, start_pos, mk, mv, lw):
    b0, bn, s0, seq = window
    d = x.shape[-1]
    conv_w, pool_w = conv_prev.shape[-1], pool_prev.shape[-1]
    n_mem, att_w = mk.shape[1], mk.shape[2]
    rows = min(MIXER_ROWS, seq)
    assert seq % rows == 0 and s0 % rows == 0 and rows % SUBLANES == 0 and rows >= POOL_STATE
    row0 = s0 // rows
    in_cols = lw["w_in"].shape[1]
    per_b = lambda shape: pl.BlockSpec((1,) + shape, lambda b, s: (b, 0, 0))
    body = functools.partial(_mixer_body, rows=rows, start_pos=start_pos, conv_w=conv_w, pool_w=pool_w, att_w=att_w)
    return pl.pallas_call(
        body,
        grid=(bn, seq // rows),
        in_specs=[
            pl.BlockSpec((1, rows, d), lambda b, s: (b0 + b, row0 + s, 0)),
            per_b((CONV_K - 1, conv_w)), per_b((POOL_STATE, pool_w)), per_b((n_mem, att_w)), per_b((n_mem, att_w)),
            _resident((1, d)), _resident((d, in_cols)), _resident((CONV_K, conv_w)), _resident((1, conv_w)),
            _resident(lw["pool_w"].shape), _resident((1, pool_w)), _resident((conv_w, d)), _resident((pool_w, d)),
            _resident((att_w, d)), _resident((1, 3 * d)), _resident((d, d)),
        ],
        out_specs=[
            pl.BlockSpec((1, rows, d), lambda b, s: (b, s, 0)),
            per_b((CONV_K - 1, conv_w)), per_b((POOL_STATE, pool_w)),
        ],
        out_shape=[
            jax.ShapeDtypeStruct((bn, seq, d), F32),
            jax.ShapeDtypeStruct((bn, CONV_K - 1, conv_w), F32),
            jax.ShapeDtypeStruct((bn, POOL_STATE, pool_w), F32),
        ],
        scratch_shapes=[pltpu.VMEM((SUBLANES + rows, conv_w), F32), pltpu.VMEM((2 * SUBLANES + rows, pool_w), F32)],
        compiler_params=pltpu.CompilerParams(dimension_semantics=("arbitrary", "arbitrary"), vmem_limit_bytes=VMEM_LIMIT),
        name="mixer",
    )(x, conv_prev, pool_prev, mk, mv, lw["g_mix"], lw["w_in"], lw["conv_w"], lw["conv_b"], lw["pool_w"],
      lw["pool_scale"], lw["w_bc"], lw["w_bp"], lw["w_ba"], lw["gate_b"], lw["w_o"])


_STAIR = [(a, b) for a in range(PEER_TOPK) for b in range(PEER_TOPK) if (a + 1) * (b + 1) <= PEER_TOPK]
_STAIR_HEAD = 24
_STAIR_TAIL = [_STAIR[_STAIR_HEAD + 8 * i:_STAIR_HEAD + 8 * (i + 1)] for i in range(4)]


def _top16(v, payload=None):
    n, t = v.shape
    rows = lax.broadcasted_iota(jnp.int32, (n, t), 0)
    out_rows = lax.broadcasted_iota(jnp.int32, (PEER_TOPK, t), 0)
    vals = jnp.zeros((PEER_TOPK, t), F32)
    pay = jnp.zeros((PEER_TOPK, t), jnp.int32)
    for k in range(PEER_TOPK):
        m = jnp.max(v, axis=0, keepdims=True)
        am = jnp.min(jnp.where(v == m, rows, n), axis=0, keepdims=True)
        hit = rows == am
        p = am if payload is None else jnp.sum(jnp.where(hit, payload, 0), axis=0, keepdims=True)
        vals = jnp.where(out_rows == k, m, vals)
        pay = jnp.where(out_rows == k, p, pay)
        v = jnp.where(hit, -jnp.inf, v)
    return vals, pay


def _take_rows(src, idxs):
    t = src.shape[1]
    out_rows = lax.broadcasted_iota(jnp.int32, (SUBLANES, t), 0)
    acc = jnp.broadcast_to(src[idxs[0]:idxs[0] + 1, :], (SUBLANES, t))
    for i in range(1, SUBLANES):
        acc = jnp.where(out_rows == i, src[idxs[i]:idxs[i] + 1, :], acc)
    return acc


def _stair(first, second, combine):
    blocks = [combine(first[0:1, :], second), combine(first[1:2, :], second[0:SUBLANES, :])]
    for blk in _STAIR_TAIL:
        blk = blk + [(0, 0)] * (SUBLANES - len(blk))
        blocks.append(combine(_take_rows(first, [a for a, _ in blk]), _take_rows(second, [b for _, b in blk])))
    return jnp.concatenate(blocks, axis=0)


def _route_body(x2_ref, gffn_ref, wq_ref, keys_ref, xn_ref, ids_ref, gate_ref, *, n_keys):
    xn = _rms(x2_ref[...], gffn_ref[...])
    xn_ref[...] = xn
    q = _dot(xn.astype(BF16), wq_ref[...])
    half = keys_ref.shape[-1]
    t = q.shape[0]
    for h in range(PEER_HEADS):
        scores = [_dot_nt(keys_ref[p], q[:, (2 * h + p) * half:(2 * h + p + 1) * half].astype(BF16)) for p in range(2)]
        for l0 in range(0, t, LANES):
            (s1, i1), (s2, i2) = [_top16(sc[:, l0:l0 + LANES]) for sc in scores]
            cand = _stair(s1, s2, lambda a, b: a + b)
            pad = lax.broadcasted_iota(jnp.int32, cand.shape, 0) >= len(_STAIR)
            expert = _stair(i1, i2, lambda a, b: a * n_keys + b)
            sc, e = _top16(jnp.where(pad, -jnp.inf, cand), expert)
            ex = jnp.exp(sc - sc[0:1, :])
            ids_ref[h * PEER_TOPK:(h + 1) * PEER_TOPK, l0:l0 + LANES] = e
            gate_ref[h * PEER_TOPK:(h + 1) * PEER_TOPK, l0:l0 + LANES] = ex / jnp.sum(ex, axis=0, keepdims=True)


def _route(x2, g_ffn, wq, keys):
    t, d = x2.shape
    rows = ROUTE_ROWS
    assert t % rows == 0
    picks = PEER_HEADS * PEER_TOPK
    return pl.pallas_call(
        functools.partial(_route_body, n_keys=keys.shape[1]),
        grid=(t // rows,),
        in_specs=[pl.BlockSpec((rows, d), lambda i: (i, 0)), _resident((1, d)), _resident(wq.shape), _resident(keys.shape)],
        out_specs=[pl.BlockSpec((rows, d), lambda i: (i, 0)), pl.BlockSpec((picks, rows), lambda i: (0, i)),
                   pl.BlockSpec((picks, rows), lambda i: (0, i))],
        out_shape=[jax.ShapeDtypeStruct((t, d), F32), jax.ShapeDtypeStruct((picks, t), jnp.int32),
                   jax.ShapeDtypeStruct((picks, t), F32)],
        compiler_params=pltpu.CompilerParams(dimension_semantics=("arbitrary",), vmem_limit_bytes=VMEM_LIMIT),
        name="route",
    )(x2, g_ffn, wq, keys)


def _gelu(h):
    return 0.5 * h * (1.0 + lax.erf(h * (2.0 ** -0.5)))


def _experts_body(ids_hbm, gate_ref, xn_ref, x2_ref, gfin_ref, uv_hbm, after_hbm, y_ref, *scratch,
                  rows, picks, first_step, steps, final_norm):
    del after_hbm
    bufs, (sem, ids_smem, ids_sem) = scratch[:GATHER_SLOTS], scratch[GATHER_SLOTS:]
    i = pl.program_id(0)
    half = i % 2
    n = rows * picks
    d = xn_ref.shape[-1]
    lead = GATHER_SLOTS - 1
    groups = picks // SUBLANES

    def ids_copy(step, which):
        return pltpu.make_async_copy(ids_hbm.at[first_step + step], ids_smem.at[pl.ds(which * n, n)], ids_sem.at[which])

    def issue(off, slot, j0, j1):
        for j in range(j0, j1):
            pltpu.async_copy(uv_hbm.at[ids_smem[off + j]], bufs[slot].at[pl.ds(j, 1)], sem.at[slot], priority=j % 2)

    def wait_slot(slot):
        pltpu.make_async_copy(bufs[slot], bufs[slot], sem.at[slot]).wait()

    @pl.when(i == 0)
    def _():
        ids_copy(0, 0).start()
        ids_copy(0, 0).wait()
        if steps > 1:
            ids_copy(1, 1).start()
        for s in range(lead):
            issue(s * picks, s, 0, picks)

    lane = lax.broadcasted_iota(jnp.int32, (picks, rows), 1)

    def token(t, slot, ahead_off):
        ahead_slot = (slot + lead) % GATHER_SLOTS
        buf = bufs[slot]
        wait_slot(slot)
        if ahead_off is not None:
            issue(ahead_off, ahead_slot, 0, picks)
        x = xn_ref[pl.ds(t, 1), :]
        g = jnp.sum(jnp.where(lane == t, gate_ref[...], 0.0), axis=-1, keepdims=True)
        hs = []
        for k in range(groups):
            r0 = k * SUBLANES
            u_rows = lax.bitcast_convert_type(buf[r0:r0 + SUBLANES, :] & HI_MASK, F32)
            hs.append(jnp.sum(u_rows * x, axis=-1, keepdims=True))
        w = g * _gelu(jnp.concatenate(hs, axis=0))
        acc = None
        for k in range(groups):
            r0 = k * SUBLANES
            v_rows = lax.bitcast_convert_type(buf[r0:r0 + SUBLANES, :] << 16, F32)
            part = w[r0:r0 + SUBLANES, :] * v_rows
            acc = part if acc is None else acc + part
        y = x2_ref[pl.ds(t, 1), :] + jnp.sum(acc, axis=0, keepdims=True)
        y_ref[pl.ds(t, 1), :] = _rms(y, gfin_ref[...]) if final_norm else y

    def four_tokens(k, carry):
        for s in range(GATHER_SLOTS):
            t = k * GATHER_SLOTS + s
            token(t, s, half * n + (t + lead) * picks)
        return carry

    lax.fori_loop(0, rows // GATHER_SLOTS - 1, four_tokens, 0)

    t0 = rows - GATHER_SLOTS
    token(t0, 0, half * n + (t0 + lead) * picks)

    @pl.when(i + 1 < steps)
    def _():
        ids_copy(i + 1, 1 - half).wait()

    for s in range(1, GATHER_SLOTS):
        @pl.when(i + 1 < steps)
        def _():
            issue((1 - half) * n + (s - 1) * picks, (s + lead) % GATHER_SLOTS, 0, picks)

        token(t0 + s, s, None)

    @pl.when(i + 2 < steps)
    def _():
        ids_copy(i + 2, half).start()


def _experts(ids_tok, gate_t, xn, x2, g_final, uv, final_norm, first_step, steps, after):
    t, picks = ids_tok.shape
    d = xn.shape[1]
    rows = EXPERT_ROWS
    assert t % rows == 0 and rows >= 2 * GATHER_SLOTS and rows % GATHER_SLOTS == 0 and picks % (2 * SUBLANES) == 0
    ids = ids_tok.reshape(t // rows, rows * picks)
    body = functools.partial(_experts_body, rows=rows, picks=picks, first_step=first_step, steps=steps,
                             final_norm=final_norm)
    return pl.pallas_call(
        body,
        grid=(steps,),
        in_specs=[
            pl.BlockSpec(memory_space=pl.ANY),
            pl.BlockSpec((picks, rows), lambda i: (0, first_step + i)),
            pl.BlockSpec((rows, d), lambda i: (first_step + i, 0)),
            pl.BlockSpec((rows, d), lambda i: (first_step + i, 0)),
            _resident((1, d)),
            pl.BlockSpec(memory_space=pl.ANY),
            pl.BlockSpec(memory_space=pl.ANY),
        ],
        out_specs=pl.BlockSpec((rows, d), lambda i: (i, 0)),
        out_shape=jax.ShapeDtypeStruct((steps * rows, d), F32),
        scratch_shapes=[pltpu.VMEM((picks, d), jnp.int32)] * GATHER_SLOTS + [
            pltpu.SemaphoreType.DMA((GATHER_SLOTS,)),
            pltpu.SMEM((2 * rows * picks,), jnp.int32),
            pltpu.SemaphoreType.DMA((2,)),
        ],
        compiler_params=pltpu.CompilerParams(dimension_semantics=("arbitrary",), vmem_limit_bytes=VMEM_LIMIT),
        name="experts",
    )(ids, gate_t, xn, x2, g_final, uv, after)


def _sc_token_pipeline(per_w, nch, wid, ids_hbm, vec_hbm, tab_hbm, out_hbm, idx_v, vec_v, rows_v, out_v,
                       gsem, isem, vsem, osem, compute):
    base = wid * per_w

    def ids_copy(t, p):
        return pltpu.make_async_copy(ids_hbm.at[t], idx_v.at[p], isem.at[p])

    def vec_copy(t, p):
        return pltpu.make_async_copy(vec_hbm.at[t], vec_v.at[p], vsem.at[p])

    def out_copy(t, p):
        return pltpu.make_async_copy(out_v.at[p], out_hbm.at[t], osem.at[p])

    def gather(p, c, b):
        return pltpu.make_async_copy(tab_hbm.at[idx_v.at[p, c]], rows_v.at[b], gsem.at[b])

    ids_copy(base, 0).start()
    vec_copy(base, 0).start()
    ids_copy(base, 0).wait()
    vec_copy(base, 0).wait()
    gather(0, 0, 0).start()

    def pair(ii, carry):
        for p in range(2):
            i = 2 * ii + p
            t = base + i
            has_next = i + 1 < per_w

            @pl.when(has_next)
            def _():
                ids_copy(t + 1, 1 - p).start()
                vec_copy(t + 1, 1 - p).start()

            @pl.when(i >= 2)
            def _():
                out_copy(t - 2, p).wait()

            for c in range(nch):
                b = c % 2
                if c + 1 < nch:
                    gather(p, c + 1, 1 - b).start()
                else:
                    @pl.when(has_next)
                    def _():
                        ids_copy(t + 1, 1 - p).wait()
                        vec_copy(t + 1, 1 - p).wait()
                        gather(1 - p, 0, 1 - b).start()
                gather(p, c, b).wait()
                compute(p, c, b)
            out_copy(t, p).start()
        return carry

    lax.fori_loop(0, per_w // 2, pair, 0)
    out_copy(base + per_w - 2, 0).wait()
    out_copy(base + per_w - 1, 1).wait()


HI_MASK = -65536


def _pack_pairs(tab):
    d = tab.shape[1]
    return _pack_halves(tab[:, :d // 2], tab[:, d // 2:])


def _pack_halves(hi, lo):
    bits = lambda a: lax.bitcast_convert_type(a.astype(jnp.bfloat16), jnp.uint16).astype(jnp.uint32)
    return lax.bitcast_convert_type((bits(hi) << 16) | bits(lo), jnp.int32)


def _unpack(words):
    hi = lax.bitcast_convert_type(words & HI_MASK, F32)
    lo = lax.bitcast_convert_type(words << 16, F32)
    return hi, lo


def _sc_dots(ids, xn, u2, n_tok):
    d = xn.shape[1]
    dw = u2.shape[1]
    nch = ids.shape[1]
    picks = nch * SC_CHUNK
    per_w = n_tok // SC_WORKERS
    assert per_w % 2 == 0 and nch % 2 == 0 and 2 * dw == d
    mesh = plsc.VectorSubcoreMesh(core_axis_name="c", subcore_axis_name="s")

    @functools.partial(
        pl.kernel, mesh=mesh, out_type=jax.ShapeDtypeStruct((n_tok, picks), F32),
        scratch_types=[pltpu.VMEM((2, nch, SC_CHUNK), jnp.int32), pltpu.VMEM((2, d), F32),
                       pltpu.VMEM((2, SC_CHUNK, dw), jnp.int32), pltpu.VMEM((2, picks), F32)]
        + [pltpu.SemaphoreType.DMA((2,))] * 4,
        compiler_params=pltpu.CompilerParams(needs_layout_passes=False), name="sc_dots")
    def k(ids_hbm, xn_hbm, u_hbm, h_hbm, idx_v, x_v, rows_v, h_v, gsem, isem, vsem, osem):
        wid = lax.axis_index("s") * SC_CORES + lax.axis_index("c")
        lane = lax.iota(jnp.int32, SC_LANES)

        def compute(p, c, b):
            for g in range(SC_CHUNK // SC_LANES):
                def rows4(q, hv):
                    j0 = g * SC_LANES + q * 4

                    def span(cc, accs):
                        for uu in range(SC_UNROLL):
                            off = pl.multiple_of((cc * SC_UNROLL + uu) * SC_LANES, SC_LANES)
                            x_hi = x_v[p, pl.ds(off, SC_LANES)]
                            x_lo = x_v[p, pl.ds(dw + off, SC_LANES)]
                            nxt = []
                            for kk in range(4):
                                hi, lo = _unpack(rows_v[b, j0 + kk, pl.ds(off, SC_LANES)])
                                nxt.append((accs[2 * kk] + hi * x_hi, accs[2 * kk + 1] + lo * x_lo))
                            accs = tuple(a for pair in nxt for a in pair)
                        return accs

                    accs = lax.fori_loop(0, dw // (SC_LANES * SC_UNROLL), span,
                                         tuple(jnp.zeros((SC_LANES,), F32) for _ in range(8)))
                    for kk in range(4):
                        hv = jnp.where(lane == q * 4 + kk, jnp.sum(accs[2 * kk] + accs[2 * kk + 1]), hv)
                    return hv

                hv = lax.fori_loop(0, SC_LANES // 4, rows4, jnp.zeros((SC_LANES,), F32))
                h_v[p, pl.ds(c * SC_CHUNK + g * SC_LANES, SC_LANES)] = hv

        _sc_token_pipeline(per_w, nch, wid, ids_hbm, xn_hbm, u_hbm, h_hbm, idx_v, x_v, rows_v, h_v,
                           gsem, isem, vsem, osem, compute)

    return k(ids, xn, u2)


def _sc_mix(ids, w, v2, n_tok):
    dw = v2.shape[1]
    d = 2 * dw
    nch = ids.shape[1]
    picks = nch * SC_CHUNK
    per_w = n_tok // SC_WORKERS
    assert per_w % 2 == 0 and nch % 2 == 0
    cb = 8
    mesh = plsc.VectorSubcoreMesh(core_axis_name="c", subcore_axis_name="s")

    @functools.partial(
        pl.kernel, mesh=mesh, out_type=jax.ShapeDtypeStruct((n_tok, d), F32),
        scratch_types=[pltpu.VMEM((2, nch, SC_CHUNK), jnp.int32), pltpu.VMEM((2, picks), F32),
                       pltpu.VMEM((2, SC_CHUNK, dw), jnp.int32), pltpu.VMEM((2, d), F32)]
        + [pltpu.SemaphoreType.DMA((2,))] * 4,
        compiler_params=pltpu.CompilerParams(needs_layout_passes=False), name="sc_mix")
    def k(ids_hbm, w_hbm, v_hbm, o_hbm, idx_v, w_v, rows_v, o_v, gsem, isem, vsem, osem):
        wid = lax.axis_index("s") * SC_CORES + lax.axis_index("c")

        def compute(p, c, b):
            for blk in range(dw // (cb * SC_LANES)):
                base = blk * cb * SC_LANES
                offs = [base + kk * SC_LANES for kk in range(cb)] + [dw + base + kk * SC_LANES for kk in range(cb)]
                if c == 0:
                    init = tuple(jnp.zeros((SC_LANES,), F32) for _ in offs)
                else:
                    init = tuple(o_v[p, pl.ds(o, SC_LANES)] for o in offs)

                def row(r, accs):
                    wj = plsc.load_gather(w_v.at[p], [jnp.full((SC_LANES,), c * SC_CHUNK, jnp.int32) + r])
                    his, los = [], []
                    for kk in range(cb):
                        hi, lo = _unpack(rows_v[b, r, pl.ds(base + kk * SC_LANES, SC_LANES)])
                        his.append(accs[kk] + wj * hi)
                        los.append(accs[cb + kk] + wj * lo)
                    return tuple(his + los)

                accs = lax.fori_loop(0, SC_CHUNK, row, init)
                for o, a in zip(offs, accs):
                    o_v[p, pl.ds(o, SC_LANES)] = a

        _sc_token_pipeline(per_w, nch, wid, ids_hbm, w_hbm, v_hbm, o_hbm, idx_v, w_v, rows_v, o_v,
                           gsem, isem, vsem, osem, compute)

    return k(ids, w, v2)


def _gate_gelu_body(h_ref, gate_ref, after_hbm, w_ref):
    del after_hbm
    w_ref[...] = gate_ref[...] * _gelu(h_ref[...])


def _gate_gelu(h, gate_tok, after):
    n, picks = h.shape
    rows = math.gcd(n, 1024)
    assert n % rows == 0
    return pl.pallas_call(
        _gate_gelu_body, grid=(n // rows,),
        in_specs=[pl.BlockSpec((rows, picks), lambda i: (i, 0)), pl.BlockSpec((rows, picks), lambda i: (i, 0)),
                  pl.BlockSpec(memory_space=pl.ANY)],
        out_specs=pl.BlockSpec((rows, picks), lambda i: (i, 0)),
        out_shape=jax.ShapeDtypeStruct((n, picks), F32), name="gate_gelu",
    )(h, gate_tok, after)


def _finish_body(x2_ref, o_ref, gfin_ref, y_ref, *, final_norm):
    y = x2_ref[...] + o_ref[...]
    y_ref[...] = _rms(y, gfin_ref[...]) if final_norm else y


def _finish(x2, o, g_final, final_norm):
    n, d = o.shape
    rows = math.gcd(n, 512)
    assert n % rows == 0
    return pl.pallas_call(
        functools.partial(_finish_body, final_norm=final_norm), grid=(n // rows,),
        in_specs=[pl.BlockSpec((rows, d), lambda i: (i, 0)), pl.BlockSpec((rows, d), lambda i: (i, 0)), _resident((1, d))],
        out_specs=pl.BlockSpec((rows, d), lambda i: (i, 0)),
        out_shape=jax.ShapeDtypeStruct((n, d), F32), name="finish",
    )(x2, o, g_final)


def _peer(ids_t, gate_t, xn, x2, g_final, lw, final_norm):
    picks, t = ids_t.shape
    ids_tok = ids_t.T
    total_steps = t // EXPERT_ROWS
    sc_steps = int(total_steps * SC_SHARE) if t >= SC_MIN_TOKENS else 0
    n_sc = sc_steps * EXPERT_ROWS
    if n_sc == 0:
        return [_experts(ids_tok, gate_t, xn, x2, g_final, lw["peer_uv"], final_norm, 0, total_steps, g_final)]
    assert n_sc % SC_WORKERS == 0 and picks % SC_CHUNK == 0
    tc_steps = total_steps - sc_steps
    first = int(tc_steps * TC_FIRST_SHARE)
    ids_sc = ids_tok.reshape(t, picks // SC_CHUNK, SC_CHUNK)
    h = _sc_dots(ids_sc, xn, lw["peer_u2"], n_sc)
    y_tc1 = _experts(ids_tok, gate_t, xn, x2, g_final, lw["peer_uv"], final_norm, sc_steps, first, g_final)
    w = _gate_gelu(h, gate_t.T, y_tc1)
    o = _sc_mix(ids_sc, w, lw["peer_v2"], n_sc)
    y_tc2 = _experts(ids_tok, gate_t, xn, x2, g_final, lw["peer_uv"], final_norm, sc_steps + first, tc_steps - first, w)
    y_sc = _finish(x2, o, g_final, final_norm)
    return [y_sc, y_tc1, y_tc2]


def _layer(x, window, conv_prev, pool_prev, start_pos, mk, mv, lw, g_final, final_norm):
    _, bn, _, seq = window
    d = x.shape[-1]
    x2, new_conv, new_pool = _mixer(x, window, conv_prev, pool_prev, start_pos, mk, mv, lw)
    x2 = x2.reshape(bn * seq, d)
    xn, ids_t, gate_t = _route(x2, lw["g_ffn"], lw["peer_wq"], lw["peer_keys"])
    y = _peer(ids_t, gate_t, xn, x2, g_final, lw, final_norm)
    return y, new_conv, new_pool


def kernel(x_prompt, x_sample, mem_prompt, cache_conv, cache_pool, cache_mem_k, cache_mem_v, g_mix, w_in, conv_w, conv_b, pool_w, pool_scale, g_mem, w_mk, w_mv, w_bc, w_bp, w_ba, gate_b, w_o, g_ffn, peer_wq, peer_keys, peer_u, peer_v, g_final):
    depth, d = g_mix.shape
    xp, xs = x_prompt, x_sample
    bp, bs = xp.shape[0], xs.shape[0]
    n_exp = peer_u.shape[1]
    gfin = g_final.reshape(1, d)
    conv_p, pool_p, mk_p, mv_p, conv_s, pool_s = [], [], [], [], [], []
    for l in range(depth):
        lw = dict(
            g_mix=g_mix[l].reshape(1, d), w_in=w_in[l].astype(BF16), conv_w=conv_w[l], conv_b=conv_b[l].reshape(1, -1),
            pool_w=pool_w[l].astype(BF16), pool_scale=pool_scale[l].reshape(1, -1), w_bc=w_bc[l].astype(BF16),
            w_bp=w_bp[l].astype(BF16), w_ba=w_ba[l].astype(BF16), gate_b=gate_b[l].reshape(1, -1), w_o=w_o[l].astype(BF16),
            g_ffn=g_ffn[l].reshape(1, d), peer_wq=peer_wq[l].astype(BF16), peer_keys=peer_keys[l].astype(BF16),
            peer_uv=_pack_halves(peer_u[l], peer_v[l]).reshape(n_exp, 1, d),
            peer_u2=_pack_pairs(peer_u[l]), peer_v2=_pack_pairs(peer_v[l]),
        )
        last = l == depth - 1
        mk, mv = _mem_kv(mem_prompt, g_mem[l], w_mk[l], w_mv[l])
        zc = jnp.zeros((bp, CONV_K - 1, conv_w.shape[-1]), xp.dtype)
        zp = jnp.zeros((bp, POOL_STATE, pool_scale.shape[-1]), xp.dtype)
        seq = xp.shape[1]
        seg = seq // PROMPT_SEGMENTS if seq % PROMPT_SEGMENTS == 0 and seq // PROMPT_SEGMENTS >= SC_MIN_TOKENS else seq
        pieces, cps, pps = [], [], []
        for b in range(bp):
            cprev, pprev = zc[b:b + 1], zp[b:b + 1]
            for s0 in range(0, seq, seg):
                ys, cprev, pprev = _layer(xp, (b, 1, s0, seg), cprev, pprev, s0, mk[b:b + 1], mv[b:b + 1], lw, gfin, last)
                pieces += ys
            cps.append(cprev); pps.append(pprev)
        xp = jnp.concatenate(pieces, axis=0).reshape(bp, seq, d)
        cp, pp = jnp.concatenate(cps, axis=0), jnp.concatenate(pps, axis=0)
        n_mem = cache_mem_k.shape[2]
        ys, cs, ps = _layer(xs, (0, bs, 0, xs.shape[1]), cache_conv[l], cache_pool[l], PAST_LEN, cache_mem_k[l].reshape(bs, n_mem, -1),
                            cache_mem_v[l].reshape(bs, n_mem, -1), lw, gfin, last)
        xs = jnp.concatenate(ys, axis=0).reshape(xs.shape)
        heads_shape = (bp, n_mem) + cache_mem_k.shape[3:]
        conv_p.append(cp); pool_p.append(pp); mk_p.append(mk.reshape(heads_shape)); mv_p.append(mv.reshape(heads_shape))
        conv_s.append(cs); pool_s.append(ps)
    return (xp, xs, jnp.stack(conv_p), jnp.stack(pool_p), jnp.stack(mk_p), jnp.stack(mv_p),
            jnp.stack(conv_s), jnp.stack(pool_s))
```

```python
import functools
import math

import jax
import jax.numpy as jnp
from jax import lax
from jax.experimental import pallas as pl
from jax.experimental.pallas import tpu as pltpu
from jax.experimental.pallas import tpu_sc as plsc

F32 = jnp.float32
BF16 = jnp.bfloat16
EPS = 1e-6

CONV_K = 3
POOL_WINDOWS = (2, 4, 8, 16)
POOL_STATE = max(POOL_WINDOWS) - 1
PAST_LEN = 1024
MEM_HEADS = 4
PEER_HEADS = 8
PEER_TOPK = 16

SUBLANES = 8
LANES = 128
MIXER_ROWS = 256
ROUTE_ROWS = 256
EXPERT_ROWS = 128
GATHER_SLOTS = 4
VMEM_LIMIT = 56 * 1024 * 1024

SC_CORES = 2
SC_SUBCORES = 16
SC_WORKERS = SC_CORES * SC_SUBCORES
SC_LANES = 16
SC_CHUNK = 64
SC_UNROLL = 16
SC_MIX_ROWS = 1
SC_SHARE = 0.72
SC_MIN_TOKENS = 8192
PROMPT_SEGMENTS = 2
TC_FIRST_SHARE = 0.56


def _rms(x, g):
    return x * lax.rsqrt(jnp.mean(x * x, axis=-1, keepdims=True) + EPS) * g


def _dot(a, b):
    return jnp.dot(a, b, preferred_element_type=F32)


def _dot_nt(a, b):
    return lax.dot_general(a, b, (((1,), (1,)), ((), ())), preferred_element_type=F32)


def _resident(shape):
    zeros = (0,) * len(shape)
    return pl.BlockSpec(shape, lambda *_: zeros, pipeline_mode=pl.Buffered(1))


def _memkv_body(mem_ref, g_ref, wk_ref, wv_ref, k_ref, v_ref):
    m = _rms(mem_ref[...], g_ref[...]).astype(BF16)
    k_ref[...] = _dot(m, wk_ref[...])
    v_ref[...] = _dot(m, wv_ref[...])


def _mem_kv(mem, g_mem, w_mk, w_mv):
    bn, n_mem, d = mem.shape
    att_w = w_mk.shape[1]
    k, v = pl.pallas_call(
        _memkv_body,
        out_shape=[jax.ShapeDtypeStruct((bn * n_mem, att_w), F32)] * 2,
        name="mem_kv",
    )(mem.reshape(bn * n_mem, d), g_mem.reshape(1, d), w_mk.astype(BF16), w_mv.astype(BF16))
    return k.reshape(bn, n_mem, att_w), v.reshape(bn, n_mem, att_w)


def _mixer_body(x_ref, cprev_ref, pprev_ref, mk_ref, mv_ref, gmix_ref, win_ref, convw_ref, convb_ref, poolw_ref,
                pscale_ref, wbc_ref, wbp_ref, wba_ref, gateb_ref, wo_ref,
                x2_ref, nconv_ref, npool_ref, zbuf, ubuf, *, rows, start_pos, conv_w, pool_w, att_w):
    s = pl.program_id(1)
    d_model = x_ref.shape[-1]
    z0 = SUBLANES
    u0 = 2 * SUBLANES

    @pl.when(s == 0)
    def _():
        zbuf[z0 - (CONV_K - 1):z0, :] = cprev_ref[0]
        ubuf[u0 - POOL_STATE:u0, :] = pprev_ref[0]

    x = x_ref[0]
    hb = _rms(x, gmix_ref[...]).astype(BF16)

    c0 = 0
    pa = _dot(hb, win_ref[:, c0:c0 + 3 * conv_w])
    xc, bg, cg = pa[:, :conv_w], pa[:, conv_w:2 * conv_w], pa[:, 2 * conv_w:]
    zbuf[z0:z0 + rows, :] = cg * xc
    cw = convw_ref[...]
    conv = zbuf[z0 - 2:z0 - 2 + rows, :] * cw[0:1]
    for k in range(1, CONV_K):
        conv = conv + zbuf[z0 - 2 + k:z0 - 2 + k + rows, :] * cw[k:k + 1]
    conv = conv + convb_ref[...]
    ya = _dot((bg * conv).astype(BF16), wbc_ref[...])
    last_z = zbuf[z0 + rows - (CONV_K - 1):z0 + rows, :]
    nconv_ref[0] = last_z
    zbuf[z0 - (CONV_K - 1):z0, :] = last_z
    c0 += 3 * conv_w

    up = _dot(hb, win_ref[:, c0:c0 + pool_w])
    ubuf[u0:u0 + rows, :] = up
    pos = start_pos + s * rows + lax.broadcasted_iota(jnp.int32, (rows, 1), 0)
    gw = pool_w // len(POOL_WINDOWS)
    ys = []
    for g, w in enumerate(POOL_WINDOWS):
        cur = up[:, g * gw:(g + 1) * gw]
        acc = cur
        for k in range(1, w):
            acc = acc + ubuf[u0 - k:u0 - k + rows, g * gw:(g + 1) * gw]
        cnt = jnp.minimum(pos + 1, w).astype(F32)
        ys.append(_dot((acc / cnt - cur).astype(BF16), poolw_ref[g]))
    yb = _dot((jnp.concatenate(ys, axis=-1) * pscale_ref[...]).astype(BF16), wbp_ref[...])
    last_u = ubuf[u0 + rows - POOL_STATE:u0 + rows, :]
    npool_ref[0] = last_u
    ubuf[u0 - POOL_STATE:u0, :] = last_u
    c0 += pool_w

    q = _dot(hb, win_ref[:, c0:c0 + att_w])
    kb = mk_ref[0].astype(BF16)
    vb = mv_ref[0].astype(BF16)
    hd = att_w // MEM_HEADS
    heads = []
    for h in range(MEM_HEADS):
        sc = _dot_nt(q[:, h * hd:(h + 1) * hd].astype(BF16), kb[:, h * hd:(h + 1) * hd]) * (hd ** -0.5)
        e = jnp.exp(sc - jnp.max(sc, axis=-1, keepdims=True))
        p = e / jnp.sum(e, axis=-1, keepdims=True)
        heads.append(_dot(p.astype(BF16), vb[:, h * hd:(h + 1) * hd]))
    yc = _dot(jnp.concatenate(heads, axis=-1).astype(BF16), wba_ref[...])
    c0 += att_w

    merged = None
    for i, y in enumerate((ya, yb, yc)):
        gl = _dot(hb, win_ref[:, c0 + i * d_model:c0 + (i + 1) * d_model]) + gateb_ref[:, i * d_model:(i + 1) * d_model]
        term = (1.0 / (1.0 + jnp.exp(-gl))) * y
        merged = term if merged is None else merged + term
    x2_ref[0] = x + _dot(merged.astype(BF16), wo_ref[...])


def _mixer(x, window, conv_prev, pool_prev, start_pos, mk, mv, lw):
    b0, bn, s0, seq = window
    d = x.shape[-1]
    conv_w, pool_w = conv_prev.shape[-1], pool_prev.shape[-1]
    n_mem, att_w = mk.shape[1], mk.shape[2]
    rows = min(MIXER_ROWS, seq)
    assert seq % rows == 0 and s0 % rows == 0 and rows % SUBLANES == 0 and rows >= POOL_STATE
    row0 = s0 // rows
    in_cols = lw["w_in"].shape[1]
    per_b = lambda shape: pl.BlockSpec((1,) + shape, lambda b, s: (b, 0, 0))
    body = functools.partial(_mixer_body, rows=rows, start_pos=start_pos, conv_w=conv_w, pool_w=pool_w, att_w=att_w)
    return pl.pallas_call(
        body,
        grid=(bn, seq // rows),
        in_specs=[
            pl.BlockSpec((1, rows, d), lambda b, s: (b0 + b, row0 + s, 0)),
            per_b((CONV_K - 1, conv_w)), per_b((POOL_STATE, pool_w)), per_b((n_mem, att_w)), per_b((n_mem, att_w)),
            _resident((1, d)), _resident((d, in_cols)), _resident((CONV_K, conv_w)), _resident((1, conv_w)),
            _resident(lw["pool_w"].shape), _resident((1, pool_w)), _resident((conv_w, d)), _resident((pool_w, d)),
            _resident((att_w, d)), _resident((1, 3 * d)), _resident((d, d)),
        ],
        out_specs=[
            pl.BlockSpec((1, rows, d), lambda b, s: (b, s, 0)),
            per_b((CONV_K - 1, conv_w)), per_b((POOL_STATE, pool_w)),
        ],
        out_shape=[
            jax.ShapeDtypeStruct((bn, seq, d), F32),
            jax.ShapeDtypeStruct((bn, CONV_K - 1, conv_w), F32),
            jax.ShapeDtypeStruct((bn, POOL_STATE, pool_w), F32),
        ],
        scratch_shapes=[pltpu.VMEM((SUBLANES + rows, conv_w), F32), pltpu.VMEM((2 * SUBLANES + rows, pool_w), F32)],
        compiler_params=pltpu.CompilerParams(dimension_semantics=("arbitrary", "arbitrary"), vmem_limit_bytes=VMEM_LIMIT),
        name="mixer",
    )(x, conv_prev, pool_prev, mk, mv, lw["g_mix"], lw["w_in"], lw["conv_w"], lw["conv_b"], lw["pool_w"],
      lw["pool_scale"], lw["w_bc"], lw["w_bp"], lw["w_ba"], lw["gate_b"], lw["w_o"])


_STAIR = [(a, b) for a in range(PEER_TOPK) for b in range(PEER_TOPK) if (a + 1) * (b + 1) <= PEER_TOPK]
_STAIR_HEAD = 24
_STAIR_TAIL = [_STAIR[_STAIR_HEAD + 8 * i:_STAIR_HEAD + 8 * (i + 1)] for i in range(4)]


def _top16(v, payload=None):
    n, t = v.shape
    rows = lax.broadcasted_iota(jnp.int32, (n, t), 0)
    out_rows = lax.broadcasted_iota(jnp.int32, (PEER_TOPK, t), 0)
    vals = jnp.zeros((PEER_TOPK, t), F32)
    pay = jnp.zeros((PEER_TOPK, t), jnp.int32)
    for k in range(PEER_TOPK):
        m = jnp.max(v, axis=0, keepdims=True)
        am = jnp.min(jnp.where(v == m, rows, n), axis=0, keepdims=True)
        hit = rows == am
        p = am if payload is None else jnp.sum(jnp.where(hit, payload, 0), axis=0, keepdims=True)
        vals = jnp.where(out_rows == k, m, vals)
        pay = jnp.where(out_rows == k, p, pay)
        v = jnp.where(hit, -jnp.inf, v)
    return vals, pay


def _take_rows(src, idxs):
    t = src.shape[1]
    out_rows = lax.broadcasted_iota(jnp.int32, (SUBLANES, t), 0)
    acc = jnp.broadcast_to(src[idxs[0]:idxs[0] + 1, :], (SUBLANES, t))
    for i in range(1, SUBLANES):
        acc = jnp.where(out_rows == i, src[idxs[i]:idxs[i] + 1, :], acc)
    return acc


def _stair(first, second, combine):
    blocks = [combine(first[0:1, :], second), combine(first[1:2, :], second[0:SUBLANES, :])]
    for blk in _STAIR_TAIL:
        blk = blk + [(0, 0)] * (SUBLANES - len(blk))
        blocks.append(combine(_take_rows(first, [a for a, _ in blk]), _take_rows(second, [b for _, b in blk])))
    return jnp.concatenate(blocks, axis=0)


def _route_body(x2_ref, gffn_ref, wq_ref, keys_ref, xn_ref, ids_ref, gate_ref, *, n_keys):
    xn = _rms(x2_ref[...], gffn_ref[...])
    xn_ref[...] = xn
    q = _dot(xn.astype(BF16), wq_ref[...])
    half = keys_ref.shape[-1]
    t = q.shape[0]
    for h in range(PEER_HEADS):
        scores = [_dot_nt(keys_ref[p], q[:, (2 * h + p) * half:(2 * h + p + 1) * half].astype(BF16)) for p in range(2)]
        for l0 in range(0, t, LANES):
            (s1, i1), (s2, i2) = [_top16(sc[:, l0:l0 + LANES]) for sc in scores]
            cand = _stair(s1, s2, lambda a, b: a + b)
            pad = lax.broadcasted_iota(jnp.int32, cand.shape, 0) >= len(_STAIR)
            expert = _stair(i1, i2, lambda a, b: a * n_keys + b)
            sc, e = _top16(jnp.where(pad, -jnp.inf, cand), expert)
            ex = jnp.exp(sc - sc[0:1, :])
            ids_ref[h * PEER_TOPK:(h + 1) * PEER_TOPK, l0:l0 + LANES] = e
            gate_ref[h * PEER_TOPK:(h + 1) * PEER_TOPK, l0:l0 + LANES] = ex / jnp.sum(ex, axis=0, keepdims=True)


def _route(x2, g_ffn, wq, keys):
    t, d = x2.shape
    rows = ROUTE_ROWS
    assert t % rows == 0
    picks = PEER_HEADS * PEER_TOPK
    return pl.pallas_call(
        functools.partial(_route_body, n_keys=keys.shape[1]),
        grid=(t // rows,),
        in_specs=[pl.BlockSpec((rows, d), lambda i: (i, 0)), _resident((1, d)), _resident(wq.shape), _resident(keys.shape)],
        out_specs=[pl.BlockSpec((rows, d), lambda i: (i, 0)), pl.BlockSpec((picks, rows), lambda i: (0, i)),
                   pl.BlockSpec((picks, rows), lambda i: (0, i))],
        out_shape=[jax.ShapeDtypeStruct((t, d), F32), jax.ShapeDtypeStruct((picks, t), jnp.int32),
                   jax.ShapeDtypeStruct((picks, t), F32)],
        compiler_params=pltpu.CompilerParams(dimension_semantics=("arbitrary",), vmem_limit_bytes=VMEM_LIMIT),
        name="route",
    )(x2, g_ffn, wq, keys)


def _gelu(h):
    return 0.5 * h * (1.0 + lax.erf(h * (2.0 ** -0.5)))


def _experts_body(ids_hbm, gate_ref, xn_ref, x2_ref, gfin_ref, uv_hbm, after_hbm, y_ref, *scratch,
                  rows, picks, first_step, steps, final_norm):
    del after_hbm
    bufs, (sem, ids_smem, ids_sem) = scratch[:GATHER_SLOTS], scratch[GATHER_SLOTS:]
    i = pl.program_id(0)
    half = i % 2
    n = rows * picks
    d = xn_ref.shape[-1]
    lead = GATHER_SLOTS - 1
    groups = picks // SUBLANES

    def ids_copy(step, which):
        return pltpu.make_async_copy(ids_hbm.at[first_step + step], ids_smem.at[pl.ds(which * n, n)], ids_sem.at[which])

    def issue(off, slot, j0, j1):
        for j in range(j0, j1):
            pltpu.async_copy(uv_hbm.at[ids_smem[off + j]], bufs[slot].at[pl.ds(j, 1)], sem.at[slot], priority=j % 2)

    def wait_slot(slot):
        pltpu.make_async_copy(bufs[slot], bufs[slot], sem.at[slot]).wait()

    @pl.when(i == 0)
    def _():
        ids_copy(0, 0).start()
        ids_copy(0, 0).wait()
        if steps > 1:
            ids_copy(1, 1).start()
        for s in range(lead):
            issue(s * picks, s, 0, picks)

    lane = lax.broadcasted_iota(jnp.int32, (picks, rows), 1)

    def token(t, slot, ahead_off):
        ahead_slot = (slot + lead) % GATHER_SLOTS
        buf = bufs[slot]
        wait_slot(slot)
        if ahead_off is not None:
            issue(ahead_off, ahead_slot, 0, picks)
        x = xn_ref[pl.ds(t, 1), :]
        g = jnp.sum(jnp.where(lane == t, gate_ref[...], 0.0), axis=-1, keepdims=True)
        hs = []
        for k in range(groups):
            r0 = k * SUBLANES
            u_rows = lax.bitcast_convert_type(buf[r0:r0 + SUBLANES, :] & HI_MASK, F32)
            hs.append(jnp.sum(u_rows * x, axis=-1, keepdims=True))
        w = g * _gelu(jnp.concatenate(hs, axis=0))
        acc = None
        for k in range(groups):
            r0 = k * SUBLANES
            v_rows = lax.bitcast_convert_type(buf[r0:r0 + SUBLANES, :] << 16, F32)
            part = w[r0:r0 + SUBLANES, :] * v_rows
            acc = part if acc is None else acc + part
        y = x2_ref[pl.ds(t, 1), :] + jnp.sum(acc, axis=0, keepdims=True)
        y_ref[pl.ds(t, 1), :] = _rms(y, gfin_ref[...]) if final_norm else y

    def four_tokens(k, carry):
        for s in range(GATHER_SLOTS):
            t = k * GATHER_SLOTS + s
            token(t, s, half * n + (t + lead) * picks)
        return carry

    lax.fori_loop(0, rows // GATHER_SLOTS - 1, four_tokens, 0)

    t0 = rows - GATHER_SLOTS
    token(t0, 0, half * n + (t0 + lead) * picks)

    @pl.when(i + 1 < steps)
    def _():
        ids_copy(i + 1, 1 - half).wait()

    for s in range(1, GATHER_SLOTS):
        @pl.when(i + 1 < steps)
        def _():
            issue((1 - half) * n + (s - 1) * picks, (s + lead) % GATHER_SLOTS, 0, picks)

        token(t0 + s, s, None)

    @pl.when(i + 2 < steps)
    def _():
        ids_copy(i + 2, half).start()


def _experts(ids_tok, gate_t, xn, x2, g_final, uv, final_norm, first_step, steps, after):
    t, picks = ids_tok.shape
    d = xn.shape[1]
    rows = EXPERT_ROWS
    assert t % rows == 0 and rows >= 2 * GATHER_SLOTS and rows % GATHER_SLOTS == 0 and picks % (2 * SUBLANES) == 0
    ids = ids_tok.reshape(t // rows, rows * picks)
    body = functools.partial(_experts_body, rows=rows, picks=picks, first_step=first_step, steps=steps,
                             final_norm=final_norm)
    return pl.pallas_call(
        body,
        grid=(steps,),
        in_specs=[
            pl.BlockSpec(memory_space=pl.ANY),
            pl.BlockSpec((picks, rows), lambda i: (0, first_step + i)),
            pl.BlockSpec((rows, d), lambda i: (first_step + i, 0)),
            pl.BlockSpec((rows, d), lambda i: (first_step + i, 0)),
            _resident((1, d)),
            pl.BlockSpec(memory_space=pl.ANY),
            pl.BlockSpec(memory_space=pl.ANY),
        ],
        out_specs=pl.BlockSpec((rows, d), lambda i: (i, 0)),
        out_shape=jax.ShapeDtypeStruct((steps * rows, d), F32),
        scratch_shapes=[pltpu.VMEM((picks, d), jnp.int32)] * GATHER_SLOTS + [
            pltpu.SemaphoreType.DMA((GATHER_SLOTS,)),
            pltpu.SMEM((2 * rows * picks,), jnp.int32),
            pltpu.SemaphoreType.DMA((2,)),
        ],
        compiler_params=pltpu.CompilerParams(dimension_semantics=("arbitrary",), vmem_limit_bytes=VMEM_LIMIT),
        name="experts",
    )(ids, gate_t, xn, x2, g_final, uv, after)


def _sc_token_pipeline(per_w, nch, wid, ids_hbm, vec_hbm, tab_hbm, out_hbm, idx_v, vec_v, rows_v, out_v,
                       gsem, isem, vsem, osem, compute):
    base = wid * per_w

    def ids_copy(t, p):
        return pltpu.make_async_copy(ids_hbm.at[t], idx_v.at[p], isem.at[p])

    def vec_copy(t, p):
        return pltpu.make_async_copy(vec_hbm.at[t], vec_v.at[p], vsem.at[p])

    def out_copy(t, p):
        return pltpu.make_async_copy(out_v.at[p], out_hbm.at[t], osem.at[p])

    def gather(p, c, b):
        return pltpu.make_async_copy(tab_hbm.at[idx_v.at[p, c]], rows_v.at[b], gsem.at[b])

    ids_copy(base, 0).start()
    vec_copy(base, 0).start()
    ids_copy(base, 0).wait()
    vec_copy(base, 0).wait()
    gather(0, 0, 0).start()

    def pair(ii, carry):
        for p in range(2):
            i = 2 * ii + p
            t = base + i
            has_next = i + 1 < per_w

            @pl.when(has_next)
            def _():
                ids_copy(t + 1, 1 - p).start()
                vec_copy(t + 1, 1 - p).start()

            @pl.when(i >= 2)
            def _():
                out_copy(t - 2, p).wait()

            for c in range(nch):
                b = c % 2
                if c + 1 < nch:
                    gather(p, c + 1, 1 - b).start()
                else:
                    @pl.when(has_next)
                    def _():
                        ids_copy(t + 1, 1 - p).wait()
                        vec_copy(t + 1, 1 - p).wait()
                        gather(1 - p, 0, 1 - b).start()
                gather(p, c, b).wait()
                compute(p, c, b)
            out_copy(t, p).start()
        return carry

    lax.fori_loop(0, per_w // 2, pair, 0)
    out_copy(base + per_w - 2, 0).wait()
    out_copy(base + per_w - 1, 1).wait()


HI_MASK = -65536


def _pack_pairs(tab):
    d = tab.shape[1]
    return _pack_halves(tab[:, :d // 2], tab[:, d // 2:])


def _pack_halves(hi, lo):
    bits = lambda a: lax.bitcast_convert_type(a.astype(jnp.bfloat16), jnp.uint16).astype(jnp.uint32)
    return lax.bitcast_convert_type((bits(hi) << 16) | bits(lo), jnp.int32)


def _unpack(words):
    hi = lax.bitcast_convert_type(words & HI_MASK, F32)
    lo = lax.bitcast_convert_type(words << 16, F32)
    return hi, lo


def _sc_dots(ids, xn, u2, n_tok):
    d = xn.shape[1]
    dw = u2.shape[1]
    nch = ids.shape[1]
    picks = nch * SC_CHUNK
    per_w = n_tok // SC_WORKERS
    assert per_w % 2 == 0 and nch % 2 == 0 and 2 * dw == d
    mesh = plsc.VectorSubcoreMesh(core_axis_name="c", subcore_axis_name="s")

    @functools.partial(
        pl.kernel, mesh=mesh, out_type=jax.ShapeDtypeStruct((n_tok, picks), F32),
        scratch_types=[pltpu.VMEM((2, nch, SC_CHUNK), jnp.int32), pltpu.VMEM((2, d), F32),
                       pltpu.VMEM((2, SC_CHUNK, dw), jnp.int32), pltpu.VMEM((2, picks), F32)]
        + [pltpu.SemaphoreType.DMA((2,))] * 4,
        compiler_params=pltpu.CompilerParams(needs_layout_passes=False), name="sc_dots")
    def k(ids_hbm, xn_hbm, u_hbm, h_hbm, idx_v, x_v, rows_v, h_v, gsem, isem, vsem, osem):
        wid = lax.axis_index("s") * SC_CORES + lax.axis_index("c")
        lane = lax.iota(jnp.int32, SC_LANES)

        def compute(p, c, b):
            for g in range(SC_CHUNK // SC_LANES):
                def rows4(q, hv):
                    j0 = g * SC_LANES + q * 4

                    def span(cc, accs):
                        for uu in range(SC_UNROLL):
                            off = pl.multiple_of((cc * SC_UNROLL + uu) * SC_LANES, SC_LANES)
                            x_hi = x_v[p, pl.ds(off, SC_LANES)]
                            x_lo = x_v[p, pl.ds(dw + off, SC_LANES)]
                            nxt = []
                            for kk in range(4):
                                hi, lo = _unpack(rows_v[b, j0 + kk, pl.ds(off, SC_LANES)])
                                nxt.append((accs[2 * kk] + hi * x_hi, accs[2 * kk + 1] + lo * x_lo))
                            accs = tuple(a for pair in nxt for a in pair)
                        return accs

                    accs = lax.fori_loop(0, dw // (SC_LANES * SC_UNROLL), span,
                                         tuple(jnp.zeros((SC_LANES,), F32) for _ in range(8)))
                    for kk in range(4):
                        hv = jnp.where(lane == q * 4 + kk, jnp.sum(accs[2 * kk] + accs[2 * kk + 1]), hv)
                    return hv

                hv = lax.fori_loop(0, SC_LANES // 4, rows4, jnp.zeros((SC_LANES,), F32))
                h_v[p, pl.ds(c * SC_CHUNK + g * SC_LANES, SC_LANES)] = hv

        _sc_token_pipeline(per_w, nch, wid, ids_hbm, xn_hbm, u_hbm, h_hbm, idx_v, x_v, rows_v, h_v,
                           gsem, isem, vsem, osem, compute)

    return k(ids, xn, u2)


def _sc_mix(ids, w, v2, n_tok):
    dw = v2.shape[1]
    d = 2 * dw
    nch = ids.shape[1]
    picks = nch * SC_CHUNK
    per_w = n_tok // SC_WORKERS
    assert per_w % 2 == 0 and nch % 2 == 0
    cb = 8
    mesh = plsc.VectorSubcoreMesh(core_axis_name="c", subcore_axis_name="s")

    @functools.partial(
        pl.kernel, mesh=mesh, out_type=jax.ShapeDtypeStruct((n_tok, d), F32),
        scratch_types=[pltpu.VMEM((2, nch, SC_CHUNK), jnp.int32), pltpu.VMEM((2, picks), F32),
                       pltpu.VMEM((2, SC_CHUNK, dw), jnp.int32), pltpu.VMEM((2, d), F32)]
        + [pltpu.SemaphoreType.DMA((2,))] * 4,
        compiler_params=pltpu.CompilerParams(needs_layout_passes=False), name="sc_mix")
    def k(ids_hbm, w_hbm, v_hbm, o_hbm, idx_v, w_v, rows_v, o_v, gsem, isem, vsem, osem):
        wid = lax.axis_index("s") * SC_CORES + lax.axis_index("c")

        def compute(p, c, b):
            for blk in range(dw // (cb * SC_LANES)):
                base = blk * cb * SC_LANES
                offs = [base + kk * SC_LANES for kk in range(cb)] + [dw + base + kk * SC_LANES for kk in range(cb)]
                if c == 0:
                    init = tuple(jnp.zeros((SC_LANES,), F32) for _ in offs)
                else:
                    init = tuple(o_v[p, pl.ds(o, SC_LANES)] for o in offs)

                def rows(rr, accs):
                    for k in range(SC_MIX_ROWS):
                        r = rr * SC_MIX_ROWS + k
                        wj = plsc.load_gather(w_v.at[p], [jnp.full((SC_LANES,), c * SC_CHUNK, jnp.int32) + r])
                        his, los = [], []
                        for kk in range(cb):
                            hi, lo = _unpack(rows_v[b, r, pl.ds(base + kk * SC_LANES, SC_LANES)])
                            his.append(accs[kk] + wj * hi)
                            los.append(accs[cb + kk] + wj * lo)
                        accs = tuple(his + los)
                    return accs

                accs = lax.fori_loop(0, SC_CHUNK // SC_MIX_ROWS, rows, init)
                for o, a in zip(offs, accs):
                    o_v[p, pl.ds(o, SC_LANES)] = a

        _sc_token_pipeline(per_w, nch, wid, ids_hbm, w_hbm, v_hbm, o_hbm, idx_v, w_v, rows_v, o_v,
                           gsem, isem, vsem, osem, compute)

    return k(ids, w, v2)


def _gate_gelu_body(h_ref, gate_ref, after_hbm, w_ref):
    del after_hbm
    w_ref[...] = gate_ref[...] * _gelu(h_ref[...])


def _gate_gelu(h, gate_tok, after):
    n, picks = h.shape
    rows = math.gcd(n, 1024)
    assert n % rows == 0
    return pl.pallas_call(
        _gate_gelu_body, grid=(n // rows,),
        in_specs=[pl.BlockSpec((rows, picks), lambda i: (i, 0)), pl.BlockSpec((rows, picks), lambda i: (i, 0)),
                  pl.BlockSpec(memory_space=pl.ANY)],
        out_specs=pl.BlockSpec((rows, picks), lambda i: (i, 0)),
        out_shape=jax.ShapeDtypeStruct((n, picks), F32), name="gate_gelu",
    )(h, gate_tok, after)


def _finish_body(x2_ref, o_ref, gfin_ref, y_ref, *, final_norm):
    y = x2_ref[...] + o_ref[...]
    y_ref[...] = _rms(y, gfin_ref[...]) if final_norm else y


def _finish(x2, o, g_final, final_norm):
    n, d = o.shape
    rows = math.gcd(n, 512)
    assert n % rows == 0
    return pl.pallas_call(
        functools.partial(_finish_body, final_norm=final_norm), grid=(n // rows,),
        in_specs=[pl.BlockSpec((rows, d), lambda i: (i, 0)), pl.BlockSpec((rows, d), lambda i: (i, 0)), _resident((1, d))],
        out_specs=pl.BlockSpec((rows, d), lambda i: (i, 0)),
        out_shape=jax.ShapeDtypeStruct((n, d), F32), name="finish",
    )(x2, o, g_final)


def _peer(ids_t, gate_t, xn, x2, g_final, lw, final_norm):
    picks, t = ids_t.shape
    ids_tok = ids_t.T
    total_steps = t // EXPERT_ROWS
    sc_steps = int(total_steps * SC_SHARE) if t >= SC_MIN_TOKENS else 0
    n_sc = sc_steps * EXPERT_ROWS
    if n_sc == 0:
        return [_experts(ids_tok, gate_t, xn, x2, g_final, lw["peer_uv"], final_norm, 0, total_steps, g_final)]
    assert n_sc % SC_WORKERS == 0 and picks % SC_CHUNK == 0
    tc_steps = total_steps - sc_steps
    first = int(tc_steps * TC_FIRST_SHARE)
    ids_sc = ids_tok.reshape(t, picks // SC_CHUNK, SC_CHUNK)
    h = _sc_dots(ids_sc, xn, lw["peer_u2"], n_sc)
    y_tc1 = _experts(ids_tok, gate_t, xn, x2, g_final, lw["peer_uv"], final_norm, sc_steps, first, g_final)
    w = _gate_gelu(h, gate_t.T, y_tc1)
    o = _sc_mix(ids_sc, w, lw["peer_v2"], n_sc)
    y_tc2 = _experts(ids_tok, gate_t, xn, x2, g_final, lw["peer_uv"], final_norm, sc_steps + first, tc_steps - first, w)
    y_sc = _finish(x2, o, g_final, final_norm)
    return [y_sc, y_tc1, y_tc2]


def _layer(x, window, conv_prev, pool_prev, start_pos, mk, mv, lw, g_final, final_norm):
    _, bn, _, seq = window
    d = x.shape[-1]
    x2, new_conv, new_pool = _mixer(x, window, conv_prev, pool_prev, start_pos, mk, mv, lw)
    x2 = x2.reshape(bn * seq, d)
    xn, ids_t, gate_t = _route(x2, lw["g_ffn"], lw["peer_wq"], lw["peer_keys"])
    y = _peer(ids_t, gate_t, xn, x2, g_final, lw, final_norm)
    return y, new_conv, new_pool


def kernel(x_prompt, x_sample, mem_prompt, cache_conv, cache_pool, cache_mem_k, cache_mem_v, g_mix, w_in, conv_w, conv_b, pool_w, pool_scale, g_mem, w_mk, w_mv, w_bc, w_bp, w_ba, gate_b, w_o, g_ffn, peer_wq, peer_keys, peer_u, peer_v, g_final):
    depth, d = g_mix.shape
    xp, xs = x_prompt, x_sample
    bp, bs = xp.shape[0], xs.shape[0]
    n_exp = peer_u.shape[1]
    gfin = g_final.reshape(1, d)
    conv_p, pool_p, mk_p, mv_p, conv_s, pool_s = [], [], [], [], [], []
    for l in range(depth):
        lw = dict(
            g_mix=g_mix[l].reshape(1, d), w_in=w_in[l].astype(BF16), conv_w=conv_w[l], conv_b=conv_b[l].reshape(1, -1),
            pool_w=pool_w[l].astype(BF16), pool_scale=pool_scale[l].reshape(1, -1), w_bc=w_bc[l].astype(BF16),
            w_bp=w_bp[l].astype(BF16), w_ba=w_ba[l].astype(BF16), gate_b=gate_b[l].reshape(1, -1), w_o=w_o[l].astype(BF16),
            g_ffn=g_ffn[l].reshape(1, d), peer_wq=peer_wq[l].astype(BF16), peer_keys=peer_keys[l].astype(BF16),
            peer_uv=_pack_halves(peer_u[l], peer_v[l]).reshape(n_exp, 1, d),
            peer_u2=_pack_pairs(peer_u[l]), peer_v2=_pack_pairs(peer_v[l]),
        )
        last = l == depth - 1
        mk, mv = _mem_kv(mem_prompt, g_mem[l], w_mk[l], w_mv[l])
        zc = jnp.zeros((bp, CONV_K - 1, conv_w.shape[-1]), xp.dtype)
        zp = jnp.zeros((bp, POOL_STATE, pool_scale.shape[-1]), xp.dtype)
        seq = xp.shape[1]
        seg = seq // PROMPT_SEGMENTS if seq % PROMPT_SEGMENTS == 0 and seq // PROMPT_SEGMENTS >= SC_MIN_TOKENS else seq
        pieces, cps, pps = [], [], []
        for b in range(bp):
            cprev, pprev = zc[b:b + 1], zp[b:b + 1]
            for s0 in range(0, seq, seg):
                ys, cprev, pprev = _layer(xp, (b, 1, s0, seg), cprev, pprev, s0, mk[b:b + 1], mv[b:b + 1], lw, gfin, last)
                pieces += ys
            cps.append(cprev); pps.append(pprev)
        xp = jnp.concatenate(pieces, axis=0).reshape(bp, seq, d)
        cp, pp = jnp.concatenate(cps, axis=0), jnp.concatenate(pps, axis=0)
        n_mem = cache_mem_k.shape[2]
        ys, cs, ps = _layer(xs, (0, bs, 0, xs.shape[1]), cache_conv[l], cache_pool[l], PAST_LEN, cache_mem_k[l].reshape(bs, n_mem, -1),
                            cache_mem_v[l].reshape(bs, n_mem, -1), lw, gfin, last)
        xs = jnp.concatenate(ys, axis=0).reshape(xs.shape)
        heads_shape = (bp, n_mem) + cache_mem_k.shape[3:]
        conv_p.append(cp); pool_p.append(pp); mk_p.append(mk.reshape(heads_shape)); mv_p.append(mv.reshape(heads_shape))
        conv_s.append(cs); pool_s.append(ps)
    return (xp, xs, jnp.stack(conv_p), jnp.stack(pool_p), jnp.stack(mk_p), jnp.stack(mv_p),
            jnp.stack(conv_s), jnp.stack(pool_s))
```

```python
import functools
import math

import jax
import jax.numpy as jnp
from jax import lax
from jax.experimental import pallas as pl
from jax.experimental.pallas import tpu as pltpu
from jax.experimental.pallas import tpu_sc as plsc

F32 = jnp.float32
BF16 = jnp.bfloat16
EPS = 1e-6

CONV_K = 3
POOL_WINDOWS = (2, 4, 8, 16)
POOL_STATE = max(POOL_WINDOWS) - 1
PAST_LEN = 1024
MEM_HEADS = 4
PEER_HEADS = 8
PEER_TOPK = 16

SUBLANES = 8
LANES = 128
MIXER_ROWS = 512
ROUTE_ROWS = 256
EXPERT_ROWS = 128
GATHER_SLOTS = 4
VMEM_LIMIT = 56 * 1024 * 1024

SC_CORES = 2
SC_SUBCORES = 16
SC_WORKERS = SC_CORES * SC_SUBCORES
SC_LANES = 16
SC_CHUNK = 64
SC_UNROLL = 16
SC_SHARE = 0.72
SC_MIN_TOKENS = 8192
PROMPT_SEGMENTS = 2
TC_FIRST_SHARE = 0.56


def _rms(x, g):
    return x * lax.rsqrt(jnp.mean(x * x, axis=-1, keepdims=True) + EPS) * g


def _dot(a, b):
    return jnp.dot(a, b, preferred_element_type=F32)


def _dot_nt(a, b):
    return lax.dot_general(a, b, (((1,), (1,)), ((), ())), preferred_element_type=F32)


def _resident(shape):
    zeros = (0,) * len(shape)
    return pl.BlockSpec(shape, lambda *_: zeros, pipeline_mode=pl.Buffered(1))


def _memkv_body(mem_ref, g_ref, wk_ref, wv_ref, k_ref, v_ref):
    m = _rms(mem_ref[...], g_ref[...]).astype(BF16)
    k_ref[...] = _dot(m, wk_ref[...])
    v_ref[...] = _dot(m, wv_ref[...])


def _mem_kv(mem, g_mem, w_mk, w_mv):
    bn, n_mem, d = mem.shape
    att_w = w_mk.shape[1]
    k, v = pl.pallas_call(
        _memkv_body,
        out_shape=[jax.ShapeDtypeStruct((bn * n_mem, att_w), F32)] * 2,
        name="mem_kv",
    )(mem.reshape(bn * n_mem, d), g_mem.reshape(1, d), w_mk.astype(BF16), w_mv.astype(BF16))
    return k.reshape(bn, n_mem, att_w), v.reshape(bn, n_mem, att_w)


def _mixer_body(x_ref, cprev_ref, pprev_ref, mk_ref, mv_ref, gmix_ref, win_ref, convw_ref, convb_ref, poolw_ref,
                pscale_ref, wbc_ref, wbp_ref, wba_ref, gateb_ref, wo_ref,
                x2_ref, nconv_ref, npool_ref, zbuf, ubuf, *, rows, start_pos, conv_w, pool_w, att_w):
    s = pl.program_id(1)
    d_model = x_ref.shape[-1]
    z0 = SUBLANES
    u0 = 2 * SUBLANES

    @pl.when(s == 0)
    def _():
        zbuf[z0 - (CONV_K - 1):z0, :] = cprev_ref[0]
        ubuf[u0 - POOL_STATE:u0, :] = pprev_ref[0]

    x = x_ref[0]
    hb = _rms(x, gmix_ref[...]).astype(BF16)

    c0 = 0
    pa = _dot(hb, win_ref[:, c0:c0 + 3 * conv_w])
    xc, bg, cg = pa[:, :conv_w], pa[:, conv_w:2 * conv_w], pa[:, 2 * conv_w:]
    zbuf[z0:z0 + rows, :] = cg * xc
    cw = convw_ref[...]
    conv = zbuf[z0 - 2:z0 - 2 + rows, :] * cw[0:1]
    for k in range(1, CONV_K):
        conv = conv + zbuf[z0 - 2 + k:z0 - 2 + k + rows, :] * cw[k:k + 1]
    conv = conv + convb_ref[...]
    ya = _dot((bg * conv).astype(BF16), wbc_ref[...])
    last_z = zbuf[z0 + rows - (CONV_K - 1):z0 + rows, :]
    nconv_ref[0] = last_z
    zbuf[z0 - (CONV_K - 1):z0, :] = last_z
    c0 += 3 * conv_w

    up = _dot(hb, win_ref[:, c0:c0 + pool_w])
    ubuf[u0:u0 + rows, :] = up
    pos = start_pos + s * rows + lax.broadcasted_iota(jnp.int32, (rows, 1), 0)
    gw = pool_w // len(POOL_WINDOWS)
    ys = []
    for g, w in enumerate(POOL_WINDOWS):
        cur = up[:, g * gw:(g + 1) * gw]
        acc = cur
        for k in range(1, w):
            acc = acc + ubuf[u0 - k:u0 - k + rows, g * gw:(g + 1) * gw]
        cnt = jnp.minimum(pos + 1, w).astype(F32)
        ys.append(_dot((acc / cnt - cur).astype(BF16), poolw_ref[g]))
    yb = _dot((jnp.concatenate(ys, axis=-1) * pscale_ref[...]).astype(BF16), wbp_ref[...])
    last_u = ubuf[u0 + rows - POOL_STATE:u0 + rows, :]
    npool_ref[0] = last_u
    ubuf[u0 - POOL_STATE:u0, :] = last_u
    c0 += pool_w

    q = _dot(hb, win_ref[:, c0:c0 + att_w])
    kb = mk_ref[0].astype(BF16)
    vb = mv_ref[0].astype(BF16)
    hd = att_w // MEM_HEADS
    heads = []
    for h in range(MEM_HEADS):
        sc = _dot_nt(q[:, h * hd:(h + 1) * hd].astype(BF16), kb[:, h * hd:(h + 1) * hd]) * (hd ** -0.5)
        e = jnp.exp(sc - jnp.max(sc, axis=-1, keepdims=True))
        p = e / jnp.sum(e, axis=-1, keepdims=True)
        heads.append(_dot(p.astype(BF16), vb[:, h * hd:(h + 1) * hd]))
    yc = _dot(jnp.concatenate(heads, axis=-1).astype(BF16), wba_ref[...])
    c0 += att_w

    merged = None
    for i, y in enumerate((ya, yb, yc)):
        gl = _dot(hb, win_ref[:, c0 + i * d_model:c0 + (i + 1) * d_model]) + gateb_ref[:, i * d_model:(i + 1) * d_model]
        term = (1.0 / (1.0 + jnp.exp(-gl))) * y
        merged = term if merged is None else merged + term
    x2_ref[0] = x + _dot(merged.astype(BF16), wo_ref[...])


def _mixer(x, window, conv_prev, pool_prev, start_pos, mk, mv, lw):
    b0, bn, s0, seq = window
    d = x.shape[-1]
    conv_w, pool_w = conv_prev.shape[-1], pool_prev.shape[-1]
    n_mem, att_w = mk.shape[1], mk.shape[2]
    rows = min(MIXER_ROWS, seq)
    assert seq % rows == 0 and s0 % rows == 0 and rows % SUBLANES == 0 and rows >= POOL_STATE
    row0 = s0 // rows
    in_cols = lw["w_in"].shape[1]
    per_b = lambda shape: pl.BlockSpec((1,) + shape, lambda b, s: (b, 0, 0))
    body = functools.partial(_mixer_body, rows=rows, start_pos=start_pos, conv_w=conv_w, pool_w=pool_w, att_w=att_w)
    return pl.pallas_call(
        body,
        grid=(bn, seq // rows),
        in_specs=[
            pl.BlockSpec((1, rows, d), lambda b, s: (b0 + b, row0 + s, 0)),
            per_b((CONV_K - 1, conv_w)), per_b((POOL_STATE, pool_w)), per_b((n_mem, att_w)), per_b((n_mem, att_w)),
            _resident((1, d)), _resident((d, in_cols)), _resident((CONV_K, conv_w)), _resident((1, conv_w)),
            _resident(lw["pool_w"].shape), _resident((1, pool_w)), _resident((conv_w, d)), _resident((pool_w, d)),
            _resident((att_w, d)), _resident((1, 3 * d)), _resident((d, d)),
        ],
        out_specs=[
            pl.BlockSpec((1, rows, d), lambda b, s: (b, s, 0)),
            per_b((CONV_K - 1, conv_w)), per_b((POOL_STATE, pool_w)),
        ],
        out_shape=[
            jax.ShapeDtypeStruct((bn, seq, d), F32),
            jax.ShapeDtypeStruct((bn, CONV_K - 1, conv_w), F32),
            jax.ShapeDtypeStruct((bn, POOL_STATE, pool_w), F32),
        ],
        scratch_shapes=[pltpu.VMEM((SUBLANES + rows, conv_w), F32), pltpu.VMEM((2 * SUBLANES + rows, pool_w), F32)],
        compiler_params=pltpu.CompilerParams(dimension_semantics=("arbitrary", "arbitrary"), vmem_limit_bytes=VMEM_LIMIT),
        name="mixer",
    )(x, conv_prev, pool_prev, mk, mv, lw["g_mix"], lw["w_in"], lw["conv_w"], lw["conv_b"], lw["pool_w"],
      lw["pool_scale"], lw["w_bc"], lw["w_bp"], lw["w_ba"], lw["gate_b"], lw["w_o"])


_STAIR = [(a, b) for a in range(PEER_TOPK) for b in range(PEER_TOPK) if (a + 1) * (b + 1) <= PEER_TOPK]
_STAIR_HEAD = 24
_STAIR_TAIL = [_STAIR[_STAIR_HEAD + 8 * i:_STAIR_HEAD + 8 * (i + 1)] for i in range(4)]


def _top16(v, payload=None):
    n, t = v.shape
    rows = lax.broadcasted_iota(jnp.int32, (n, t), 0)
    out_rows = lax.broadcasted_iota(jnp.int32, (PEER_TOPK, t), 0)
    vals = jnp.zeros((PEER_TOPK, t), F32)
    pay = jnp.zeros((PEER_TOPK, t), jnp.int32)
    for k in range(PEER_TOPK):
        m = jnp.max(v, axis=0, keepdims=True)
        am = jnp.min(jnp.where(v == m, rows, n), axis=0, keepdims=True)
        hit = rows == am
        p = am if payload is None else jnp.sum(jnp.where(hit, payload, 0), axis=0, keepdims=True)
        vals = jnp.where(out_rows == k, m, vals)
        pay = jnp.where(out_rows == k, p, pay)
        v = jnp.where(hit, -jnp.inf, v)
    return vals, pay


def _take_rows(src, idxs):
    t = src.shape[1]
    out_rows = lax.broadcasted_iota(jnp.int32, (SUBLANES, t), 0)
    acc = jnp.broadcast_to(src[idxs[0]:idxs[0] + 1, :], (SUBLANES, t))
    for i in range(1, SUBLANES):
        acc = jnp.where(out_rows == i, src[idxs[i]:idxs[i] + 1, :], acc)
    return acc


def _stair(first, second, combine):
    blocks = [combine(first[0:1, :], second), combine(first[1:2, :], second[0:SUBLANES, :])]
    for blk in _STAIR_TAIL:
        blk = blk + [(0, 0)] * (SUBLANES - len(blk))
        blocks.append(combine(_take_rows(first, [a for a, _ in blk]), _take_rows(second, [b for _, b in blk])))
    return jnp.concatenate(blocks, axis=0)


def _route_body(x2_ref, gffn_ref, wq_ref, keys_ref, xn_ref, ids_ref, gate_ref, *, n_keys):
    xn = _rms(x2_ref[...], gffn_ref[...])
    xn_ref[...] = xn
    q = _dot(xn.astype(BF16), wq_ref[...])
    half = keys_ref.shape[-1]
    t = q.shape[0]
    for h in range(PEER_HEADS):
        scores = [_dot_nt(keys_ref[p], q[:, (2 * h + p) * half:(2 * h + p + 1) * half].astype(BF16)) for p in range(2)]
        for l0 in range(0, t, LANES):
            (s1, i1), (s2, i2) = [_top16(sc[:, l0:l0 + LANES]) for sc in scores]
            cand = _stair(s1, s2, lambda a, b: a + b)
            pad = lax.broadcasted_iota(jnp.int32, cand.shape, 0) >= len(_STAIR)
            expert = _stair(i1, i2, lambda a, b: a * n_keys + b)
            sc, e = _top16(jnp.where(pad, -jnp.inf, cand), expert)
            ex = jnp.exp(sc - sc[0:1, :])
            ids_ref[h * PEER_TOPK:(h + 1) * PEER_TOPK, l0:l0 + LANES] = e
            gate_ref[h * PEER_TOPK:(h + 1) * PEER_TOPK, l0:l0 + LANES] = ex / jnp.sum(ex, axis=0, keepdims=True)


def _route(x2, g_ffn, wq, keys):
    t, d = x2.shape
    rows = ROUTE_ROWS
    assert t % rows == 0
    picks = PEER_HEADS * PEER_TOPK
    return pl.pallas_call(
        functools.partial(_route_body, n_keys=keys.shape[1]),
        grid=(t // rows,),
        in_specs=[pl.BlockSpec((rows, d), lambda i: (i, 0)), _resident((1, d)), _resident(wq.shape), _resident(keys.shape)],
        out_specs=[pl.BlockSpec((rows, d), lambda i: (i, 0)), pl.BlockSpec((picks, rows), lambda i: (0, i)),
                   pl.BlockSpec((picks, rows), lambda i: (0, i))],
        out_shape=[jax.ShapeDtypeStruct((t, d), F32), jax.ShapeDtypeStruct((picks, t), jnp.int32),
                   jax.ShapeDtypeStruct((picks, t), F32)],
        compiler_params=pltpu.CompilerParams(dimension_semantics=("arbitrary",), vmem_limit_bytes=VMEM_LIMIT),
        name="route",
    )(x2, g_ffn, wq, keys)


def _gelu(h):
    return 0.5 * h * (1.0 + lax.erf(h * (2.0 ** -0.5)))


def _experts_body(ids_hbm, gate_ref, xn_ref, x2_ref, gfin_ref, uv_hbm, after_hbm, y_ref, *scratch,
                  rows, picks, first_step, steps, final_norm):
    del after_hbm
    bufs, (sem, ids_smem, ids_sem) = scratch[:GATHER_SLOTS], scratch[GATHER_SLOTS:]
    i = pl.program_id(0)
    half = i % 2
    n = rows * picks
    d = xn_ref.shape[-1]
    lead = GATHER_SLOTS - 1
    groups = picks // SUBLANES

    def ids_copy(step, which):
        return pltpu.make_async_copy(ids_hbm.at[first_step + step], ids_smem.at[pl.ds(which * n, n)], ids_sem.at[which])

    def issue(off, slot, j0, j1):
        for j in range(j0, j1):
            pltpu.async_copy(uv_hbm.at[ids_smem[off + j]], bufs[slot].at[pl.ds(j, 1)], sem.at[slot], priority=j % 2)

    def wait_slot(slot):
        pltpu.make_async_copy(bufs[slot], bufs[slot], sem.at[slot]).wait()

    @pl.when(i == 0)
    def _():
        ids_copy(0, 0).start()
        ids_copy(0, 0).wait()
        if steps > 1:
            ids_copy(1, 1).start()
        for s in range(lead):
            issue(s * picks, s, 0, picks)

    lane = lax.broadcasted_iota(jnp.int32, (picks, rows), 1)

    def token(t, slot, ahead_off):
        ahead_slot = (slot + lead) % GATHER_SLOTS
        buf = bufs[slot]
        wait_slot(slot)
        if ahead_off is not None:
            issue(ahead_off, ahead_slot, 0, picks)
        x = xn_ref[pl.ds(t, 1), :]
        g = jnp.sum(jnp.where(lane == t, gate_ref[...], 0.0), axis=-1, keepdims=True)
        hs = []
        for k in range(groups):
            r0 = k * SUBLANES
            u_rows = lax.bitcast_convert_type(buf[r0:r0 + SUBLANES, :] & HI_MASK, F32)
            hs.append(jnp.sum(u_rows * x, axis=-1, keepdims=True))
        w = g * _gelu(jnp.concatenate(hs, axis=0))
        acc = None
        for k in range(groups):
            r0 = k * SUBLANES
            v_rows = lax.bitcast_convert_type(buf[r0:r0 + SUBLANES, :] << 16, F32)
            part = w[r0:r0 + SUBLANES, :] * v_rows
            acc = part if acc is None else acc + part
        y = x2_ref[pl.ds(t, 1), :] + jnp.sum(acc, axis=0, keepdims=True)
        y_ref[pl.ds(t, 1), :] = _rms(y, gfin_ref[...]) if final_norm else y

    def four_tokens(k, carry):
        for s in range(GATHER_SLOTS):
            t = k * GATHER_SLOTS + s
            token(t, s, half * n + (t + lead) * picks)
        return carry

    lax.fori_loop(0, rows // GATHER_SLOTS - 1, four_tokens, 0)

    t0 = rows - GATHER_SLOTS
    token(t0, 0, half * n + (t0 + lead) * picks)

    @pl.when(i + 1 < steps)
    def _():
        ids_copy(i + 1, 1 - half).wait()

    for s in range(1, GATHER_SLOTS):
        @pl.when(i + 1 < steps)
        def _():
            issue((1 - half) * n + (s - 1) * picks, (s + lead) % GATHER_SLOTS, 0, picks)

        token(t0 + s, s, None)

    @pl.when(i + 2 < steps)
    def _():
        ids_copy(i + 2, half).start()


def _experts(ids_tok, gate_t, xn, x2, g_final, uv, final_norm, first_step, steps, after):
    t, picks = ids_tok.shape
    d = xn.shape[1]
    rows = EXPERT_ROWS
    assert t % rows == 0 and rows >= 2 * GATHER_SLOTS and rows % GATHER_SLOTS == 0 and picks % (2 * SUBLANES) == 0
    ids = ids_tok.reshape(t // rows, rows * picks)
    body = functools.partial(_experts_body, rows=rows, picks=picks, first_step=first_step, steps=steps,
                             final_norm=final_norm)
    return pl.pallas_call(
        body,
        grid=(steps,),
        in_specs=[
            pl.BlockSpec(memory_space=pl.ANY),
            pl.BlockSpec((picks, rows), lambda i: (0, first_step + i)),
            pl.BlockSpec((rows, d), lambda i: (first_step + i, 0)),
            pl.BlockSpec((rows, d), lambda i: (first_step + i, 0)),
            _resident((1, d)),
            pl.BlockSpec(memory_space=pl.ANY),
            pl.BlockSpec(memory_space=pl.ANY),
        ],
        out_specs=pl.BlockSpec((rows, d), lambda i: (i, 0)),
        out_shape=jax.ShapeDtypeStruct((steps * rows, d), F32),
        scratch_shapes=[pltpu.VMEM((picks, d), jnp.int32)] * GATHER_SLOTS + [
            pltpu.SemaphoreType.DMA((GATHER_SLOTS,)),
            pltpu.SMEM((2 * rows * picks,), jnp.int32),
            pltpu.SemaphoreType.DMA((2,)),
        ],
        compiler_params=pltpu.CompilerParams(dimension_semantics=("arbitrary",), vmem_limit_bytes=VMEM_LIMIT),
        name="experts",
    )(ids, gate_t, xn, x2, g_final, uv, after)


def _sc_token_pipeline(per_w, nch, wid, ids_hbm, vec_hbm, tab_hbm, out_hbm, idx_v, vec_v, rows_v, out_v,
                       gsem, isem, vsem, osem, compute):
    base = wid * per_w

    def ids_copy(t, p):
        return pltpu.make_async_copy(ids_hbm.at[t], idx_v.at[p], isem.at[p])

    def vec_copy(t, p):
        return pltpu.make_async_copy(vec_hbm.at[t], vec_v.at[p], vsem.at[p])

    def out_copy(t, p):
        return pltpu.make_async_copy(out_v.at[p], out_hbm.at[t], osem.at[p])

    def gather(p, c, b):
        return pltpu.make_async_copy(tab_hbm.at[idx_v.at[p, c]], rows_v.at[b], gsem.at[b])

    ids_copy(base, 0).start()
    vec_copy(base, 0).start()
    ids_copy(base, 0).wait()
    vec_copy(base, 0).wait()
    gather(0, 0, 0).start()

    def pair(ii, carry):
        for p in range(2):
            i = 2 * ii + p
            t = base + i
            has_next = i + 1 < per_w

            @pl.when(has_next)
            def _():
                ids_copy(t + 1, 1 - p).start()
                vec_copy(t + 1, 1 - p).start()

            @pl.when(i >= 2)
            def _():
                out_copy(t - 2, p).wait()

            for c in range(nch):
                b = c % 2
                if c + 1 < nch:
                    gather(p, c + 1, 1 - b).start()
                else:
                    @pl.when(has_next)
                    def _():
                        ids_copy(t + 1, 1 - p).wait()
                        vec_copy(t + 1, 1 - p).wait()
                        gather(1 - p, 0, 1 - b).start()
                gather(p, c, b).wait()
                compute(p, c, b)
            out_copy(t, p).start()
        return carry

    lax.fori_loop(0, per_w // 2, pair, 0)
    out_copy(base + per_w - 2, 0).wait()
    out_copy(base + per_w - 1, 1).wait()


HI_MASK = -65536


def _pack_pairs(tab):
    d = tab.shape[1]
    return _pack_halves(tab[:, :d // 2], tab[:, d // 2:])


def _pack_halves(hi, lo):
    bits = lambda a: lax.bitcast_convert_type(a.astype(jnp.bfloat16), jnp.uint16).astype(jnp.uint32)
    return lax.bitcast_convert_type((bits(hi) << 16) | bits(lo), jnp.int32)


def _unpack(words):
    hi = lax.bitcast_convert_type(words & HI_MASK, F32)
    lo = lax.bitcast_convert_type(words << 16, F32)
    return hi, lo


def _sc_dots(ids, xn, u2, n_tok):
    d = xn.shape[1]
    dw = u2.shape[1]
    nch = ids.shape[1]
    picks = nch * SC_CHUNK
    per_w = n_tok // SC_WORKERS
    assert per_w % 2 == 0 and nch % 2 == 0 and 2 * dw == d
    mesh = plsc.VectorSubcoreMesh(core_axis_name="c", subcore_axis_name="s")

    @functools.partial(
        pl.kernel, mesh=mesh, out_type=jax.ShapeDtypeStruct((n_tok, picks), F32),
        scratch_types=[pltpu.VMEM((2, nch, SC_CHUNK), jnp.int32), pltpu.VMEM((2, d), F32),
                       pltpu.VMEM((2, SC_CHUNK, dw), jnp.int32), pltpu.VMEM((2, picks), F32)]
        + [pltpu.SemaphoreType.DMA((2,))] * 4,
        compiler_params=pltpu.CompilerParams(needs_layout_passes=False), name="sc_dots")
    def k(ids_hbm, xn_hbm, u_hbm, h_hbm, idx_v, x_v, rows_v, h_v, gsem, isem, vsem, osem):
        wid = lax.axis_index("s") * SC_CORES + lax.axis_index("c")
        lane = lax.iota(jnp.int32, SC_LANES)

        def compute(p, c, b):
            for g in range(SC_CHUNK // SC_LANES):
                def rows4(q, hv):
                    j0 = g * SC_LANES + q * 4

                    def span(cc, accs):
                        for uu in range(SC_UNROLL):
                            off = pl.multiple_of((cc * SC_UNROLL + uu) * SC_LANES, SC_LANES)
                            x_hi = x_v[p, pl.ds(off, SC_LANES)]
                            x_lo = x_v[p, pl.ds(dw + off, SC_LANES)]
                            nxt = []
                            for kk in range(4):
                                hi, lo = _unpack(rows_v[b, j0 + kk, pl.ds(off, SC_LANES)])
                                nxt.append((accs[2 * kk] + hi * x_hi, accs[2 * kk + 1] + lo * x_lo))
                            accs = tuple(a for pair in nxt for a in pair)
                        return accs

                    accs = lax.fori_loop(0, dw // (SC_LANES * SC_UNROLL), span,
                                         tuple(jnp.zeros((SC_LANES,), F32) for _ in range(8)))
                    for kk in range(4):
                        hv = jnp.where(lane == q * 4 + kk, jnp.sum(accs[2 * kk] + accs[2 * kk + 1]), hv)
                    return hv

                hv = lax.fori_loop(0, SC_LANES // 4, rows4, jnp.zeros((SC_LANES,), F32))
                h_v[p, pl.ds(c * SC_CHUNK + g * SC_LANES, SC_LANES)] = hv

        _sc_token_pipeline(per_w, nch, wid, ids_hbm, xn_hbm, u_hbm, h_hbm, idx_v, x_v, rows_v, h_v,
                           gsem, isem, vsem, osem, compute)

    return k(ids, xn, u2)


def _sc_mix(ids, w, v2, n_tok):
    dw = v2.shape[1]
    d = 2 * dw
    nch = ids.shape[1]
    picks = nch * SC_CHUNK
    per_w = n_tok // SC_WORKERS
    assert per_w % 2 == 0 and nch % 2 == 0
    cb = 8
    mesh = plsc.VectorSubcoreMesh(core_axis_name="c", subcore_axis_name="s")

    @functools.partial(
        pl.kernel, mesh=mesh, out_type=jax.ShapeDtypeStruct((n_tok, d), F32),
        scratch_types=[pltpu.VMEM((2, nch, SC_CHUNK), jnp.int32), pltpu.VMEM((2, picks), F32),
                       pltpu.VMEM((2, SC_CHUNK, dw), jnp.int32), pltpu.VMEM((2, d), F32)]
        + [pltpu.SemaphoreType.DMA((2,))] * 4,
        compiler_params=pltpu.CompilerParams(needs_layout_passes=False), name="sc_mix")
    def k(ids_hbm, w_hbm, v_hbm, o_hbm, idx_v, w_v, rows_v, o_v, gsem, isem, vsem, osem):
        wid = lax.axis_index("s") * SC_CORES + lax.axis_index("c")

        def compute(p, c, b):
            for blk in range(dw // (cb * SC_LANES)):
                base = blk * cb * SC_LANES
                offs = [base + kk * SC_LANES for kk in range(cb)] + [dw + base + kk * SC_LANES for kk in range(cb)]
                if c == 0:
                    init = tuple(jnp.zeros((SC_LANES,), F32) for _ in offs)
                else:
                    init = tuple(o_v[p, pl.ds(o, SC_LANES)] for o in offs)

                def row(r, accs):
                    wj = plsc.load_gather(w_v.at[p], [jnp.full((SC_LANES,), c * SC_CHUNK, jnp.int32) + r])
                    his, los = [], []
                    for kk in range(cb):
                        hi, lo = _unpack(rows_v[b, r, pl.ds(base + kk * SC_LANES, SC_LANES)])
                        his.append(accs[kk] + wj * hi)
                        los.append(accs[cb + kk] + wj * lo)
                    return tuple(his + los)

                accs = lax.fori_loop(0, SC_CHUNK, row, init)
                for o, a in zip(offs, accs):
                    o_v[p, pl.ds(o, SC_LANES)] = a

        _sc_token_pipeline(per_w, nch, wid, ids_hbm, w_hbm, v_hbm, o_hbm, idx_v, w_v, rows_v, o_v,
                           gsem, isem, vsem, osem, compute)

    return k(ids, w, v2)


def _gate_gelu_body(h_ref, gate_ref, after_hbm, w_ref):
    del after_hbm
    w_ref[...] = gate_ref[...] * _gelu(h_ref[...])


def _gate_gelu(h, gate_tok, after):
    n, picks = h.shape
    rows = math.gcd(n, 1024)
    assert n % rows == 0
    return pl.pallas_call(
        _gate_gelu_body, grid=(n // rows,),
        in_specs=[pl.BlockSpec((rows, picks), lambda i: (i, 0)), pl.BlockSpec((rows, picks), lambda i: (i, 0)),
                  pl.BlockSpec(memory_space=pl.ANY)],
        out_specs=pl.BlockSpec((rows, picks), lambda i: (i, 0)),
        out_shape=jax.ShapeDtypeStruct((n, picks), F32), name="gate_gelu",
    )(h, gate_tok, after)


def _finish_body(x2_ref, o_ref, gfin_ref, y_ref, *, final_norm):
    y = x2_ref[...] + o_ref[...]
    y_ref[...] = _rms(y, gfin_ref[...]) if final_norm else y


def _finish(x2, o, g_final, final_norm):
    n, d = o.shape
    rows = math.gcd(n, 512)
    assert n % rows == 0
    return pl.pallas_call(
        functools.partial(_finish_body, final_norm=final_norm), grid=(n // rows,),
        in_specs=[pl.BlockSpec((rows, d), lambda i: (i, 0)), pl.BlockSpec((rows, d), lambda i: (i, 0)), _resident((1, d))],
        out_specs=pl.BlockSpec((rows, d), lambda i: (i, 0)),
        out_shape=jax.ShapeDtypeStruct((n, d), F32), name="finish",
    )(x2, o, g_final)


def _peer(ids_t, gate_t, xn, x2, g_final, lw, final_norm):
    picks, t = ids_t.shape
    ids_tok = ids_t.T
    total_steps = t // EXPERT_ROWS
    sc_steps = int(total_steps * SC_SHARE) if t >= SC_MIN_TOKENS else 0
    n_sc = sc_steps * EXPERT_ROWS
    if n_sc == 0:
        return [_experts(ids_tok, gate_t, xn, x2, g_final, lw["peer_uv"], final_norm, 0, total_steps, g_final)]
    assert n_sc % SC_WORKERS == 0 and picks % SC_CHUNK == 0
    tc_steps = total_steps - sc_steps
    first = int(tc_steps * TC_FIRST_SHARE)
    ids_sc = ids_tok.reshape(t, picks // SC_CHUNK, SC_CHUNK)
    h = _sc_dots(ids_sc, xn, lw["peer_u2"], n_sc)
    y_tc1 = _experts(ids_tok, gate_t, xn, x2, g_final, lw["peer_uv"], final_norm, sc_steps, first, g_final)
    w = _gate_gelu(h, gate_t.T, y_tc1)
    o = _sc_mix(ids_sc, w, lw["peer_v2"], n_sc)
    y_tc2 = _experts(ids_tok, gate_t, xn, x2, g_final, lw["peer_uv"], final_norm, sc_steps + first, tc_steps - first, w)
    y_sc = _finish(x2, o, g_final, final_norm)
    return [y_sc, y_tc1, y_tc2]


def _layer(x, window, conv_prev, pool_prev, start_pos, mk, mv, lw, g_final, final_norm):
    _, bn, _, seq = window
    d = x.shape[-1]
    x2, new_conv, new_pool = _mixer(x, window, conv_prev, pool_prev, start_pos, mk, mv, lw)
    x2 = x2.reshape(bn * seq, d)
    xn, ids_t, gate_t = _route(x2, lw["g_ffn"], lw["peer_wq"], lw["peer_keys"])
    y = _peer(ids_t, gate_t, xn, x2, g_final, lw, final_norm)
    return y, new_conv, new_pool


def kernel(x_prompt, x_sample, mem_prompt, cache_conv, cache_pool, cache_mem_k, cache_mem_v, g_mix, w_in, conv_w, conv_b, pool_w, pool_scale, g_mem, w_mk, w_mv, w_bc, w_bp, w_ba, gate_b, w_o, g_ffn, peer_wq, peer_keys, peer_u, peer_v, g_final):
    depth, d = g_mix.shape
    xp, xs = x_prompt, x_sample
    bp, bs = xp.shape[0], xs.shape[0]
    n_exp = peer_u.shape[1]
    gfin = g_final.reshape(1, d)
    conv_p, pool_p, mk_p, mv_p, conv_s, pool_s = [], [], [], [], [], []
    for l in range(depth):
        lw = dict(
            g_mix=g_mix[l].reshape(1, d), w_in=w_in[l].astype(BF16), conv_w=conv_w[l], conv_b=conv_b[l].reshape(1, -1),
            pool_w=pool_w[l].astype(BF16), pool_scale=pool_scale[l].reshape(1, -1), w_bc=w_bc[l].astype(BF16),
            w_bp=w_bp[l].astype(BF16), w_ba=w_ba[l].astype(BF16), gate_b=gate_b[l].reshape(1, -1), w_o=w_o[l].astype(BF16),
            g_ffn=g_ffn[l].reshape(1, d), peer_wq=peer_wq[l].astype(BF16), peer_keys=peer_keys[l].astype(BF16),
            peer_uv=_pack_halves(peer_u[l], peer_v[l]).reshape(n_exp, 1, d),
            peer_u2=_pack_pairs(peer_u[l]), peer_v2=_pack_pairs(peer_v[l]),
        )
        last = l == depth - 1
        mk, mv = _mem_kv(mem_prompt, g_mem[l], w_mk[l], w_mv[l])
        zc = jnp.zeros((bp, CONV_K - 1, conv_w.shape[-1]), xp.dtype)
        zp = jnp.zeros((bp, POOL_STATE, pool_scale.shape[-1]), xp.dtype)
        seq = xp.shape[1]
        seg = seq // PROMPT_SEGMENTS if seq % PROMPT_SEGMENTS == 0 and seq // PROMPT_SEGMENTS >= SC_MIN_TOKENS else seq
        pieces, cps, pps = [], [], []
        for b in range(bp):
            cprev, pprev = zc[b:b + 1], zp[b:b + 1]
            for s0 in range(0, seq, seg):
                ys, cprev, pprev = _layer(xp, (b, 1, s0, seg), cprev, pprev, s0, mk[b:b + 1], mv[b:b + 1], lw, gfin, last)
                pieces += ys
            cps.append(cprev); pps.append(pprev)
        xp = jnp.concatenate(pieces, axis=0).reshape(bp, seq, d)
        cp, pp = jnp.concatenate(cps, axis=0), jnp.concatenate(pps, axis=0)
        n_mem = cache_mem_k.shape[2]
        ys, cs, ps = _layer(xs, (0, bs, 0, xs.shape[1]), cache_conv[l], cache_pool[l], PAST_LEN, cache_mem_k[l].reshape(bs, n_mem, -1),
                            cache_mem_v[l].reshape(bs, n_mem, -1), lw, gfin, last)
        xs = jnp.concatenate(ys, axis=0).reshape(xs.shape)
        heads_shape = (bp, n_mem) + cache_mem_k.shape[3:]
        conv_p.append(cp); pool_p.append(pp); mk_p.append(mk.reshape(heads_shape)); mv_p.append(mv.reshape(heads_shape))
        conv_s.append(cs); pool_s.append(ps)
    return (xp, xs, jnp.stack(conv_p), jnp.stack(pool_p), jnp.stack(mk_p), jnp.stack(mv_p),
            jnp.stack(conv_s), jnp.stack(pool_s))
```

```python
import functools
import math

import jax
import jax.numpy as jnp
from jax import lax
from jax.experimental import pallas as pl
from jax.experimental.pallas import tpu as pltpu
from jax.experimental.pallas import tpu_sc as plsc

F32 = jnp.float32
BF16 = jnp.bfloat16
EPS = 1e-6

CONV_K = 3
POOL_WINDOWS = (2, 4, 8, 16)
POOL_STATE = max(POOL_WINDOWS) - 1
PAST_LEN = 1024
MEM_HEADS = 4
PEER_HEADS = 8
PEER_TOPK = 16

SUBLANES = 8
LANES = 128
MIXER_ROWS = 512
ROUTE_ROWS = 256
EXPERT_ROWS = 128
GATHER_SLOTS = 8
VMEM_LIMIT = 56 * 1024 * 1024

SC_CORES = 2
SC_SUBCORES = 16
SC_WORKERS = SC_CORES * SC_SUBCORES
SC_LANES = 16
SC_CHUNK = 64
SC_UNROLL = 16
SC_SHARE = 0.69
SC_MIN_TOKENS = 8192
PROMPT_SEGMENTS = 2
TC_FIRST_SHARE = 0.56


def _rms(x, g):
    return x * lax.rsqrt(jnp.mean(x * x, axis=-1, keepdims=True) + EPS) * g


def _dot(a, b):
    return jnp.dot(a, b, preferred_element_type=F32)


def _dot_nt(a, b):
    return lax.dot_general(a, b, (((1,), (1,)), ((), ())), preferred_element_type=F32)


def _resident(shape):
    zeros = (0,) * len(shape)
    return pl.BlockSpec(shape, lambda *_: zeros, pipeline_mode=pl.Buffered(1))


def _memkv_body(mem_ref, g_ref, wk_ref, wv_ref, k_ref, v_ref):
    m = _rms(mem_ref[...], g_ref[...]).astype(BF16)
    k_ref[...] = _dot(m, wk_ref[...])
    v_ref[...] = _dot(m, wv_ref[...])


def _mem_kv(mem, g_mem, w_mk, w_mv):
    bn, n_mem, d = mem.shape
    att_w = w_mk.shape[1]
    k, v = pl.pallas_call(
        _memkv_body,
        out_shape=[jax.ShapeDtypeStruct((bn * n_mem, att_w), F32)] * 2,
        name="mem_kv",
    )(mem.reshape(bn * n_mem, d), g_mem.reshape(1, d), w_mk.astype(BF16), w_mv.astype(BF16))
    return k.reshape(bn, n_mem, att_w), v.reshape(bn, n_mem, att_w)


def _mixer_body(x_ref, cprev_ref, pprev_ref, mk_ref, mv_ref, gmix_ref, win_ref, convw_ref, convb_ref, poolw_ref,
                pscale_ref, wbc_ref, wbp_ref, wba_ref, gateb_ref, wo_ref,
                x2_ref, nconv_ref, npool_ref, zbuf, ubuf, *, rows, start_pos, conv_w, pool_w, att_w):
    s = pl.program_id(1)
    d_model = x_ref.shape[-1]
    z0 = SUBLANES
    u0 = 2 * SUBLANES

    @pl.when(s == 0)
    def _():
        zbuf[z0 - (CONV_K - 1):z0, :] = cprev_ref[0]
        ubuf[u0 - POOL_STATE:u0, :] = pprev_ref[0]

    x = x_ref[0]
    hb = _rms(x, gmix_ref[...]).astype(BF16)

    c0 = 0
    pa = _dot(hb, win_ref[:, c0:c0 + 3 * conv_w])
    xc, bg, cg = pa[:, :conv_w], pa[:, conv_w:2 * conv_w], pa[:, 2 * conv_w:]
    zbuf[z0:z0 + rows, :] = cg * xc
    cw = convw_ref[...]
    conv = zbuf[z0 - 2:z0 - 2 + rows, :] * cw[0:1]
    for k in range(1, CONV_K):
        conv = conv + zbuf[z0 - 2 + k:z0 - 2 + k + rows, :] * cw[k:k + 1]
    conv = conv + convb_ref[...]
    ya = _dot((bg * conv).astype(BF16), wbc_ref[...])
    last_z = zbuf[z0 + rows - (CONV_K - 1):z0 + rows, :]
    nconv_ref[0] = last_z
    zbuf[z0 - (CONV_K - 1):z0, :] = last_z
    c0 += 3 * conv_w

    up = _dot(hb, win_ref[:, c0:c0 + pool_w])
    ubuf[u0:u0 + rows, :] = up
    pos = start_pos + s * rows + lax.broadcasted_iota(jnp.int32, (rows, 1), 0)
    gw = pool_w // len(POOL_WINDOWS)
    ys = []
    for g, w in enumerate(POOL_WINDOWS):
        cur = up[:, g * gw:(g + 1) * gw]
        acc = cur
        for k in range(1, w):
            acc = acc + ubuf[u0 - k:u0 - k + rows, g * gw:(g + 1) * gw]
        cnt = jnp.minimum(pos + 1, w).astype(F32)
        ys.append(_dot((acc / cnt - cur).astype(BF16), poolw_ref[g]))
    yb = _dot((jnp.concatenate(ys, axis=-1) * pscale_ref[...]).astype(BF16), wbp_ref[...])
    last_u = ubuf[u0 + rows - POOL_STATE:u0 + rows, :]
    npool_ref[0] = last_u
    ubuf[u0 - POOL_STATE:u0, :] = last_u
    c0 += pool_w

    q = _dot(hb, win_ref[:, c0:c0 + att_w])
    kb = mk_ref[0].astype(BF16)
    vb = mv_ref[0].astype(BF16)
    hd = att_w // MEM_HEADS
    heads = []
    for h in range(MEM_HEADS):
        sc = _dot_nt(q[:, h * hd:(h + 1) * hd].astype(BF16), kb[:, h * hd:(h + 1) * hd]) * (hd ** -0.5)
        e = jnp.exp(sc - jnp.max(sc, axis=-1, keepdims=True))
        p = e / jnp.sum(e, axis=-1, keepdims=True)
        heads.append(_dot(p.astype(BF16), vb[:, h * hd:(h + 1) * hd]))
    yc = _dot(jnp.concatenate(heads, axis=-1).astype(BF16), wba_ref[...])
    c0 += att_w

    merged = None
    for i, y in enumerate((ya, yb, yc)):
        gl = _dot(hb, win_ref[:, c0 + i * d_model:c0 + (i + 1) * d_model]) + gateb_ref[:, i * d_model:(i + 1) * d_model]
        term = (1.0 / (1.0 + jnp.exp(-gl))) * y
        merged = term if merged is None else merged + term
    x2_ref[0] = x + _dot(merged.astype(BF16), wo_ref[...])


def _mixer(x, window, conv_prev, pool_prev, start_pos, mk, mv, lw):
    b0, bn, s0, seq = window
    d = x.shape[-1]
    conv_w, pool_w = conv_prev.shape[-1], pool_prev.shape[-1]
    n_mem, att_w = mk.shape[1], mk.shape[2]
    rows = min(MIXER_ROWS, seq)
    assert seq % rows == 0 and s0 % rows == 0 and rows % SUBLANES == 0 and rows >= POOL_STATE
    row0 = s0 // rows
    in_cols = lw["w_in"].shape[1]
    per_b = lambda shape: pl.BlockSpec((1,) + shape, lambda b, s: (b, 0, 0))
    body = functools.partial(_mixer_body, rows=rows, start_pos=start_pos, conv_w=conv_w, pool_w=pool_w, att_w=att_w)
    return pl.pallas_call(
        body,
        grid=(bn, seq // rows),
        in_specs=[
            pl.BlockSpec((1, rows, d), lambda b, s: (b0 + b, row0 + s, 0)),
            per_b((CONV_K - 1, conv_w)), per_b((POOL_STATE, pool_w)), per_b((n_mem, att_w)), per_b((n_mem, att_w)),
            _resident((1, d)), _resident((d, in_cols)), _resident((CONV_K, conv_w)), _resident((1, conv_w)),
            _resident(lw["pool_w"].shape), _resident((1, pool_w)), _resident((conv_w, d)), _resident((pool_w, d)),
            _resident((att_w, d)), _resident((1, 3 * d)), _resident((d, d)),
        ],
        out_specs=[
            pl.BlockSpec((1, rows, d), lambda b, s: (b, s, 0)),
            per_b((CONV_K - 1, conv_w)), per_b((POOL_STATE, pool_w)),
        ],
        out_shape=[
            jax.ShapeDtypeStruct((bn, seq, d), F32),
            jax.ShapeDtypeStruct((bn, CONV_K - 1, conv_w), F32),
            jax.ShapeDtypeStruct((bn, POOL_STATE, pool_w), F32),
        ],
        scratch_shapes=[pltpu.VMEM((SUBLANES + rows, conv_w), F32), pltpu.VMEM((2 * SUBLANES + rows, pool_w), F32)],
        compiler_params=pltpu.CompilerParams(dimension_semantics=("arbitrary", "arbitrary"), vmem_limit_bytes=VMEM_LIMIT),
        name="mixer",
    )(x, conv_prev, pool_prev, mk, mv, lw["g_mix"], lw["w_in"], lw["conv_w"], lw["conv_b"], lw["pool_w"],
      lw["pool_scale"], lw["w_bc"], lw["w_bp"], lw["w_ba"], lw["gate_b"], lw["w_o"])


_STAIR = [(a, b) for a in range(PEER_TOPK) for b in range(PEER_TOPK) if (a + 1) * (b + 1) <= PEER_TOPK]
_STAIR_HEAD = 24
_STAIR_TAIL = [_STAIR[_STAIR_HEAD + 8 * i:_STAIR_HEAD + 8 * (i + 1)] for i in range(4)]


def _top16(v, payload=None):
    n, t = v.shape
    rows = lax.broadcasted_iota(jnp.int32, (n, t), 0)
    out_rows = lax.broadcasted_iota(jnp.int32, (PEER_TOPK, t), 0)
    vals = jnp.zeros((PEER_TOPK, t), F32)
    pay = jnp.zeros((PEER_TOPK, t), jnp.int32)
    for k in range(PEER_TOPK):
        m = jnp.max(v, axis=0, keepdims=True)
        am = jnp.min(jnp.where(v == m, rows, n), axis=0, keepdims=True)
        hit = rows == am
        p = am if payload is None else jnp.sum(jnp.where(hit, payload, 0), axis=0, keepdims=True)
        vals = jnp.where(out_rows == k, m, vals)
        pay = jnp.where(out_rows == k, p, pay)
        v = jnp.where(hit, -jnp.inf, v)
    return vals, pay


def _take_rows(src, idxs):
    t = src.shape[1]
    out_rows = lax.broadcasted_iota(jnp.int32, (SUBLANES, t), 0)
    acc = jnp.broadcast_to(src[idxs[0]:idxs[0] + 1, :], (SUBLANES, t))
    for i in range(1, SUBLANES):
        acc = jnp.where(out_rows == i, src[idxs[i]:idxs[i] + 1, :], acc)
    return acc


def _stair(first, second, combine):
    blocks = [combine(first[0:1, :], second), combine(first[1:2, :], second[0:SUBLANES, :])]
    for blk in _STAIR_TAIL:
        blk = blk + [(0, 0)] * (SUBLANES - len(blk))
        blocks.append(combine(_take_rows(first, [a for a, _ in blk]), _take_rows(second, [b for _, b in blk])))
    return jnp.concatenate(blocks, axis=0)


def _route_body(x2_ref, gffn_ref, wq_ref, keys_ref, xn_ref, ids_ref, gate_ref, *, n_keys):
    xn = _rms(x2_ref[...], gffn_ref[...])
    xn_ref[...] = xn
    q = _dot(xn.astype(BF16), wq_ref[...])
    half = keys_ref.shape[-1]
    t = q.shape[0]
    for h in range(PEER_HEADS):
        scores = [_dot_nt(keys_ref[p], q[:, (2 * h + p) * half:(2 * h + p + 1) * half].astype(BF16)) for p in range(2)]
        for l0 in range(0, t, LANES):
            (s1, i1), (s2, i2) = [_top16(sc[:, l0:l0 + LANES]) for sc in scores]
            cand = _stair(s1, s2, lambda a, b: a + b)
            pad = lax.broadcasted_iota(jnp.int32, cand.shape, 0) >= len(_STAIR)
            expert = _stair(i1, i2, lambda a, b: a * n_keys + b)
            sc, e = _top16(jnp.where(pad, -jnp.inf, cand), expert)
            ex = jnp.exp(sc - sc[0:1, :])
            ids_ref[h * PEER_TOPK:(h + 1) * PEER_TOPK, l0:l0 + LANES] = e
            gate_ref[h * PEER_TOPK:(h + 1) * PEER_TOPK, l0:l0 + LANES] = ex / jnp.sum(ex, axis=0, keepdims=True)


def _route(x2, g_ffn, wq, keys):
    t, d = x2.shape
    rows = ROUTE_ROWS
    assert t % rows == 0
    picks = PEER_HEADS * PEER_TOPK
    return pl.pallas_call(
        functools.partial(_route_body, n_keys=keys.shape[1]),
        grid=(t // rows,),
        in_specs=[pl.BlockSpec((rows, d), lambda i: (i, 0)), _resident((1, d)), _resident(wq.shape), _resident(keys.shape)],
        out_specs=[pl.BlockSpec((rows, d), lambda i: (i, 0)), pl.BlockSpec((picks, rows), lambda i: (0, i)),
                   pl.BlockSpec((picks, rows), lambda i: (0, i))],
        out_shape=[jax.ShapeDtypeStruct((t, d), F32), jax.ShapeDtypeStruct((picks, t), jnp.int32),
                   jax.ShapeDtypeStruct((picks, t), F32)],
        compiler_params=pltpu.CompilerParams(dimension_semantics=("arbitrary",), vmem_limit_bytes=VMEM_LIMIT),
        name="route",
    )(x2, g_ffn, wq, keys)


def _gelu(h):
    return 0.5 * h * (1.0 + lax.erf(h * (2.0 ** -0.5)))


def _experts_body(ids_hbm, gate_ref, xn_ref, x2_ref, gfin_ref, uv_hbm, after_hbm, y_ref, *scratch,
                  rows, picks, first_step, steps, final_norm):
    del after_hbm
    bufs, (sem, ids_smem, ids_sem) = scratch[:GATHER_SLOTS], scratch[GATHER_SLOTS:]
    i = pl.program_id(0)
    half = i % 2
    n = rows * picks
    d = xn_ref.shape[-1]
    lead = GATHER_SLOTS - 2
    groups = picks // SUBLANES

    def ids_copy(step, which):
        return pltpu.make_async_copy(ids_hbm.at[first_step + step], ids_smem.at[pl.ds(which * n, n)], ids_sem.at[which])

    def issue(off, slot, j0, j1):
        for j in range(j0, j1):
            pltpu.async_copy(uv_hbm.at[ids_smem[off + j]], bufs[slot].at[pl.ds(j, 1)], sem.at[slot], priority=j % 2)

    def wait_slot(slot):
        pltpu.make_async_copy(bufs[slot], bufs[slot], sem.at[slot]).wait()

    @pl.when(i == 0)
    def _():
        ids_copy(0, 0).start()
        ids_copy(0, 0).wait()
        if steps > 1:
            ids_copy(1, 1).start()
        for s in range(lead):
            issue(s * picks, s, 0, picks)

    lane = lax.broadcasted_iota(jnp.int32, (picks, rows), 1)

    def weighted_sum(t, slot, w):
        buf = bufs[slot]
        acc = None
        for k in range(groups):
            r0 = k * SUBLANES
            v_rows = lax.bitcast_convert_type(buf[r0:r0 + SUBLANES, :] << 16, F32)
            part = w[r0:r0 + SUBLANES, :] * v_rows
            acc = part if acc is None else acc + part
        y = x2_ref[pl.ds(t, 1), :] + jnp.sum(acc, axis=0, keepdims=True)
        y_ref[pl.ds(t, 1), :] = _rms(y, gfin_ref[...]) if final_norm else y

    def token(t, slot, ahead_off, w_prev):
        buf = bufs[slot]
        wait_slot(slot)
        x = xn_ref[pl.ds(t, 1), :]
        g = jnp.sum(jnp.where(lane == t, gate_ref[...], 0.0), axis=-1, keepdims=True)
        hs = []
        for k in range(groups):
            r0 = k * SUBLANES
            u_rows = lax.bitcast_convert_type(buf[r0:r0 + SUBLANES, :] & HI_MASK, F32)
            hs.append(jnp.sum(u_rows * x, axis=-1, keepdims=True))
        if w_prev is not None:
            weighted_sum(t - 1, (slot - 1) % GATHER_SLOTS, w_prev)
        if ahead_off is not None:
            issue(ahead_off, (slot + lead) % GATHER_SLOTS, 0, picks)
        return g * _gelu(jnp.concatenate(hs, axis=0))

    w = None
    for s in range(GATHER_SLOTS):
        w = token(s, s, half * n + (s + lead) * picks, w)

    def slot_round(k, w):
        for s in range(GATHER_SLOTS):
            t = k * GATHER_SLOTS + s
            w = token(t, s, half * n + (t + lead) * picks, w)
        return w

    w = lax.fori_loop(1, rows // GATHER_SLOTS - 1, slot_round, w)

    t0 = rows - GATHER_SLOTS
    own = GATHER_SLOTS - lead
    for s in range(own):
        w = token(t0 + s, s, half * n + (t0 + s + lead) * picks, w)

    @pl.when(i + 1 < steps)
    def _():
        ids_copy(i + 1, 1 - half).wait()

    for s in range(own, GATHER_SLOTS):
        @pl.when(i + 1 < steps)
        def _():
            issue((1 - half) * n + (s - own) * picks, (s + lead) % GATHER_SLOTS, 0, picks)

        w = token(t0 + s, s, None, w)
    weighted_sum(rows - 1, GATHER_SLOTS - 1, w)

    @pl.when(i + 2 < steps)
    def _():
        ids_copy(i + 2, half).start()


def _experts(ids_tok, gate_t, xn, x2, g_final, uv, final_norm, first_step, steps, after):
    t, picks = ids_tok.shape
    d = xn.shape[1]
    rows = EXPERT_ROWS
    assert t % rows == 0 and rows >= 2 * GATHER_SLOTS and rows % GATHER_SLOTS == 0 and picks % (2 * SUBLANES) == 0
    ids = ids_tok.reshape(t // rows, rows * picks)
    body = functools.partial(_experts_body, rows=rows, picks=picks, first_step=first_step, steps=steps,
                             final_norm=final_norm)
    return pl.pallas_call(
        body,
        grid=(steps,),
        in_specs=[
            pl.BlockSpec(memory_space=pl.ANY),
            pl.BlockSpec((picks, rows), lambda i: (0, first_step + i)),
            pl.BlockSpec((rows, d), lambda i: (first_step + i, 0)),
            pl.BlockSpec((rows, d), lambda i: (first_step + i, 0)),
            _resident((1, d)),
            pl.BlockSpec(memory_space=pl.ANY),
            pl.BlockSpec(memory_space=pl.ANY),
        ],
        out_specs=pl.BlockSpec((rows, d), lambda i: (i, 0)),
        out_shape=jax.ShapeDtypeStruct((steps * rows, d), F32),
        scratch_shapes=[pltpu.VMEM((picks, d), jnp.int32)] * GATHER_SLOTS + [
            pltpu.SemaphoreType.DMA((GATHER_SLOTS,)),
            pltpu.SMEM((2 * rows * picks,), jnp.int32),
            pltpu.SemaphoreType.DMA((2,)),
        ],
        compiler_params=pltpu.CompilerParams(dimension_semantics=("arbitrary",), vmem_limit_bytes=VMEM_LIMIT),
        name="experts",
    )(ids, gate_t, xn, x2, g_final, uv, after)


def _sc_token_pipeline(per_w, nch, wid, ids_hbm, vec_hbm, tab_hbm, out_hbm, idx_v, vec_v, rows_v, out_v,
                       gsem, isem, vsem, osem, compute):
    base = wid * per_w

    def ids_copy(t, p):
        return pltpu.make_async_copy(ids_hbm.at[t], idx_v.at[p], isem.at[p])

    def vec_copy(t, p):
        return pltpu.make_async_copy(vec_hbm.at[t], vec_v.at[p], vsem.at[p])

    def out_copy(t, p):
        return pltpu.make_async_copy(out_v.at[p], out_hbm.at[t], osem.at[p])

    def gather(p, c, b):
        return pltpu.make_async_copy(tab_hbm.at[idx_v.at[p, c]], rows_v.at[b], gsem.at[b])

    ids_copy(base, 0).start()
    vec_copy(base, 0).start()
    ids_copy(base, 0).wait()
    vec_copy(base, 0).wait()
    gather(0, 0, 0).start()

    def pair(ii, carry):
        for p in range(2):
            i = 2 * ii + p
            t = base + i
            has_next = i + 1 < per_w

            @pl.when(has_next)
            def _():
                ids_copy(t + 1, 1 - p).start()
                vec_copy(t + 1, 1 - p).start()

            @pl.when(i >= 2)
            def _():
                out_copy(t - 2, p).wait()

            for c in range(nch):
                b = c % 2
                if c + 1 < nch:
                    gather(p, c + 1, 1 - b).start()
                else:
                    @pl.when(has_next)
                    def _():
                        ids_copy(t + 1, 1 - p).wait()
                        vec_copy(t + 1, 1 - p).wait()
                        gather(1 - p, 0, 1 - b).start()
                gather(p, c, b).wait()
                compute(p, c, b)
            out_copy(t, p).start()
        return carry

    lax.fori_loop(0, per_w // 2, pair, 0)
    out_copy(base + per_w - 2, 0).wait()
    out_copy(base + per_w - 1, 1).wait()


HI_MASK = -65536


def _pack_pairs(tab):
    d = tab.shape[1]
    return _pack_halves(tab[:, :d // 2], tab[:, d // 2:])


def _pack_halves(hi, lo):
    bits = lambda a: lax.bitcast_convert_type(a.astype(jnp.bfloat16), jnp.uint16).astype(jnp.uint32)
    return lax.bitcast_convert_type((bits(hi) << 16) | bits(lo), jnp.int32)


def _unpack(words):
    hi = lax.bitcast_convert_type(words & HI_MASK, F32)
    lo = lax.bitcast_convert_type(words << 16, F32)
    return hi, lo


def _sc_dots(ids, xn, u2, n_tok):
    d = xn.shape[1]
    dw = u2.shape[1]
    nch = ids.shape[1]
    picks = nch * SC_CHUNK
    per_w = n_tok // SC_WORKERS
    assert per_w % 2 == 0 and nch % 2 == 0 and 2 * dw == d
    mesh = plsc.VectorSubcoreMesh(core_axis_name="c", subcore_axis_name="s")

    @functools.partial(
        pl.kernel, mesh=mesh, out_type=jax.ShapeDtypeStruct((n_tok, picks), F32),
        scratch_types=[pltpu.VMEM((2, nch, SC_CHUNK), jnp.int32), pltpu.VMEM((2, d), F32),
                       pltpu.VMEM((2, SC_CHUNK, dw), jnp.int32), pltpu.VMEM((2, picks), F32)]
        + [pltpu.SemaphoreType.DMA((2,))] * 4,
        compiler_params=pltpu.CompilerParams(needs_layout_passes=False), name="sc_dots")
    def k(ids_hbm, xn_hbm, u_hbm, h_hbm, idx_v, x_v, rows_v, h_v, gsem, isem, vsem, osem):
        wid = lax.axis_index("s") * SC_CORES + lax.axis_index("c")
        lane = lax.iota(jnp.int32, SC_LANES)

        def compute(p, c, b):
            for g in range(SC_CHUNK // SC_LANES):
                def rows4(q, hv):
                    j0 = g * SC_LANES + q * 4

                    def span(cc, accs):
                        for uu in range(SC_UNROLL):
                            off = pl.multiple_of((cc * SC_UNROLL + uu) * SC_LANES, SC_LANES)
                            x_hi = x_v[p, pl.ds(off, SC_LANES)]
                            x_lo = x_v[p, pl.ds(dw + off, SC_LANES)]
                            nxt = []
                            for kk in range(4):
                                hi, lo = _unpack(rows_v[b, j0 + kk, pl.ds(off, SC_LANES)])
                                nxt.append((accs[2 * kk] + hi * x_hi, accs[2 * kk + 1] + lo * x_lo))
                            accs = tuple(a for pair in nxt for a in pair)
                        return accs

                    accs = lax.fori_loop(0, dw // (SC_LANES * SC_UNROLL), span,
                                         tuple(jnp.zeros((SC_LANES,), F32) for _ in range(8)))
                    for kk in range(4):
                        hv = jnp.where(lane == q * 4 + kk, jnp.sum(accs[2 * kk] + accs[2 * kk + 1]), hv)
                    return hv

                hv = lax.fori_loop(0, SC_LANES // 4, rows4, jnp.zeros((SC_LANES,), F32))
                h_v[p, pl.ds(c * SC_CHUNK + g * SC_LANES, SC_LANES)] = hv

        _sc_token_pipeline(per_w, nch, wid, ids_hbm, xn_hbm, u_hbm, h_hbm, idx_v, x_v, rows_v, h_v,
                           gsem, isem, vsem, osem, compute)

    return k(ids, xn, u2)


def _sc_mix(ids, w, v2, n_tok):
    dw = v2.shape[1]
    d = 2 * dw
    nch = ids.shape[1]
    picks = nch * SC_CHUNK
    per_w = n_tok // SC_WORKERS
    assert per_w % 2 == 0 and nch % 2 == 0
    cb = 8
    mesh = plsc.VectorSubcoreMesh(core_axis_name="c", subcore_axis_name="s")

    @functools.partial(
        pl.kernel, mesh=mesh, out_type=jax.ShapeDtypeStruct((n_tok, d), F32),
        scratch_types=[pltpu.VMEM((2, nch, SC_CHUNK), jnp.int32), pltpu.VMEM((2, picks), F32),
                       pltpu.VMEM((2, SC_CHUNK, dw), jnp.int32), pltpu.VMEM((2, d), F32)]
        + [pltpu.SemaphoreType.DMA((2,))] * 4,
        compiler_params=pltpu.CompilerParams(needs_layout_passes=False), name="sc_mix")
    def k(ids_hbm, w_hbm, v_hbm, o_hbm, idx_v, w_v, rows_v, o_v, gsem, isem, vsem, osem):
        wid = lax.axis_index("s") * SC_CORES + lax.axis_index("c")

        def compute(p, c, b):
            for blk in range(dw // (cb * SC_LANES)):
                base = blk * cb * SC_LANES
                offs = [base + kk * SC_LANES for kk in range(cb)] + [dw + base + kk * SC_LANES for kk in range(cb)]
                if c == 0:
                    init = tuple(jnp.zeros((SC_LANES,), F32) for _ in offs)
                else:
                    init = tuple(o_v[p, pl.ds(o, SC_LANES)] for o in offs)

                def row(r, accs):
                    wj = plsc.load_gather(w_v.at[p], [jnp.full((SC_LANES,), c * SC_CHUNK, jnp.int32) + r])
                    his, los = [], []
                    for kk in range(cb):
                        hi, lo = _unpack(rows_v[b, r, pl.ds(base + kk * SC_LANES, SC_LANES)])
                        his.append(accs[kk] + wj * hi)
                        los.append(accs[cb + kk] + wj * lo)
                    return tuple(his + los)

                accs = lax.fori_loop(0, SC_CHUNK, row, init)
                for o, a in zip(offs, accs):
                    o_v[p, pl.ds(o, SC_LANES)] = a

        _sc_token_pipeline(per_w, nch, wid, ids_hbm, w_hbm, v_hbm, o_hbm, idx_v, w_v, rows_v, o_v,
                           gsem, isem, vsem, osem, compute)

    return k(ids, w, v2)


def _gate_gelu_body(h_ref, gate_ref, after_hbm, w_ref):
    del after_hbm
    w_ref[...] = gate_ref[...] * _gelu(h_ref[...])


def _gate_gelu(h, gate_tok, after):
    n, picks = h.shape
    rows = math.gcd(n, 1024)
    assert n % rows == 0
    return pl.pallas_call(
        _gate_gelu_body, grid=(n // rows,),
        in_specs=[pl.BlockSpec((rows, picks), lambda i: (i, 0)), pl.BlockSpec((rows, picks), lambda i: (i, 0)),
                  pl.BlockSpec(memory_space=pl.ANY)],
        out_specs=pl.BlockSpec((rows, picks), lambda i: (i, 0)),
        out_shape=jax.ShapeDtypeStruct((n, picks), F32), name="gate_gelu",
    )(h, gate_tok, after)


def _finish_body(x2_ref, o_ref, gfin_ref, y_ref, *, final_norm):
    y = x2_ref[...] + o_ref[...]
    y_ref[...] = _rms(y, gfin_ref[...]) if final_norm else y


def _finish(x2, o, g_final, final_norm):
    n, d = o.shape
    rows = math.gcd(n, 512)
    assert n % rows == 0
    return pl.pallas_call(
        functools.partial(_finish_body, final_norm=final_norm), grid=(n // rows,),
        in_specs=[pl.BlockSpec((rows, d), lambda i: (i, 0)), pl.BlockSpec((rows, d), lambda i: (i, 0)), _resident((1, d))],
        out_specs=pl.BlockSpec((rows, d), lambda i: (i, 0)),
        out_shape=jax.ShapeDtypeStruct((n, d), F32), name="finish",
    )(x2, o, g_final)


def _peer(ids_t, gate_t, xn, x2, g_final, lw, final_norm):
    picks, t = ids_t.shape
    ids_tok = ids_t.T
    total_steps = t // EXPERT_ROWS
    sc_steps = int(total_steps * SC_SHARE) if t >= SC_MIN_TOKENS else 0
    n_sc = sc_steps * EXPERT_ROWS
    if n_sc == 0:
        return [_experts(ids_tok, gate_t, xn, x2, g_final, lw["peer_uv"], final_norm, 0, total_steps, g_final)]
    assert n_sc % SC_WORKERS == 0 and picks % SC_CHUNK == 0
    tc_steps = total_steps - sc_steps
    first = int(tc_steps * TC_FIRST_SHARE)
    ids_sc = ids_tok.reshape(t, picks // SC_CHUNK, SC_CHUNK)
    h = _sc_dots(ids_sc, xn, lw["peer_u2"], n_sc)
    y_tc1 = _experts(ids_tok, gate_t, xn, x2, g_final, lw["peer_uv"], final_norm, sc_steps, first, g_final)
    w = _gate_gelu(h, gate_t.T, y_tc1)
    o = _sc_mix(ids_sc, w, lw["peer_v2"], n_sc)
    y_tc2 = _experts(ids_tok, gate_t, xn, x2, g_final, lw["peer_uv"], final_norm, sc_steps + first, tc_steps - first, w)
    y_sc = _finish(x2, o, g_final, final_norm)
    return [y_sc, y_tc1, y_tc2]


def _layer(x, window, conv_prev, pool_prev, start_pos, mk, mv, lw, g_final, final_norm):
    _, bn, _, seq = window
    d = x.shape[-1]
    x2, new_conv, new_pool = _mixer(x, window, conv_prev, pool_prev, start_pos, mk, mv, lw)
    x2 = x2.reshape(bn * seq, d)
    xn, ids_t, gate_t = _route(x2, lw["g_ffn"], lw["peer_wq"], lw["peer_keys"])
    y = _peer(ids_t, gate_t, xn, x2, g_final, lw, final_norm)
    return y, new_conv, new_pool


def kernel(x_prompt, x_sample, mem_prompt, cache_conv, cache_pool, cache_mem_k, cache_mem_v, g_mix, w_in, conv_w, conv_b, pool_w, pool_scale, g_mem, w_mk, w_mv, w_bc, w_bp, w_ba, gate_b, w_o, g_ffn, peer_wq, peer_keys, peer_u, peer_v, g_final):
    depth, d = g_mix.shape
    xp, xs = x_prompt, x_sample
    bp, bs = xp.shape[0], xs.shape[0]
    n_exp = peer_u.shape[1]
    gfin = g_final.reshape(1, d)
    conv_p, pool_p, mk_p, mv_p, conv_s, pool_s = [], [], [], [], [], []
    for l in range(depth):
        lw = dict(
            g_mix=g_mix[l].reshape(1, d), w_in=w_in[l].astype(BF16), conv_w=conv_w[l], conv_b=conv_b[l].reshape(1, -1),
            pool_w=pool_w[l].astype(BF16), pool_scale=pool_scale[l].reshape(1, -1), w_bc=w_bc[l].astype(BF16),
            w_bp=w_bp[l].astype(BF16), w_ba=w_ba[l].astype(BF16), gate_b=gate_b[l].reshape(1, -1), w_o=w_o[l].astype(BF16),
            g_ffn=g_ffn[l].reshape(1, d), peer_wq=peer_wq[l].astype(BF16), peer_keys=peer_keys[l].astype(BF16),
            peer_uv=_pack_halves(peer_u[l], peer_v[l]).reshape(n_exp, 1, d),
            peer_u2=_pack_pairs(peer_u[l]), peer_v2=_pack_pairs(peer_v[l]),
        )
        last = l == depth - 1
        mk, mv = _mem_kv(mem_prompt, g_mem[l], w_mk[l], w_mv[l])
        zc = jnp.zeros((bp, CONV_K - 1, conv_w.shape[-1]), xp.dtype)
        zp = jnp.zeros((bp, POOL_STATE, pool_scale.shape[-1]), xp.dtype)
        seq = xp.shape[1]
        seg = seq // PROMPT_SEGMENTS if seq % PROMPT_SEGMENTS == 0 and seq // PROMPT_SEGMENTS >= SC_MIN_TOKENS else seq
        pieces, cps, pps = [], [], []
        for b in range(bp):
            cprev, pprev = zc[b:b + 1], zp[b:b + 1]
            for s0 in range(0, seq, seg):
                ys, cprev, pprev = _layer(xp, (b, 1, s0, seg), cprev, pprev, s0, mk[b:b + 1], mv[b:b + 1], lw, gfin, last)
                pieces += ys
            cps.append(cprev); pps.append(pprev)
        xp = jnp.concatenate(pieces, axis=0).reshape(bp, seq, d)
        cp, pp = jnp.concatenate(cps, axis=0), jnp.concatenate(pps, axis=0)
        n_mem = cache_mem_k.shape[2]
        ys, cs, ps = _layer(xs, (0, bs, 0, xs.shape[1]), cache_conv[l], cache_pool[l], PAST_LEN, cache_mem_k[l].reshape(bs, n_mem, -1),
                            cache_mem_v[l].reshape(bs, n_mem, -1), lw, gfin, last)
        xs = jnp.concatenate(ys, axis=0).reshape(xs.shape)
        heads_shape = (bp, n_mem) + cache_mem_k.shape[3:]
        conv_p.append(cp); pool_p.append(pp); mk_p.append(mk.reshape(heads_shape)); mv_p.append(mv.reshape(heads_shape))
        conv_s.append(cs); pool_s.append(ps)
    return (xp, xs, jnp.stack(conv_p), jnp.stack(pool_p), jnp.stack(mk_p), jnp.stack(mv_p),
            jnp.stack(conv_s), jnp.stack(pool_s))
```

```python
import functools
import math

import jax
import jax.numpy as jnp
from jax import lax
from jax.experimental import pallas as pl
from jax.experimental.pallas import tpu as pltpu
from jax.experimental.pallas import tpu_sc as plsc

F32 = jnp.float32
BF16 = jnp.bfloat16
EPS = 1e-6

CONV_K = 3
POOL_WINDOWS = (2, 4, 8, 16)
POOL_STATE = max(POOL_WINDOWS) - 1
PAST_LEN = 1024
MEM_HEADS = 4
PEER_HEADS = 8
PEER_TOPK = 16

SUBLANES = 8
LANES = 128
MIXER_ROWS = 512
ROUTE_ROWS = 256
EXPERT_ROWS = 128
GATHER_SLOTS = 8
VMEM_LIMIT = 56 * 1024 * 1024

SC_CORES = 2
SC_SUBCORES = 16
SC_WORKERS = SC_CORES * SC_SUBCORES
SC_LANES = 16
SC_CHUNK = 64
SC_UNROLL = 16
SC_SHARE = 0.66
SC_MIN_TOKENS = 8192
PROMPT_SEGMENTS = 2
TC_FIRST_SHARE = 0.56


def _rms(x, g):
    return x * lax.rsqrt(jnp.mean(x * x, axis=-1, keepdims=True) + EPS) * g


def _dot(a, b):
    return jnp.dot(a, b, preferred_element_type=F32)


def _dot_nt(a, b):
    return lax.dot_general(a, b, (((1,), (1,)), ((), ())), preferred_element_type=F32)


def _resident(shape):
    zeros = (0,) * len(shape)
    return pl.BlockSpec(shape, lambda *_: zeros, pipeline_mode=pl.Buffered(1))


def _memkv_body(mem_ref, g_ref, wk_ref, wv_ref, k_ref, v_ref):
    m = _rms(mem_ref[...], g_ref[...]).astype(BF16)
    k_ref[...] = _dot(m, wk_ref[...])
    v_ref[...] = _dot(m, wv_ref[...])


def _mem_kv(mem, g_mem, w_mk, w_mv):
    bn, n_mem, d = mem.shape
    att_w = w_mk.shape[1]
    k, v = pl.pallas_call(
        _memkv_body,
        out_shape=[jax.ShapeDtypeStruct((bn * n_mem, att_w), F32)] * 2,
        name="mem_kv",
    )(mem.reshape(bn * n_mem, d), g_mem.reshape(1, d), w_mk.astype(BF16), w_mv.astype(BF16))
    return k.reshape(bn, n_mem, att_w), v.reshape(bn, n_mem, att_w)


def _mixer_body(x_ref, cprev_ref, pprev_ref, mk_ref, mv_ref, gmix_ref, win_ref, convw_ref, convb_ref, poolw_ref,
                pscale_ref, wbc_ref, wbp_ref, wba_ref, gateb_ref, wo_ref,
                x2_ref, nconv_ref, npool_ref, zbuf, ubuf, *, rows, start_pos, conv_w, pool_w, att_w):
    s = pl.program_id(1)
    d_model = x_ref.shape[-1]
    z0 = SUBLANES
    u0 = 2 * SUBLANES

    @pl.when(s == 0)
    def _():
        zbuf[z0 - (CONV_K - 1):z0, :] = cprev_ref[0]
        ubuf[u0 - POOL_STATE:u0, :] = pprev_ref[0]

    x = x_ref[0]
    hb = _rms(x, gmix_ref[...]).astype(BF16)

    c0 = 0
    pa = _dot(hb, win_ref[:, c0:c0 + 3 * conv_w])
    xc, bg, cg = pa[:, :conv_w], pa[:, conv_w:2 * conv_w], pa[:, 2 * conv_w:]
    zbuf[z0:z0 + rows, :] = cg * xc
    cw = convw_ref[...]
    conv = zbuf[z0 - 2:z0 - 2 + rows, :] * cw[0:1]
    for k in range(1, CONV_K):
        conv = conv + zbuf[z0 - 2 + k:z0 - 2 + k + rows, :] * cw[k:k + 1]
    conv = conv + convb_ref[...]
    ya = _dot((bg * conv).astype(BF16), wbc_ref[...])
    last_z = zbuf[z0 + rows - (CONV_K - 1):z0 + rows, :]
    nconv_ref[0] = last_z
    zbuf[z0 - (CONV_K - 1):z0, :] = last_z
    c0 += 3 * conv_w

    up = _dot(hb, win_ref[:, c0:c0 + pool_w])
    ubuf[u0:u0 + rows, :] = up
    pos = start_pos + s * rows + lax.broadcasted_iota(jnp.int32, (rows, 1), 0)
    gw = pool_w // len(POOL_WINDOWS)
    ys = []
    for g, w in enumerate(POOL_WINDOWS):
        cur = up[:, g * gw:(g + 1) * gw]
        acc = cur
        for k in range(1, w):
            acc = acc + ubuf[u0 - k:u0 - k + rows, g * gw:(g + 1) * gw]
        cnt = jnp.minimum(pos + 1, w).astype(F32)
        ys.append(_dot((acc / cnt - cur).astype(BF16), poolw_ref[g]))
    yb = _dot((jnp.concatenate(ys, axis=-1) * pscale_ref[...]).astype(BF16), wbp_ref[...])
    last_u = ubuf[u0 + rows - POOL_STATE:u0 + rows, :]
    npool_ref[0] = last_u
    ubuf[u0 - POOL_STATE:u0, :] = last_u
    c0 += pool_w

    q = _dot(hb, win_ref[:, c0:c0 + att_w])
    kb = mk_ref[0].astype(BF16)
    vb = mv_ref[0].astype(BF16)
    hd = att_w // MEM_HEADS
    heads = []
    for h in range(MEM_HEADS):
        sc = _dot_nt(q[:, h * hd:(h + 1) * hd].astype(BF16), kb[:, h * hd:(h + 1) * hd]) * (hd ** -0.5)
        e = jnp.exp(sc - jnp.max(sc, axis=-1, keepdims=True))
        p = e / jnp.sum(e, axis=-1, keepdims=True)
        heads.append(_dot(p.astype(BF16), vb[:, h * hd:(h + 1) * hd]))
    yc = _dot(jnp.concatenate(heads, axis=-1).astype(BF16), wba_ref[...])
    c0 += att_w

    merged = None
    for i, y in enumerate((ya, yb, yc)):
        gl = _dot(hb, win_ref[:, c0 + i * d_model:c0 + (i + 1) * d_model]) + gateb_ref[:, i * d_model:(i + 1) * d_model]
        term = (1.0 / (1.0 + jnp.exp(-gl))) * y
        merged = term if merged is None else merged + term
    x2_ref[0] = x + _dot(merged.astype(BF16), wo_ref[...])


def _mixer(x, window, conv_prev, pool_prev, start_pos, mk, mv, lw):
    b0, bn, s0, seq = window
    d = x.shape[-1]
    conv_w, pool_w = conv_prev.shape[-1], pool_prev.shape[-1]
    n_mem, att_w = mk.shape[1], mk.shape[2]
    rows = min(MIXER_ROWS, seq)
    assert seq % rows == 0 and s0 % rows == 0 and rows % SUBLANES == 0 and rows >= POOL_STATE
    row0 = s0 // rows
    in_cols = lw["w_in"].shape[1]
    per_b = lambda shape: pl.BlockSpec((1,) + shape, lambda b, s: (b, 0, 0))
    body = functools.partial(_mixer_body, rows=rows, start_pos=start_pos, conv_w=conv_w, pool_w=pool_w, att_w=att_w)
    return pl.pallas_call(
        body,
        grid=(bn, seq // rows),
        in_specs=[
            pl.BlockSpec((1, rows, d), lambda b, s: (b0 + b, row0 + s, 0)),
            per_b((CONV_K - 1, conv_w)), per_b((POOL_STATE, pool_w)), per_b((n_mem, att_w)), per_b((n_mem, att_w)),
            _resident((1, d)), _resident((d, in_cols)), _resident((CONV_K, conv_w)), _resident((1, conv_w)),
            _resident(lw["pool_w"].shape), _resident((1, pool_w)), _resident((conv_w, d)), _resident((pool_w, d)),
            _resident((att_w, d)), _resident((1, 3 * d)), _resident((d, d)),
        ],
        out_specs=[
            pl.BlockSpec((1, rows, d), lambda b, s: (b, s, 0)),
            per_b((CONV_K - 1, conv_w)), per_b((POOL_STATE, pool_w)),
        ],
        out_shape=[
            jax.ShapeDtypeStruct((bn, seq, d), F32),
            jax.ShapeDtypeStruct((bn, CONV_K - 1, conv_w), F32),
            jax.ShapeDtypeStruct((bn, POOL_STATE, pool_w), F32),
        ],
        scratch_shapes=[pltpu.VMEM((SUBLANES + rows, conv_w), F32), pltpu.VMEM((2 * SUBLANES + rows, pool_w), F32)],
        compiler_params=pltpu.CompilerParams(dimension_semantics=("arbitrary", "arbitrary"), vmem_limit_bytes=VMEM_LIMIT),
        name="mixer",
    )(x, conv_prev, pool_prev, mk, mv, lw["g_mix"], lw["w_in"], lw["conv_w"], lw["conv_b"], lw["pool_w"],
      lw["pool_scale"], lw["w_bc"], lw["w_bp"], lw["w_ba"], lw["gate_b"], lw["w_o"])


_STAIR = [(a, b) for a in range(PEER_TOPK) for b in range(PEER_TOPK) if (a + 1) * (b + 1) <= PEER_TOPK]
_STAIR_HEAD = 24
_STAIR_TAIL = [_STAIR[_STAIR_HEAD + 8 * i:_STAIR_HEAD + 8 * (i + 1)] for i in range(4)]


def _top16(v, payload=None):
    n, t = v.shape
    rows = lax.broadcasted_iota(jnp.int32, (n, t), 0)
    out_rows = lax.broadcasted_iota(jnp.int32, (PEER_TOPK, t), 0)
    vals = jnp.zeros((PEER_TOPK, t), F32)
    pay = jnp.zeros((PEER_TOPK, t), jnp.int32)
    for k in range(PEER_TOPK):
        m = jnp.max(v, axis=0, keepdims=True)
        am = jnp.min(jnp.where(v == m, rows, n), axis=0, keepdims=True)
        hit = rows == am
        p = am if payload is None else jnp.sum(jnp.where(hit, payload, 0), axis=0, keepdims=True)
        vals = jnp.where(out_rows == k, m, vals)
        pay = jnp.where(out_rows == k, p, pay)
        v = jnp.where(hit, -jnp.inf, v)
    return vals, pay


def _take_rows(src, idxs):
    t = src.shape[1]
    out_rows = lax.broadcasted_iota(jnp.int32, (SUBLANES, t), 0)
    acc = jnp.broadcast_to(src[idxs[0]:idxs[0] + 1, :], (SUBLANES, t))
    for i in range(1, SUBLANES):
        acc = jnp.where(out_rows == i, src[idxs[i]:idxs[i] + 1, :], acc)
    return acc


def _stair(first, second, combine):
    blocks = [combine(first[0:1, :], second), combine(first[1:2, :], second[0:SUBLANES, :])]
    for blk in _STAIR_TAIL:
        blk = blk + [(0, 0)] * (SUBLANES - len(blk))
        blocks.append(combine(_take_rows(first, [a for a, _ in blk]), _take_rows(second, [b for _, b in blk])))
    return jnp.concatenate(blocks, axis=0)


def _route_body(x2_ref, gffn_ref, wq_ref, keys_ref, xn_ref, ids_ref, gate_ref, *, n_keys):
    xn = _rms(x2_ref[...], gffn_ref[...])
    xn_ref[...] = xn
    q = _dot(xn.astype(BF16), wq_ref[...])
    half = keys_ref.shape[-1]
    t = q.shape[0]
    for h in range(PEER_HEADS):
        scores = [_dot_nt(keys_ref[p], q[:, (2 * h + p) * half:(2 * h + p + 1) * half].astype(BF16)) for p in range(2)]
        for l0 in range(0, t, LANES):
            (s1, i1), (s2, i2) = [_top16(sc[:, l0:l0 + LANES]) for sc in scores]
            cand = _stair(s1, s2, lambda a, b: a + b)
            pad = lax.broadcasted_iota(jnp.int32, cand.shape, 0) >= len(_STAIR)
            expert = _stair(i1, i2, lambda a, b: a * n_keys + b)
            sc, e = _top16(jnp.where(pad, -jnp.inf, cand), expert)
            ex = jnp.exp(sc - sc[0:1, :])
            ids_ref[h * PEER_TOPK:(h + 1) * PEER_TOPK, l0:l0 + LANES] = e
            gate_ref[h * PEER_TOPK:(h + 1) * PEER_TOPK, l0:l0 + LANES] = ex / jnp.sum(ex, axis=0, keepdims=True)


def _route(x2, g_ffn, wq, keys):
    t, d = x2.shape
    rows = ROUTE_ROWS
    assert t % rows == 0
    picks = PEER_HEADS * PEER_TOPK
    return pl.pallas_call(
        functools.partial(_route_body, n_keys=keys.shape[1]),
        grid=(t // rows,),
        in_specs=[pl.BlockSpec((rows, d), lambda i: (i, 0)), _resident((1, d)), _resident(wq.shape), _resident(keys.shape)],
        out_specs=[pl.BlockSpec((rows, d), lambda i: (i, 0)), pl.BlockSpec((picks, rows), lambda i: (0, i)),
                   pl.BlockSpec((picks, rows), lambda i: (0, i))],
        out_shape=[jax.ShapeDtypeStruct((t, d), F32), jax.ShapeDtypeStruct((picks, t), jnp.int32),
                   jax.ShapeDtypeStruct((picks, t), F32)],
        compiler_params=pltpu.CompilerParams(dimension_semantics=("arbitrary",), vmem_limit_bytes=VMEM_LIMIT),
        name="route",
    )(x2, g_ffn, wq, keys)


def _gelu(h):
    return 0.5 * h * (1.0 + lax.erf(h * (2.0 ** -0.5)))


def _experts_body(ids_hbm, gate_ref, xn_ref, x2_ref, gfin_ref, uv_hbm, after_hbm, y_ref, *scratch,
                  rows, picks, first_step, steps, final_norm):
    del after_hbm
    bufs, (sem, ids_smem, ids_sem) = scratch[:GATHER_SLOTS], scratch[GATHER_SLOTS:]
    i = pl.program_id(0)
    half = i % 2
    n = rows * picks
    d = xn_ref.shape[-1]
    lead = GATHER_SLOTS - 2
    groups = picks // SUBLANES

    def ids_copy(step, which):
        return pltpu.make_async_copy(ids_hbm.at[first_step + step], ids_smem.at[pl.ds(which * n, n)], ids_sem.at[which])

    def issue(off, slot, j0, j1):
        for j in range(j0, j1):
            pltpu.async_copy(uv_hbm.at[ids_smem[off + j]], bufs[slot].at[pl.ds(j, 1)], sem.at[slot], priority=j % 2)

    def wait_slot(slot):
        pltpu.make_async_copy(bufs[slot], bufs[slot], sem.at[slot]).wait()

    @pl.when(i == 0)
    def _():
        ids_copy(0, 0).start()
        ids_copy(0, 0).wait()
        if steps > 1:
            ids_copy(1, 1).start()
        for s in range(lead):
            issue(s * picks, s, 0, picks)

    lane = lax.broadcasted_iota(jnp.int32, (picks, rows), 1)

    def weighted_sum(t, slot, w):
        buf = bufs[slot]
        acc = None
        for k in range(groups):
            r0 = k * SUBLANES
            v_rows = lax.bitcast_convert_type(buf[r0:r0 + SUBLANES, :] << 16, F32)
            part = w[r0:r0 + SUBLANES, :] * v_rows
            acc = part if acc is None else acc + part
        y = x2_ref[pl.ds(t, 1), :] + jnp.sum(acc, axis=0, keepdims=True)
        y_ref[pl.ds(t, 1), :] = _rms(y, gfin_ref[...]) if final_norm else y

    def token(t, slot, ahead_off, w_prev):
        buf = bufs[slot]
        wait_slot(slot)
        x = xn_ref[pl.ds(t, 1), :]
        g = jnp.sum(jnp.where(lane == t, gate_ref[...], 0.0), axis=-1, keepdims=True)
        hs = []
        for k in range(groups):
            r0 = k * SUBLANES
            u_rows = lax.bitcast_convert_type(buf[r0:r0 + SUBLANES, :] & HI_MASK, F32)
            hs.append(jnp.sum(u_rows * x, axis=-1, keepdims=True))
        if w_prev is not None:
            weighted_sum(t - 1, (slot - 1) % GATHER_SLOTS, w_prev)
        if ahead_off is not None:
            issue(ahead_off, (slot + lead) % GATHER_SLOTS, 0, picks)
        return g * _gelu(jnp.concatenate(hs, axis=0))

    w = None
    for s in range(GATHER_SLOTS):
        w = token(s, s, half * n + (s + lead) * picks, w)

    def slot_round(k, w):
        for s in range(GATHER_SLOTS):
            t = k * GATHER_SLOTS + s
            w = token(t, s, half * n + (t + lead) * picks, w)
        return w

    w = lax.fori_loop(1, rows // GATHER_SLOTS - 1, slot_round, w)

    t0 = rows - GATHER_SLOTS
    own = GATHER_SLOTS - lead
    for s in range(own):
        w = token(t0 + s, s, half * n + (t0 + s + lead) * picks, w)

    @pl.when(i + 1 < steps)
    def _():
        ids_copy(i + 1, 1 - half).wait()

    for s in range(own, GATHER_SLOTS):
        @pl.when(i + 1 < steps)
        def _():
            issue((1 - half) * n + (s - own) * picks, (s + lead) % GATHER_SLOTS, 0, picks)

        w = token(t0 + s, s, None, w)
    weighted_sum(rows - 1, GATHER_SLOTS - 1, w)

    @pl.when(i + 2 < steps)
    def _():
        ids_copy(i + 2, half).start()


def _experts(ids_tok, gate_t, xn, x2, g_final, uv, final_norm, first_step, steps, after):
    t, picks = ids_tok.shape
    d = xn.shape[1]
    rows = EXPERT_ROWS
    assert t % rows == 0 and rows >= 2 * GATHER_SLOTS and rows % GATHER_SLOTS == 0 and picks % (2 * SUBLANES) == 0
    ids = ids_tok.reshape(t // rows, rows * picks)
    body = functools.partial(_experts_body, rows=rows, picks=picks, first_step=first_step, steps=steps,
                             final_norm=final_norm)
    return pl.pallas_call(
        body,
        grid=(steps,),
        in_specs=[
            pl.BlockSpec(memory_space=pl.ANY),
            pl.BlockSpec((picks, rows), lambda i: (0, first_step + i)),
            pl.BlockSpec((rows, d), lambda i: (first_step + i, 0)),
            pl.BlockSpec((rows, d), lambda i: (first_step + i, 0)),
            _resident((1, d)),
            pl.BlockSpec(memory_space=pl.ANY),
            pl.BlockSpec(memory_space=pl.ANY),
        ],
        out_specs=pl.BlockSpec((rows, d), lambda i: (i, 0)),
        out_shape=jax.ShapeDtypeStruct((steps * rows, d), F32),
        scratch_shapes=[pltpu.VMEM((picks, d), jnp.int32)] * GATHER_SLOTS + [
            pltpu.SemaphoreType.DMA((GATHER_SLOTS,)),
            pltpu.SMEM((2 * rows * picks,), jnp.int32),
            pltpu.SemaphoreType.DMA((2,)),
        ],
        compiler_params=pltpu.CompilerParams(dimension_semantics=("arbitrary",), vmem_limit_bytes=VMEM_LIMIT),
        name="experts",
    )(ids, gate_t, xn, x2, g_final, uv, after)


def _sc_token_pipeline(per_w, nch, wid, ids_hbm, vec_hbm, tab_hbm, out_hbm, idx_v, vec_v, rows_v, out_v,
                       gsem, isem, vsem, osem, compute):
    base = wid * per_w

    def ids_copy(t, p):
        return pltpu.make_async_copy(ids_hbm.at[t], idx_v.at[p], isem.at[p])

    def vec_copy(t, p):
        return pltpu.make_async_copy(vec_hbm.at[t], vec_v.at[p], vsem.at[p])

    def out_copy(t, p):
        return pltpu.make_async_copy(out_v.at[p], out_hbm.at[t], osem.at[p])

    def gather(p, c, b):
        return pltpu.make_async_copy(tab_hbm.at[idx_v.at[p, c]], rows_v.at[b], gsem.at[b])

    ids_copy(base, 0).start()
    vec_copy(base, 0).start()
    ids_copy(base, 0).wait()
    vec_copy(base, 0).wait()
    gather(0, 0, 0).start()

    def pair(ii, carry):
        for p in range(2):
            i = 2 * ii + p
            t = base + i
            has_next = i + 1 < per_w

            @pl.when(has_next)
            def _():
                ids_copy(t + 1, 1 - p).start()
                vec_copy(t + 1, 1 - p).start()

            @pl.when(i >= 2)
            def _():
                out_copy(t - 2, p).wait()

            for c in range(nch):
                b = c % 2
                if c + 1 < nch:
                    gather(p, c + 1, 1 - b).start()
                else:
                    @pl.when(has_next)
                    def _():
                        ids_copy(t + 1, 1 - p).wait()
                        vec_copy(t + 1, 1 - p).wait()
                        gather(1 - p, 0, 1 - b).start()
                gather(p, c, b).wait()
                compute(p, c, b)
            out_copy(t, p).start()
        return carry

    lax.fori_loop(0, per_w // 2, pair, 0)
    out_copy(base + per_w - 2, 0).wait()
    out_copy(base + per_w - 1, 1).wait()


HI_MASK = -65536


def _pack_pairs(tab):
    d = tab.shape[1]
    return _pack_halves(tab[:, :d // 2], tab[:, d // 2:])


def _pack_halves(hi, lo):
    bits = lambda a: lax.bitcast_convert_type(a.astype(jnp.bfloat16), jnp.uint16).astype(jnp.uint32)
    return lax.bitcast_convert_type((bits(hi) << 16) | bits(lo), jnp.int32)


def _unpack(words):
    hi = lax.bitcast_convert_type(words & HI_MASK, F32)
    lo = lax.bitcast_convert_type(words << 16, F32)
    return hi, lo


def _sc_dots(ids, xn, u2, n_tok):
    d = xn.shape[1]
    dw = u2.shape[1]
    nch = ids.shape[1]
    picks = nch * SC_CHUNK
    per_w = n_tok // SC_WORKERS
    assert per_w % 2 == 0 and nch % 2 == 0 and 2 * dw == d
    mesh = plsc.VectorSubcoreMesh(core_axis_name="c", subcore_axis_name="s")

    @functools.partial(
        pl.kernel, mesh=mesh, out_type=jax.ShapeDtypeStruct((n_tok, picks), F32),
        scratch_types=[pltpu.VMEM((2, nch, SC_CHUNK), jnp.int32), pltpu.VMEM((2, d), F32),
                       pltpu.VMEM((2, SC_CHUNK, dw), jnp.int32), pltpu.VMEM((2, picks), F32)]
        + [pltpu.SemaphoreType.DMA((2,))] * 4,
        compiler_params=pltpu.CompilerParams(needs_layout_passes=False), name="sc_dots")
    def k(ids_hbm, xn_hbm, u_hbm, h_hbm, idx_v, x_v, rows_v, h_v, gsem, isem, vsem, osem):
        wid = lax.axis_index("s") * SC_CORES + lax.axis_index("c")
        lane = lax.iota(jnp.int32, SC_LANES)

        def compute(p, c, b):
            for g in range(SC_CHUNK // SC_LANES):
                def rows4(q, hv):
                    j0 = g * SC_LANES + q * 4

                    def span(cc, accs):
                        for uu in range(SC_UNROLL):
                            off = pl.multiple_of((cc * SC_UNROLL + uu) * SC_LANES, SC_LANES)
                            x_hi = x_v[p, pl.ds(off, SC_LANES)]
                            x_lo = x_v[p, pl.ds(dw + off, SC_LANES)]
                            nxt = []
                            for kk in range(4):
                                hi, lo = _unpack(rows_v[b, j0 + kk, pl.ds(off, SC_LANES)])
                                nxt.append((accs[2 * kk] + hi * x_hi, accs[2 * kk + 1] + lo * x_lo))
                            accs = tuple(a for pair in nxt for a in pair)
                        return accs

                    accs = lax.fori_loop(0, dw // (SC_LANES * SC_UNROLL), span,
                                         tuple(jnp.zeros((SC_LANES,), F32) for _ in range(8)))
                    for kk in range(4):
                        hv = jnp.where(lane == q * 4 + kk, jnp.sum(accs[2 * kk] + accs[2 * kk + 1]), hv)
                    return hv

                hv = lax.fori_loop(0, SC_LANES // 4, rows4, jnp.zeros((SC_LANES,), F32))
                h_v[p, pl.ds(c * SC_CHUNK + g * SC_LANES, SC_LANES)] = hv

        _sc_token_pipeline(per_w, nch, wid, ids_hbm, xn_hbm, u_hbm, h_hbm, idx_v, x_v, rows_v, h_v,
                           gsem, isem, vsem, osem, compute)

    return k(ids, xn, u2)


def _sc_mix(ids, w, v2, n_tok):
    dw = v2.shape[1]
    d = 2 * dw
    nch = ids.shape[1]
    picks = nch * SC_CHUNK
    per_w = n_tok // SC_WORKERS
    assert per_w % 2 == 0 and nch % 2 == 0
    cb = 8
    mesh = plsc.VectorSubcoreMesh(core_axis_name="c", subcore_axis_name="s")

    @functools.partial(
        pl.kernel, mesh=mesh, out_type=jax.ShapeDtypeStruct((n_tok, d), F32),
        scratch_types=[pltpu.VMEM((2, nch, SC_CHUNK), jnp.int32), pltpu.VMEM((2, picks), F32),
                       pltpu.VMEM((2, SC_CHUNK, dw), jnp.int32), pltpu.VMEM((2, d), F32)]
        + [pltpu.SemaphoreType.DMA((2,))] * 4,
        compiler_params=pltpu.CompilerParams(needs_layout_passes=False), name="sc_mix")
    def k(ids_hbm, w_hbm, v_hbm, o_hbm, idx_v, w_v, rows_v, o_v, gsem, isem, vsem, osem):
        wid = lax.axis_index("s") * SC_CORES + lax.axis_index("c")

        def compute(p, c, b):
            for blk in range(dw // (cb * SC_LANES)):
                base = blk * cb * SC_LANES
                offs = [base + kk * SC_LANES for kk in range(cb)] + [dw + base + kk * SC_LANES for kk in range(cb)]
                if c == 0:
                    init = tuple(jnp.zeros((SC_LANES,), F32) for _ in offs)
                else:
                    init = tuple(o_v[p, pl.ds(o, SC_LANES)] for o in offs)

                def row(r, accs):
                    wj = plsc.load_gather(w_v.at[p], [jnp.full((SC_LANES,), c * SC_CHUNK, jnp.int32) + r])
                    his, los = [], []
                    for kk in range(cb):
                        hi, lo = _unpack(rows_v[b, r, pl.ds(base + kk * SC_LANES, SC_LANES)])
                        his.append(accs[kk] + wj * hi)
                        los.append(accs[cb + kk] + wj * lo)
                    return tuple(his + los)

                accs = lax.fori_loop(0, SC_CHUNK, row, init)
                for o, a in zip(offs, accs):
                    o_v[p, pl.ds(o, SC_LANES)] = a

        _sc_token_pipeline(per_w, nch, wid, ids_hbm, w_hbm, v_hbm, o_hbm, idx_v, w_v, rows_v, o_v,
                           gsem, isem, vsem, osem, compute)

    return k(ids, w, v2)


def _gate_gelu_body(h_ref, gate_ref, after_hbm, w_ref):
    del after_hbm
    w_ref[...] = gate_ref[...] * _gelu(h_ref[...])


def _gate_gelu(h, gate_tok, after):
    n, picks = h.shape
    rows = math.gcd(n, 1024)
    assert n % rows == 0
    return pl.pallas_call(
        _gate_gelu_body, grid=(n // rows,),
        in_specs=[pl.BlockSpec((rows, picks), lambda i: (i, 0)), pl.BlockSpec((rows, picks), lambda i: (i, 0)),
                  pl.BlockSpec(memory_space=pl.ANY)],
        out_specs=pl.BlockSpec((rows, picks), lambda i: (i, 0)),
        out_shape=jax.ShapeDtypeStruct((n, picks), F32), name="gate_gelu",
    )(h, gate_tok, after)


def _finish_body(x2_ref, o_ref, gfin_ref, y_ref, *, final_norm):
    y = x2_ref[...] + o_ref[...]
    y_ref[...] = _rms(y, gfin_ref[...]) if final_norm else y


def _finish(x2, o, g_final, final_norm):
    n, d = o.shape
    rows = math.gcd(n, 512)
    assert n % rows == 0
    return pl.pallas_call(
        functools.partial(_finish_body, final_norm=final_norm), grid=(n // rows,),
        in_specs=[pl.BlockSpec((rows, d), lambda i: (i, 0)), pl.BlockSpec((rows, d), lambda i: (i, 0)), _resident((1, d))],
        out_specs=pl.BlockSpec((rows, d), lambda i: (i, 0)),
        out_shape=jax.ShapeDtypeStruct((n, d), F32), name="finish",
    )(x2, o, g_final)


def _peer(ids_t, gate_t, xn, x2, g_final, lw, final_norm):
    picks, t = ids_t.shape
    ids_tok = ids_t.T
    total_steps = t // EXPERT_ROWS
    sc_steps = int(total_steps * SC_SHARE) if t >= SC_MIN_TOKENS else 0
    n_sc = sc_steps * EXPERT_ROWS
    if n_sc == 0:
        return [_experts(ids_tok, gate_t, xn, x2, g_final, lw["peer_uv"], final_norm, 0, total_steps, g_final)]
    assert n_sc % SC_WORKERS == 0 and picks % SC_CHUNK == 0
    tc_steps = total_steps - sc_steps
    first = int(tc_steps * TC_FIRST_SHARE)
    ids_sc = ids_tok.reshape(t, picks // SC_CHUNK, SC_CHUNK)
    h = _sc_dots(ids_sc, xn, lw["peer_u2"], n_sc)
    y_tc1 = _experts(ids_tok, gate_t, xn, x2, g_final, lw["peer_uv"], final_norm, sc_steps, first, g_final)
    w = _gate_gelu(h, gate_t.T, y_tc1)
    o = _sc_mix(ids_sc, w, lw["peer_v2"], n_sc)
    y_tc2 = _experts(ids_tok, gate_t, xn, x2, g_final, lw["peer_uv"], final_norm, sc_steps + first, tc_steps - first, w)
    y_sc = _finish(x2, o, g_final, final_norm)
    return [y_sc, y_tc1, y_tc2]


def _layer(x, window, conv_prev, pool_prev, start_pos, mk, mv, lw, g_final, final_norm):
    _, bn, _, seq = window
    d = x.shape[-1]
    x2, new_conv, new_pool = _mixer(x, window, conv_prev, pool_prev, start_pos, mk, mv, lw)
    x2 = x2.reshape(bn * seq, d)
    xn, ids_t, gate_t = _route(x2, lw["g_ffn"], lw["peer_wq"], lw["peer_keys"])
    y = _peer(ids_t, gate_t, xn, x2, g_final, lw, final_norm)
    return y, new_conv, new_pool


def kernel(x_prompt, x_sample, mem_prompt, cache_conv, cache_pool, cache_mem_k, cache_mem_v, g_mix, w_in, conv_w, conv_b, pool_w, pool_scale, g_mem, w_mk, w_mv, w_bc, w_bp, w_ba, gate_b, w_o, g_ffn, peer_wq, peer_keys, peer_u, peer_v, g_final):
    depth, d = g_mix.shape
    xp, xs = x_prompt, x_sample
    bp, bs = xp.shape[0], xs.shape[0]
    n_exp = peer_u.shape[1]
    gfin = g_final.reshape(1, d)
    conv_p, pool_p, mk_p, mv_p, conv_s, pool_s = [], [], [], [], [], []
    for l in range(depth):
        lw = dict(
            g_mix=g_mix[l].reshape(1, d), w_in=w_in[l].astype(BF16), conv_w=conv_w[l], conv_b=conv_b[l].reshape(1, -1),
            pool_w=pool_w[l].astype(BF16), pool_scale=pool_scale[l].reshape(1, -1), w_bc=w_bc[l].astype(BF16),
            w_bp=w_bp[l].astype(BF16), w_ba=w_ba[l].astype(BF16), gate_b=gate_b[l].reshape(1, -1), w_o=w_o[l].astype(BF16),
            g_ffn=g_ffn[l].reshape(1, d), peer_wq=peer_wq[l].astype(BF16), peer_keys=peer_keys[l].astype(BF16),
            peer_uv=_pack_halves(peer_u[l], peer_v[l]).reshape(n_exp, 1, d),
            peer_u2=_pack_pairs(peer_u[l]), peer_v2=_pack_pairs(peer_v[l]),
        )
        last = l == depth - 1
        mk, mv = _mem_kv(mem_prompt, g_mem[l], w_mk[l], w_mv[l])
        zc = jnp.zeros((bp, CONV_K - 1, conv_w.shape[-1]), xp.dtype)
        zp = jnp.zeros((bp, POOL_STATE, pool_scale.shape[-1]), xp.dtype)
        seq = xp.shape[1]
        seg = seq // PROMPT_SEGMENTS if seq % PROMPT_SEGMENTS == 0 and seq // PROMPT_SEGMENTS >= SC_MIN_TOKENS else seq
        pieces, cps, pps = [], [], []
        for b in range(bp):
            cprev, pprev = zc[b:b + 1], zp[b:b + 1]
            for s0 in range(0, seq, seg):
                ys, cprev, pprev = _layer(xp, (b, 1, s0, seg), cprev, pprev, s0, mk[b:b + 1], mv[b:b + 1], lw, gfin, last)
                pieces += ys
            cps.append(cprev); pps.append(pprev)
        xp = jnp.concatenate(pieces, axis=0).reshape(bp, seq, d)
        cp, pp = jnp.concatenate(cps, axis=0), jnp.concatenate(pps, axis=0)
        n_mem = cache_mem_k.shape[2]
        ys, cs, ps = _layer(xs, (0, bs, 0, xs.shape[1]), cache_conv[l], cache_pool[l], PAST_LEN, cache_mem_k[l].reshape(bs, n_mem, -1),
                            cache_mem_v[l].reshape(bs, n_mem, -1), lw, gfin, last)
        xs = jnp.concatenate(ys, axis=0).reshape(xs.shape)
        heads_shape = (bp, n_mem) + cache_mem_k.shape[3:]
        conv_p.append(cp); pool_p.append(pp); mk_p.append(mk.reshape(heads_shape)); mv_p.append(mv.reshape(heads_shape))
        conv_s.append(cs); pool_s.append(ps)
    return (xp, xs, jnp.stack(conv_p), jnp.stack(pool_p), jnp.stack(mk_p), jnp.stack(mv_p),
            jnp.stack(conv_s), jnp.stack(pool_s))
```
